```python
import jax, jax.numpy as jnp
from jax import lax
import numpy as np

D_MODEL = 1024
BATCH = 4
SEQ = 8192
DEPTH = 1

GRID_W = 64
CTX_LEN = 256
HGRN_HEADS = 4
HGRN_DK = 128
HGRN_DV = 128
HGRN_WIDTH = HGRN_HEADS * HGRN_DK
HGRN_CHUNK = 64
SGU_GROUPS = 4
SGU_CHUNK = 128
SGU_WIDTH = 512
SGU_GROUP_DIM = SGU_WIDTH // SGU_GROUPS
ROWS_PER_SGU_CHUNK = SGU_CHUNK // GRID_W
N_EXPERTS = 16
EXPERT_FF = 2048
CAPACITY_FACTOR = 2
N_MOD = 6
EPS = 1e-6
IN_COLS = 5 * HGRN_WIDTH + 2 * SGU_WIDTH + 2 * D_MODEL

kernel_name = 'hybrid_hgrn2_sgu_ecmoe_dit_layer'


def rmsnorm(x, w):
    xf = x.astype(jnp.float32)
    y = xf * lax.rsqrt(jnp.mean(xf * xf, axis=-1, keepdims=True) + EPS)
    return (y * w.astype(jnp.float32)).astype(x.dtype)


def adaln(cond, w, b):
    m = jax.nn.silu(cond) @ w + b
    m = m.reshape((-1, 1, N_MOD * D_MODEL))
    return jnp.split(m, N_MOD, axis=-1)


def modulate(h, shift, scale):
    return h * (1 + scale) + shift


def split_in(p):
    sizes = (HGRN_WIDTH,) * 5 + (SGU_WIDTH,) * 2 + (D_MODEL,) * 2
    idx = []
    acc = 0
    for s in sizes[:-1]:
        acc += s
        idx.append(acc)
    return jnp.split(p, idx, axis=-1)


def to_heads(a):
    return a.reshape(a.shape[0], a.shape[1], HGRN_HEADS, HGRN_DK)


def hgrn_forget(z, lb):
    f = lb + (1 - lb) * jax.nn.sigmoid(z.astype(jnp.float32))
    return jnp.log(f), 1 - f


def hgrn_chunk_scan(q, k, logf, v, s0):
    B_, T, H, DK = q.shape
    DV = v.shape[-1]
    nc = T // HGRN_CHUNK

    def to_chunks(a):
        return a.astype(jnp.float32).reshape(B_, nc, HGRN_CHUNK, H, a.shape[-1]).transpose(1, 0, 3, 2, 4)

    qc, kc, fc, vc = to_chunks(q), to_chunks(k), to_chunks(logf), to_chunks(v)
    mask = jnp.tril(jnp.ones((HGRN_CHUNK, HGRN_CHUNK), dtype=bool))[None, None, :, :, None]

    def step(S, inp):
        qb, kb, fb, vb = inp
        b = jnp.cumsum(fb, axis=2)
        diff = b[:, :, :, None, :] - b[:, :, None, :, :]
        decay = jnp.exp(jnp.where(mask, diff, -jnp.inf))
        scores = jnp.einsum('bhik,bhijk->bhij', qb, decay * kb[:, :, None, :, :])
        o = jnp.einsum('bhij,bhjv->bhiv', scores, vb) + jnp.einsum('bhik,bhkv->bhiv', qb * jnp.exp(b), S)
        b_last = b[:, :, -1:, :]
        S_new = jnp.exp(b_last[:, :, 0, :])[..., None] * S + jnp.einsum('bhjk,bhjv->bhkv', kb * jnp.exp(b_last - b), vb)
        return S_new, o

    s_fin, o = lax.scan(step, s0.astype(jnp.float32), (qc, kc, fc, vc))
    o = o.transpose(1, 0, 3, 2, 4).reshape(B_, T, H, DV)
    return o, s_fin


def hgrn_final_state(k, logf, v):
    b = jnp.cumsum(logf.astype(jnp.float32), axis=1)
    w = k.astype(jnp.float32) * jnp.exp(b[:, -1:] - b)
    return jnp.einsum('bthk,bthv->bhkv', w, v.astype(jnp.float32))


def sgu(u, v, norm_w, w_s, b_s, n_chunks):
    B_, T, _ = u.shape
    u = jax.nn.gelu(u).reshape(B_, n_chunks, SGU_CHUNK, SGU_GROUPS, SGU_GROUP_DIM)
    v = jax.nn.gelu(v).reshape(B_, T, SGU_GROUPS, SGU_GROUP_DIM)
    v = rmsnorm(v, norm_w.reshape(SGU_GROUPS, SGU_GROUP_DIM))
    v = v.reshape(B_, n_chunks, SGU_CHUNK, SGU_GROUPS, SGU_GROUP_DIM)
    mixed = jnp.einsum('gts,bnsgc->bntgc', w_s, v) + b_s.T[:, :, None]
    return (u * mixed).reshape(B_, T, SGU_WIDTH)


def token_mixer(parts, lb_f, lb_b, hgrn_norm_w, sgu_norm_w, sgu_w, sgu_b,
                w_branch_a, w_branch_b, w_out, s0_f, s0_b, n_sgu_chunks):
    q_p, ff_p, fb_p, v_p, g_p, u_p, sv_p, ga_p, gb_p = parts
    q = jax.nn.silu(to_heads(q_p))
    v = to_heads(v_p)
    logf_f, k_f = hgrn_forget(to_heads(ff_p), lb_f)
    logf_b, k_b = hgrn_forget(to_heads(fb_p), lb_b)
    o_f, s_f = hgrn_chunk_scan(q, k_f, logf_f, v, s0_f)
    o_b, s_b = hgrn_chunk_scan(q[:, ::-1], k_b[:, ::-1], logf_b[:, ::-1], v[:, ::-1], s0_b)
    o = o_f + o_b[:, ::-1]
    o = rmsnorm(o, hgrn_norm_w.reshape(HGRN_HEADS, HGRN_DV)).astype(g_p.dtype)
    o = o.reshape(o.shape[0], o.shape[1], HGRN_WIDTH) * jax.nn.silu(g_p)
    y_a = o @ w_branch_a
    y_b = sgu(u_p, sv_p, sgu_norm_w, sgu_w, sgu_b, n_sgu_chunks) @ w_branch_b
    merged = jax.nn.sigmoid(ga_p) * y_a + jax.nn.sigmoid(gb_p) * y_b
    return merged @ w_out, s_f, s_b


def moe_expert_choice(h, router_w, w_gate, w_up, w_down):
    B_, T, _ = h.shape
    cap = CAPACITY_FACTOR * T // N_EXPERTS
    aff = jax.nn.softmax((h @ router_w).astype(jnp.float32), axis=-1)
    gate, idx = lax.top_k(jnp.swapaxes(aff, 1, 2), cap)
    bidx = jnp.arange(B_)[:, None, None]
    xs = h[bidx, idx]
    hid = jax.nn.silu(jnp.einsum('becd,edf->becf', xs, w_gate)) * jnp.einsum('becd,edf->becf', xs, w_up)
    ye = jnp.einsum('becf,efd->becd', hid, w_down) * gate[..., None].astype(h.dtype)
    return jnp.zeros_like(h).at[bidx, idx].add(ye)


def setup_inputs(seed: int = 0) -> dict:
    key = jax.random.key(seed)
    ks = jax.random.split(key, 24)
    f32 = jnp.float32
    nrm = lambda k, shape, s: jax.random.normal(k, shape, f32) * s
    return {
        'x': nrm(ks[0], (BATCH, SEQ, D_MODEL), 1.0),
        'c': nrm(ks[1], (BATCH, D_MODEL), 1.0),
        'ctx': nrm(ks[2], (BATCH, CTX_LEN, D_MODEL), 1.0),
        'c_ctx': nrm(ks[3], (D_MODEL,), 1.0),
        'ada_w': nrm(ks[4], (DEPTH, D_MODEL, N_MOD * D_MODEL), 0.02),
        'ada_b': nrm(ks[5], (DEPTH, N_MOD * D_MODEL), 0.02),
        'norm_mix_w': 1 + nrm(ks[6], (DEPTH, D_MODEL), 0.05),
        'norm_ffn_w': 1 + nrm(ks[7], (DEPTH, D_MODEL), 0.05),
        'w_in': nrm(ks[8], (DEPTH, D_MODEL, IN_COLS), D_MODEL ** -0.5),
        'hgrn_lb_logits': nrm(ks[9], (2, DEPTH + 1, HGRN_WIDTH), 0.5),
        'hgrn_norm_w': 1 + nrm(ks[10], (DEPTH, HGRN_WIDTH), 0.05),
        'sgu_norm_w': 1 + nrm(ks[11], (DEPTH, SGU_WIDTH), 0.05),
        'sgu_w': nrm(ks[12], (DEPTH, SGU_GROUPS, SGU_CHUNK, SGU_CHUNK), SGU_CHUNK ** -0.5),
        'sgu_b': 1 + nrm(ks[13], (DEPTH, SGU_GROUPS, SGU_CHUNK), 0.05),
        'w_branch_a': nrm(ks[14], (DEPTH, HGRN_WIDTH, D_MODEL), HGRN_WIDTH ** -0.5),
        'w_branch_b': nrm(ks[15], (DEPTH, SGU_WIDTH, D_MODEL), SGU_WIDTH ** -0.5),
        'w_out': nrm(ks[16], (DEPTH, D_MODEL, D_MODEL), D_MODEL ** -0.5),
        'router_w': nrm(ks[17], (DEPTH, D_MODEL, N_EXPERTS), D_MODEL ** -0.5),
        'expert_w_gate': nrm(ks[18], (DEPTH, N_EXPERTS, D_MODEL, EXPERT_FF), D_MODEL ** -0.5),
        'expert_w_up': nrm(ks[19], (DEPTH, N_EXPERTS, D_MODEL, EXPERT_FF), D_MODEL ** -0.5),
        'expert_w_down': nrm(ks[20], (DEPTH, N_EXPERTS, EXPERT_FF, D_MODEL), EXPERT_FF ** -0.5),
        'final_norm_w': 1 + nrm(ks[21], (D_MODEL,), 0.05),
    }


def reference(x, c, ctx, c_ctx, ada_w, ada_b, norm_mix_w, norm_ffn_w, w_in, hgrn_lb_logits,
              hgrn_norm_w, sgu_norm_w, sgu_w, sgu_b, w_branch_a, w_branch_b, w_out, router_w,
              expert_w_gate, expert_w_up, expert_w_down, final_norm_w):
    B_, N, _ = x.shape
    ROWS = N // GRID_W
    n_sgu_lat = ROWS // ROWS_PER_SGU_CHUNK
    n_sgu_ctx = ctx.shape[1] // SGU_CHUNK
    lb_all = jnp.cumsum(jax.nn.softmax(hgrn_lb_logits.astype(jnp.float32), axis=1), axis=1)
    for l in range(DEPTH):
        last = l == DEPTH - 1
        lb_f = lb_all[0, l].reshape(HGRN_HEADS, HGRN_DK)
        lb_b = lb_all[1, l].reshape(HGRN_HEADS, HGRN_DK)
        sh_m, sc_m, g_m, sh_f, sc_f, g_f = adaln(c, ada_w[l], ada_b[l])
        csh_m, csc_m, cg_m, csh_f, csc_f, cg_f = adaln(c_ctx, ada_w[l], ada_b[l])

        h_ctx = modulate(rmsnorm(ctx, norm_mix_w[l]), csh_m, csc_m)
        pc = split_in(h_ctx @ w_in[l])
        if last:
            v_c = to_heads(pc[3])
            logf_f, k_f = hgrn_forget(to_heads(pc[1]), lb_f)
            logf_b, k_b = hgrn_forget(to_heads(pc[2]), lb_b)
            s_f = hgrn_final_state(k_f, logf_f, v_c)
            s_b = hgrn_final_state(k_b[:, ::-1], logf_b[:, ::-1], v_c[:, ::-1])
        else:
            zeros = jnp.zeros((ctx.shape[0], HGRN_HEADS, HGRN_DK, HGRN_DV), jnp.float32)
            y_c, s_f, s_b = token_mixer(pc, lb_f, lb_b, hgrn_norm_w[l], sgu_norm_w[l], sgu_w[l], sgu_b[l],
                                        w_branch_a[l], w_branch_b[l], w_out[l], zeros, zeros, n_sgu_ctx)
            ctx = ctx + cg_m * y_c
            h2c = modulate(rmsnorm(ctx, norm_ffn_w[l]), csh_f, csc_f)
            ctx = ctx + cg_f * moe_expert_choice(h2c, router_w[l], expert_w_gate[l], expert_w_up[l], expert_w_down[l])

        h = modulate(rmsnorm(x, norm_mix_w[l]), sh_m, sc_m)
        pl = split_in(h @ w_in[l])
        y, _, _ = token_mixer(pl, lb_f, lb_b, hgrn_norm_w[l], sgu_norm_w[l], sgu_w[l], sgu_b[l],
                              w_branch_a[l], w_branch_b[l], w_out[l], s_f, s_b, n_sgu_lat)
        x = x + g_m * y
        h2 = modulate(rmsnorm(x, norm_ffn_w[l]), sh_f, sc_f)
        x = x + g_f * moe_expert_choice(h2, router_w[l], expert_w_gate[l], expert_w_up[l], expert_w_down[l])
    return rmsnorm(x, final_norm_w)
```

```python
import functools

import numpy as np
import jax
import jax.numpy as jnp
from jax import lax
from jax.experimental import pallas as pl
from jax.experimental.pallas import tpu as pltpu

F32 = jnp.float32
BF16 = jnp.bfloat16
I32 = jnp.int32

LANES = 128
EPS = 1e-6
N_MOD = 6
HEADS = 4
HD = 128
WIDTH = HEADS * HD
CHUNK = 128
N_EXPERTS = 16
CAPACITY_FACTOR = 2
LEVELS = tuple(CHUNK >> (i + 1) for i in range(7))
N_DMAT = 2 + len(LEVELS)
VMEM_LIMIT = 52 * 1024 * 1024


def _dot(a, b):
    return jnp.dot(a, b, preferred_element_type=F32)


def _dot_nt(a, b):
    return lax.dot_general(a, b, (((1,), (1,)), ((), ())), preferred_element_type=F32)


def _dot_tn(a, b):
    return lax.dot_general(a, b, (((0,), (0,)), ((), ())), preferred_element_type=F32)


def _split2(x):
    hi = x.astype(BF16)
    lo = (x - hi.astype(F32)).astype(BF16)
    return hi, lo


def _split3(x):
    hi = x.astype(BF16)
    r = x - hi.astype(F32)
    mid = r.astype(BF16)
    lo = (r - mid.astype(F32)).astype(BF16)
    return hi, mid, lo


def _sigmoid(x):
    return 1.0 / (1.0 + jnp.exp(-x))


def _silu(x):
    return x * _sigmoid(x)


def _gelu_tanh(x):
    c = np.sqrt(2.0 / np.pi).astype(np.float32)
    return 0.5 * x * (1.0 + jnp.tanh(c * (x + 0.044715 * (x * x * x))))


def _rms(x, w):
    return x * lax.rsqrt(jnp.mean(x * x, axis=-1, keepdims=True) + EPS) * w


def _lower_bound(lbl, d):
    l0 = lbl[2 * d:2 * d + 1, :]
    l1 = lbl[2 * d + 1:2 * d + 2, :]
    m = jnp.maximum(l0, l1)
    e0 = jnp.exp(l0 - m)
    e1 = jnp.exp(l1 - m)
    return e0 / (e0 + e1)


def _forget(z, lb):
    f = lb + (1.0 - lb) * _sigmoid(z)
    return jnp.log(f), 1.0 - f


def _cumsum_dot(m_bf16, x):
    hi, mid, lo = _split3(x)
    return _dot(m_bf16, hi) + _dot(m_bf16, mid) + _dot(m_bf16, lo)


def _decay_matrices(reverse):
    c = CHUNK
    i = np.arange(c)[:, None]
    m = np.arange(c)[None, :]
    mats = []
    if not reverse:
        mats.append(m <= i)
        mats.append(m > i)
    else:
        mats.append(m >= i)
        mats.append(m < i)
    for half in LEVELS:
        a = (i // (2 * half)) * (2 * half)
        mid = a + half
        if not reverse:
            qside = i >= mid
            mat = np.where(qside, (m >= mid) & (m <= i), (m > i) & (m < mid))
        else:
            qside = i < mid
            mat = np.where(qside, (m >= i) & (m < mid), (m >= mid) & (m < i))
        mats.append(mat)
    return np.concatenate(mats, axis=0).astype(np.float32)


def _block_decay_matrix(n, reverse):
    i = np.arange(n)[:, None]
    m = np.arange(n)[None, :]
    return ((m < i) if reverse else (m > i)).astype(np.float32)


def _adaln_kernel(cond_ref, w_ref, b_ref, out_ref):
    s = _silu(cond_ref[...])
    s_hi, s_lo = _split2(s)
    w_hi, w_lo = _split2(w_ref[...])
    out_ref[...] = _dot(s_hi, w_hi) + _dot(s_hi, w_lo) + _dot(s_lo, w_hi) + b_ref[...]


def _adaln(cond, ada_w, ada_b):
    rows, d = cond.shape
    n = ada_w.shape[1]
    tn = 1024
    return pl.pallas_call(
        _adaln_kernel,
        out_shape=jax.ShapeDtypeStruct((rows, n), F32),
        grid=(n // tn,),
        in_specs=[pl.BlockSpec((rows, d), lambda j: (0, 0)),
                  pl.BlockSpec((d, tn), lambda j: (0, j)),
                  pl.BlockSpec((1, tn), lambda j: (0, j))],
        out_specs=pl.BlockSpec((rows, tn), lambda j: (0, j)),
        compiler_params=pltpu.CompilerParams(dimension_semantics=("arbitrary",),
                                             vmem_limit_bytes=VMEM_LIMIT),
        name="adaln",
    )(cond, ada_w, ada_b.reshape(1, n))


def _mod_rows(mod_ref, row, d):
    return [mod_ref[pl.ds(row, 1), j * d:(j + 1) * d] for j in range(N_MOD)]


def _ctx_kernel(ctx_row, x_ref, mod_ref, nw_ref, w_ref, lbl_ref, mf_ref, mb_ref, sf_ref, sb_ref):
    d = x_ref.shape[-1]
    x = x_ref[0]
    sh, sc = mod_ref[pl.ds(ctx_row, 1), 0:d], mod_ref[pl.ds(ctx_row, 1), d:2 * d]
    h = _rms(x, nw_ref[...]) * (1.0 + sc) + sh
    p = _dot(h.astype(BF16), w_ref[...])
    v = p[:, 2 * WIDTH:3 * WIDTH].astype(BF16)
    lbl = lbl_ref[...]
    for dirn, (m_ref, out_ref) in enumerate(((mf_ref, sf_ref), (mb_ref, sb_ref))):
        logf, k = _forget(p[:, dirn * WIDTH:(dirn + 1) * WIDTH], _lower_bound(lbl, dirn))
        w = (k * jnp.exp(_cumsum_dot(m_ref[...], logf))).astype(BF16)
        for hh in range(HEADS):
            sl = slice(hh * HD, (hh + 1) * HD)
            out_ref[0, hh] = _dot_tn(v[:, sl], w[:, sl])


def _ctx_states(ctx, mod, norm_w, w_c, lbl, ctx_row):
    b, l, d = ctx.shape
    mf = jnp.asarray(_block_decay_matrix(l, False), BF16)
    mb = jnp.asarray(_block_decay_matrix(l, True), BF16)
    full = lambda a: pl.BlockSpec(a.shape, lambda i: (0,) * a.ndim)
    st = jax.ShapeDtypeStruct((b, HEADS, HD, HD), F32)
    st_spec = pl.BlockSpec((1, HEADS, HD, HD), lambda i: (i, 0, 0, 0))
    return pl.pallas_call(
        functools.partial(_ctx_kernel, ctx_row),
        out_shape=(st, st),
        grid=(b,),
        in_specs=[pl.BlockSpec((1, l, d), lambda i: (i, 0, 0)), full(mod), full(norm_w),
                  full(w_c), full(lbl), full(mf), full(mb)],
        out_specs=(st_spec, st_spec),
        compiler_params=pltpu.CompilerParams(dimension_semantics=("arbitrary",),
                                             vmem_limit_bytes=VMEM_LIMIT),
        name="ctx_state",
    )(ctx, mod, norm_w, w_c, lbl, mf, mb)


def _bwd_state_kernel(x_ref, mod_ref, nw_ref, w_ref, lbl_ref, m_ref, s0_ref, out_ref, st_ref):
    b = pl.program_id(0)
    i = pl.program_id(1)
    d = x_ref.shape[-1]

    @pl.when(i == 0)
    def _():
        st_ref[...] = s0_ref[0]

    out_ref[0, 0] = st_ref[...]
    x = x_ref[0]
    sh, sc = mod_ref[pl.ds(b, 1), 0:d], mod_ref[pl.ds(b, 1), d:2 * d]
    h = _rms(x, nw_ref[...]) * (1.0 + sc) + sh
    p = _dot(h.astype(BF16), w_ref[...])
    logf, k = _forget(p[:, 0:WIDTH], _lower_bound(lbl_ref[...], 1))
    v = p[:, WIDTH:2 * WIDTH].astype(BF16)
    w = (k * jnp.exp(_cumsum_dot(m_ref[...], logf))).astype(BF16)
    tot = jnp.exp(jnp.sum(logf, axis=0, keepdims=True))
    for hh in range(HEADS):
        sl = slice(hh * HD, (hh + 1) * HD)
        st_ref[hh] = st_ref[hh] * tot[:, sl] + _dot_tn(v[:, sl], w[:, sl])


def _bwd_states(x, mod, norm_w, w_bv, lbl, s_b0, tb):
    b, t, d = x.shape
    nb = t // tb
    m = jnp.asarray(_block_decay_matrix(tb, True), BF16)
    full = lambda a: pl.BlockSpec(a.shape, lambda bi, i: (0,) * a.ndim)
    return pl.pallas_call(
        _bwd_state_kernel,
        out_shape=jax.ShapeDtypeStruct((b, nb, HEADS, HD, HD), F32),
        grid=(b, nb),
        in_specs=[pl.BlockSpec((1, tb, d), lambda bi, i: (bi, nb - 1 - i, 0)),
                  full(mod), full(norm_w), full(w_bv), full(lbl), full(m),
                  pl.BlockSpec((1, HEADS, HD, HD), lambda bi, i: (bi, 0, 0, 0))],
        out_specs=pl.BlockSpec((1, 1, HEADS, HD, HD), lambda bi, i: (bi, nb - 1 - i, 0, 0, 0)),
        scratch_shapes=[pltpu.VMEM((HEADS, HD, HD), F32)],
        compiler_params=pltpu.CompilerParams(dimension_semantics=("arbitrary", "arbitrary"),
                                             vmem_limit_bytes=VMEM_LIMIT),
        name="bwd_state",
    )(x, mod, norm_w, w_bv, lbl, m, s_b0)


def _level_masks(reverse):
    row = lax.broadcasted_iota(I32, (CHUNK, CHUNK), 0)
    col = lax.broadcasted_iota(I32, (CHUNK, CHUNK), 1)
    x = row ^ col
    out = []
    for half in LEVELS:
        in_pair = jnp.where(x >= half, jnp.where(x < 2 * half, 1.0, 0.0), 0.0)
        bit = (col if reverse else row) & half
        out.append((jnp.where(bit != 0, in_pair, 0.0), (row & half) != 0))
    return out


def _hgrn_chunk(q, k, logf, v, dm_ref, masks, reverse, state_ref, o_ref, r0):
    hi, mid, lo = _split3(logf)
    dm = dm_ref[...]
    dall = _dot(dm, hi) + _dot(dm, mid) + _dot(dm, lo)
    e_in = jnp.exp(dall[0:CHUNK])
    e_st = jnp.exp(dall[CHUNK:2 * CHUNK])
    tot_row = 0 if reverse else CHUNK - 1
    e_tot = e_in[tot_row:tot_row + 1]
    vb = v.astype(BF16)
    for hh in range(HEADS):
        sl = slice(hh * HD, (hh + 1) * HD)
        qh, kh, vh = q[:, sl], k[:, sl], v[:, sl]
        pm = jnp.zeros((CHUNK, CHUNK), F32)
        for li in range(len(LEVELS)):
            mask, row_bit = masks[li]
            qside = jnp.logical_not(row_bit) if reverse else row_bit
            e = jnp.exp(dall[(2 + li) * CHUNK:(3 + li) * CHUNK, sl])
            xm = (jnp.where(qside, qh, kh) * e).astype(BF16)
            pm = pm + _dot_nt(xm, xm) * mask
        diag = jnp.sum(qh * kh, axis=-1, keepdims=True)
        st = state_ref[hh]
        o = (_dot(pm.astype(BF16), vb[:, sl]) + diag * vh
             + _dot_nt((qh * e_in[:, sl]).astype(BF16), st.astype(BF16)))
        o_ref[pl.ds(r0, CHUNK), sl] += o
        state_ref[hh] = st * e_tot[:, sl] + _dot_tn(vb[:, sl], (kh * e_st[:, sl]).astype(BF16))


def _mixer_kernel(x_ref, mod_ref, nmix_ref, nffn_ref, win_ref, lbl_ref, hnw_ref, snw_ref,
                  sw_ref, sbias_ref, wa_ref, wb_ref, wo_ref, rw_ref, dmf_ref, dmb_ref,
                  sf0_ref, bst_ref,
                  x1_ref, h2_ref, aff_ref,
                  p_ref, o_ref, sg_ref, stf_ref, stb_ref):
    b = pl.program_id(0)
    i = pl.program_id(1)
    tb, d = x_ref.shape[1], x_ref.shape[2]
    nch = tb // CHUNK
    sh_m, sc_m, g_m, sh_f, sc_f, _ = _mod_rows(mod_ref, b, d)

    @pl.when(i == 0)
    def _():
        stf_ref[...] = sf0_ref[0]

    stb_ref[...] = bst_ref[0, 0]

    x = x_ref[0]
    h = _rms(x, nmix_ref[...]) * (1.0 + sc_m) + sh_m
    p_ref[...] = _dot(h.astype(BF16), win_ref[...])
    o_ref[...] = jnp.zeros_like(o_ref)

    lbl = lbl_ref[...]
    c_q, c_ff, c_fb, c_v, c_g = (j * WIDTH for j in range(5))
    c_u, c_sv = 5 * WIDTH, 6 * WIDTH
    c_ga, c_gb = 7 * WIDTH, 7 * WIDTH + d

    for reverse, c_f, dm_ref, st_ref in ((False, c_ff, dmf_ref, stf_ref), (True, c_fb, dmb_ref, stb_ref)):
        masks = _level_masks(reverse)
        lb = _lower_bound(lbl, 1 if reverse else 0)
        order = range(nch - 1, -1, -1) if reverse else range(nch)
        for ci in order:
            r0 = ci * CHUNK
            rows = pl.ds(r0, CHUNK)
            q = _silu(p_ref[rows, c_q:c_q + WIDTH])
            logf, k = _forget(p_ref[rows, c_f:c_f + WIDTH], lb)
            v = p_ref[rows, c_v:c_v + WIDTH]
            _hgrn_chunk(q, k, logf, v, dm_ref, masks, reverse, st_ref, o_ref, r0)

    o = o_ref[...]
    on = jnp.concatenate(
        [o[:, hh * HD:(hh + 1) * HD]
         * lax.rsqrt(jnp.mean(o[:, hh * HD:(hh + 1) * HD] ** 2, axis=-1, keepdims=True) + EPS)
         for hh in range(HEADS)], axis=-1)
    a_in = on * hnw_ref[...] * _silu(p_ref[:, c_g:c_g + WIDTH])
    y_a = _dot(a_in.astype(BF16), wa_ref[...])

    u = _gelu_tanh(p_ref[:, c_u:c_u + WIDTH])
    sv = _gelu_tanh(p_ref[:, c_sv:c_sv + WIDTH])
    svn = jnp.concatenate(
        [sv[:, g * HD:(g + 1) * HD]
         * lax.rsqrt(jnp.mean(sv[:, g * HD:(g + 1) * HD] ** 2, axis=-1, keepdims=True) + EPS)
         for g in range(HEADS)], axis=-1)
    svn = (svn * snw_ref[...]).astype(BF16)
    for ci in range(nch):
        for g in range(HEADS):
            sl = slice(g * HD, (g + 1) * HD)
            sg_ref[ci * CHUNK:(ci + 1) * CHUNK, sl] = (
                _dot(sw_ref[g], svn[ci * CHUNK:(ci + 1) * CHUNK, sl]) + sbias_ref[:, sl])
    y_b = _dot((u * sg_ref[...]).astype(BF16), wb_ref[...])

    merged = (_sigmoid(p_ref[:, c_ga:c_ga + d]) * y_a + _sigmoid(p_ref[:, c_gb:c_gb + d]) * y_b)
    y = _dot(merged.astype(BF16), wo_ref[...])
    x1 = x + g_m * y
    x1_ref[0] = x1

    h2 = _rms(x1, nffn_ref[...]) * (1.0 + sc_f) + sh_f
    h2_hi, h2_lo = _split2(h2)
    h2_ref[0] = h2_hi
    rw_hi, rw_lo = _split2(rw_ref[...])
    logits = _dot_nt(rw_hi, h2_hi) + _dot_nt(rw_hi, h2_lo) + _dot_nt(rw_lo, h2_hi)
    mx = jnp.max(logits, axis=0, keepdims=True)
    ex = jnp.exp(logits - mx)
    aff_ref[0] = ex / jnp.sum(ex, axis=0, keepdims=True)


def _mixer(x, mod, nmix, nffn, w_in, lbl, hnw, snw, sgu_w, sgu_bias, w_a, w_b, w_o, rw_t,
           s_f0, bstates, tb):
    b, t, d = x.shape
    nb = t // tb
    ne = rw_t.shape[0]
    dmf = jnp.asarray(_decay_matrices(False), BF16)
    dmb = jnp.asarray(_decay_matrices(True), BF16)
    const = lambda a: pl.BlockSpec(a.shape, lambda bi, i: (0,) * a.ndim, pipeline_mode=pl.Buffered(1))
    in_specs = [pl.BlockSpec((1, tb, d), lambda bi, i: (bi, i, 0)),
                const(mod), const(nmix), const(nffn), const(w_in), const(lbl), const(hnw), const(snw),
                const(sgu_w), const(sgu_bias), const(w_a), const(w_b), const(w_o), const(rw_t),
                const(dmf), const(dmb),
                pl.BlockSpec((1, HEADS, HD, HD), lambda bi, i: (bi, 0, 0, 0)),
                pl.BlockSpec((1, 1, HEADS, HD, HD), lambda bi, i: (bi, i, 0, 0, 0))]
    out_shape = (jax.ShapeDtypeStruct((b, t, d), F32),
                 jax.ShapeDtypeStruct((b, t, d), BF16),
                 jax.ShapeDtypeStruct((b, ne, t), F32))
    out_specs = (pl.BlockSpec((1, tb, d), lambda bi, i: (bi, i, 0)),
                 pl.BlockSpec((1, tb, d), lambda bi, i: (bi, i, 0)),
                 pl.BlockSpec((1, ne, tb), lambda bi, i: (bi, 0, i)))
    return pl.pallas_call(
        _mixer_kernel,
        out_shape=out_shape,
        grid=(b, nb),
        in_specs=in_specs,
        out_specs=out_specs,
        scratch_shapes=[pltpu.VMEM((tb, w_in.shape[1]), F32),
                        pltpu.VMEM((tb, WIDTH), F32),
                        pltpu.VMEM((tb, WIDTH), F32),
                        pltpu.VMEM((HEADS, HD, HD), F32),
                        pltpu.VMEM((HEADS, HD, HD), F32)],
        compiler_params=pltpu.CompilerParams(dimension_semantics=("arbitrary", "arbitrary"),
                                             vmem_limit_bytes=VMEM_LIMIT),
        name="mixer",
    )(x, mod, nmix, nffn, w_in, lbl, hnw, snw, sgu_w, sgu_bias, w_a, w_b, w_o, rw_t,
      dmf, dmb, s_f0, bstates)


def _front(x, c, ctx, c_ctx, ada_w, ada_b, norm_mix_w, norm_ffn_w, w_in, hgrn_lb_logits,
           hgrn_norm_w, sgu_norm_w, sgu_w, sgu_b, w_branch_a, w_branch_b, w_out, router_w, tb):
    b, t, d = x.shape
    layer = 0
    cond = jnp.zeros((8, d), F32).at[0:b].set(c).at[b].set(c_ctx)
    mod = _adaln(cond, ada_w[layer], ada_b[layer])
    lbl = hgrn_lb_logits[:, layer:layer + 2, :].reshape(4, WIDTH)
    nmix = norm_mix_w[layer].reshape(1, d)
    nffn = norm_ffn_w[layer].reshape(1, d)
    w_in_b = w_in[layer].astype(BF16)
    s_f0, s_b0 = _ctx_states(ctx, mod, nmix, w_in_b[:, WIDTH:4 * WIDTH], lbl, b)
    w_bv = jnp.concatenate([w_in_b[:, 2 * WIDTH:3 * WIDTH], w_in_b[:, 3 * WIDTH:4 * WIDTH]], axis=1)
    bstates = _bwd_states(x, mod, nmix, w_bv, lbl, s_b0, tb)
    sgu_bias = jnp.repeat(sgu_b[layer].T, HD, axis=1)
    return _mixer(x, mod, nmix, nffn, w_in_b, lbl, hgrn_norm_w[layer].reshape(1, WIDTH),
                  sgu_norm_w[layer].reshape(1, WIDTH), sgu_w[layer].astype(BF16), sgu_bias,
                  w_branch_a[layer].astype(BF16), w_branch_b[layer].astype(BF16),
                  w_out[layer].astype(BF16), router_w[layer].T, s_f0, bstates, tb), mod


def _route_kernel(cap, aff_ref, pos_ref, gate_ref, starts_ref):
    a = aff_ref[0]
    ne, t = a.shape
    nblk = t // LANES
    bits = pltpu.bitcast(a, I32)

    def search(it, lo):
        cand = lo | (jnp.int32(1) << (30 - it))
        cnt = jnp.sum(jnp.where(bits >= cand, 1.0, 0.0), axis=-1, keepdims=True)
        return jnp.where(cnt >= cap, cand, lo)

    thr = lax.fori_loop(0, 31, search, jnp.zeros((ne, 1), I32))
    gt = bits > thr
    eq = bits == thr
    n_ties_wanted = cap - jnp.sum(jnp.where(gt, 1.0, 0.0), axis=-1, keepdims=True)

    row = lax.broadcasted_iota(I32, (LANES, LANES), 0)
    col = lax.broadcasted_iota(I32, (LANES, LANES), 1)
    upper = jnp.where(row <= col, 1.0, 0.0).astype(BF16)
    lane = lax.broadcasted_iota(I32, (ne, LANES), 1)

    off = jnp.zeros((ne, 1), F32)
    sel_blocks = []
    for j in range(nblk):
        sl = slice(j * LANES, (j + 1) * LANES)
        eqf = jnp.where(eq[:, sl], 1.0, 0.0)
        incl = _dot(eqf.astype(BF16), upper) + off
        keep_tie = jnp.where(incl - eqf < n_ties_wanted, eqf, 0.0)
        sel_blocks.append(jnp.where(gt[:, sl], 1.0, keep_tie))
        off = incl[:, LANES - 1:LANES]

    off = jnp.zeros((ne, 1), F32)
    starts = jnp.zeros((ne, LANES), F32)
    for j in range(nblk):
        sl = slice(j * LANES, (j + 1) * LANES)
        self = sel_blocks[j]
        starts = jnp.where(lane == j, off, starts)
        incl = _dot(self.astype(BF16), upper) + off
        pos_ref[0, :, sl] = jnp.where(self > 0.0, incl - 1.0, -1.0).astype(I32)
        gate_ref[0, :, sl] = jnp.where(self > 0.0, a[:, sl], 0.0)
        off = incl[:, LANES - 1:LANES]
    starts = jnp.where(lane >= nblk, off, starts)
    starts_ref[0] = starts.astype(I32)


def _route(aff_t, cap):
    b, ne, t = aff_t.shape
    spec = pl.BlockSpec((1, ne, t), lambda i: (i, 0, 0))
    return pl.pallas_call(
        functools.partial(_route_kernel, cap),
        out_shape=(jax.ShapeDtypeStruct((b, ne, t), I32),
                   jax.ShapeDtypeStruct((b, ne, t), F32),
                   jax.ShapeDtypeStruct((b, ne, LANES), I32)),
        grid=(b,),
        in_specs=[spec],
        out_specs=(spec, spec, pl.BlockSpec((1, ne, LANES), lambda i: (i, 0, 0))),
        compiler_params=pltpu.CompilerParams(dimension_semantics=("arbitrary",),
                                             vmem_limit_bytes=VMEM_LIMIT),
        name="route",
    )(aff_t)


TOK_BLOCK = 2 * LANES


def _block_range(starts_ref, base, n_tok_blocks, s0, s1):
    per = TOK_BLOCK // LANES

    def count(j, c):
        lo, hi = c
        lo = lo + (starts_ref[base + per * (j + 1)] <= s0).astype(I32)
        hi = hi + (starts_ref[base + per * j] < s1).astype(I32)
        return lo, hi

    return lax.fori_loop(0, n_tok_blocks, count, (jnp.int32(0), jnp.int32(0)))


def _gather_kernel(sb, starts_ref, pos_ref, gate_ref, h2_ref, xs_ref, meta_ref, acc_ref, macc_ref):
    b = pl.program_id(0)
    e = pl.program_id(1)
    ne = pl.num_programs(1)
    cap = xs_ref.shape[2]
    ntb = pos_ref.shape[2]
    base = (b * ne + e) * LANES
    slot = lax.broadcasted_iota(I32, (sb, TOK_BLOCK), 0)
    lane_f = lax.broadcasted_iota(I32, (8, TOK_BLOCK), 1).astype(F32)
    sub = lax.broadcasted_iota(I32, (8, TOK_BLOCK), 0)
    for s0 in range(0, cap, sb):
        lo, hi = _block_range(starts_ref, base, ntb, s0, s0 + sb)
        acc_ref[...] = jnp.zeros_like(acc_ref)
        macc_ref[...] = jnp.zeros_like(macc_ref)

        def body(j, carry):
            prow = pos_ref[0, 0, pl.ds(j, 1), :]
            onehot = jnp.where(prow == slot + s0, 1.0, 0.0).astype(BF16)
            rows = pl.ds(pl.multiple_of(j * TOK_BLOCK, TOK_BLOCK), TOK_BLOCK)
            acc_ref[...] += _dot(onehot, h2_ref[0, rows, :])
            g_hi, g_mid, g_lo = _split3(gate_ref[0, 0, pl.ds(j, 1), :])
            tokbase = (j * TOK_BLOCK).astype(F32)
            info = jnp.where(sub == 0, lane_f,
                   jnp.where(sub == 1, 1.0,
                   jnp.where(sub == 2, g_hi.astype(F32),
                   jnp.where(sub == 3, g_mid.astype(F32),
                   jnp.where(sub == 4, g_lo.astype(F32), 0.0)))))
            m = _dot_nt(info.astype(BF16), onehot)
            idx = m[0:1] + tokbase * m[1:2]
            gsl = m[2:3] + m[3:4] + m[4:5]
            macc_ref[...] += jnp.where(sub[:, 0:sb] == 0, idx, jnp.where(sub[:, 0:sb] == 1, gsl, 0.0))
            return carry

        lax.fori_loop(lo, hi, body, 0)
        xs_ref[0, 0, s0:s0 + sb, :] = acc_ref[...].astype(BF16)
        meta_ref[0, 0, :, s0:s0 + sb] = macc_ref[...]


def _gather(starts, pos, gate, h2, cap, sb):
    b, t, d = h2.shape
    ne = pos.shape[1]
    ntb = t // TOK_BLOCK
    pos4 = pos.reshape(b, ne, ntb, TOK_BLOCK)
    gate4 = gate.reshape(b, ne, ntb, TOK_BLOCK)
    grid_spec = pltpu.PrefetchScalarGridSpec(
        num_scalar_prefetch=1,
        grid=(b, ne),
        in_specs=[pl.BlockSpec((1, 1, ntb, TOK_BLOCK), lambda bi, e, s: (bi, e, 0, 0)),
                  pl.BlockSpec((1, 1, ntb, TOK_BLOCK), lambda bi, e, s: (bi, e, 0, 0)),
                  pl.BlockSpec((1, t, d), lambda bi, e, s: (bi, 0, 0), pipeline_mode=pl.Buffered(1))],
        out_specs=(pl.BlockSpec((1, 1, cap, d), lambda bi, e, s: (bi, e, 0, 0)),
                   pl.BlockSpec((1, 1, 8, cap), lambda bi, e, s: (bi, e, 0, 0))),
        scratch_shapes=[pltpu.VMEM((sb, d), F32), pltpu.VMEM((8, sb), F32)])
    return pl.pallas_call(
        functools.partial(_gather_kernel, sb),
        out_shape=(jax.ShapeDtypeStruct((b, ne, cap, d), BF16),
                   jax.ShapeDtypeStruct((b, ne, 8, cap), F32)),
        grid_spec=grid_spec,
        compiler_params=pltpu.CompilerParams(dimension_semantics=("arbitrary", "arbitrary"),
                                             vmem_limit_bytes=VMEM_LIMIT),
        name="gather",
    )(starts.reshape(-1), pos4, gate4, h2)


def _ffn_kernel(xs_ref, wg_ref, wu_ref, wd_ref, ye_ref, acc_ref):
    f = pl.program_id(2)
    xs = xs_ref[0, 0]
    a = _dot(xs, wg_ref[0].astype(BF16))
    u = _dot(xs, wu_ref[0].astype(BF16))
    contrib = _dot((_silu(a) * u).astype(BF16), wd_ref[0].astype(BF16))

    @pl.when(f == 0)
    def _():
        acc_ref[...] = contrib

    @pl.when(f > 0)
    def _():
        acc_ref[...] += contrib

    @pl.when(f == pl.num_programs(2) - 1)
    def _():
        ye_ref[0, 0] = acc_ref[...].astype(BF16)


def _ffn(xs, w_gate, w_up, w_down, tf):
    b, ne, cap, d = xs.shape
    ff = w_gate.shape[2]
    return pl.pallas_call(
        _ffn_kernel,
        out_shape=jax.ShapeDtypeStruct((b, ne, cap, d), BF16),
        grid=(b, ne, ff // tf),
        in_specs=[pl.BlockSpec((1, 1, cap, d), lambda bi, e, f: (bi, e, 0, 0)),
                  pl.BlockSpec((1, d, tf), lambda bi, e, f: (e, 0, f)),
                  pl.BlockSpec((1, d, tf), lambda bi, e, f: (e, 0, f)),
                  pl.BlockSpec((1, tf, d), lambda bi, e, f: (e, f, 0))],
        out_specs=pl.BlockSpec((1, 1, cap, d), lambda bi, e, f: (bi, e, 0, 0)),
        scratch_shapes=[pltpu.VMEM((cap, d), F32)],
        compiler_params=pltpu.CompilerParams(dimension_semantics=("arbitrary", "arbitrary", "arbitrary"),
                                             vmem_limit_bytes=VMEM_LIMIT),
        name="ffn",
    )(xs, w_gate, w_up, w_down)


def _combine_kernel(sb, starts_ref, meta_ref, ye_ref, x1_ref, gf_ref, out_ref):
    b = pl.program_id(0)
    e = pl.program_id(2)
    ne = pl.num_programs(2)
    cap = ye_ref.shape[2]
    t = out_ref.shape[1]
    ntb = t // TOK_BLOCK
    base = (b * ne + e) * LANES

    @pl.when(e == 0)
    def _():
        out_ref[...] = jnp.zeros_like(out_ref)

    tok = lax.broadcasted_iota(I32, (TOK_BLOCK, sb), 0)
    for s0 in range(0, cap, sb):
        lo, hi = _block_range(starts_ref, base, ntb, s0, s0 + sb)
        idx = meta_ref[0, 0, 0:1, s0:s0 + sb].astype(I32)
        gsl = meta_ref[0, 0, 1:2, s0:s0 + sb]
        ye = ye_ref[0, 0, s0:s0 + sb, :]

        def body(j, carry):
            w_t = jnp.where(idx == tok + j * TOK_BLOCK, gsl, 0.0).astype(BF16)
            rows = pl.ds(pl.multiple_of(j * TOK_BLOCK, TOK_BLOCK), TOK_BLOCK)
            out_ref[0, rows, :] += _dot(w_t, ye)
            return carry

        lax.fori_loop(lo, hi, body, 0)

    @pl.when(e == ne - 1)
    def _():
        out_ref[0] = x1_ref[0] + gf_ref[pl.ds(b, 1), :] * out_ref[0]


def _combine(starts, meta, ye, x1, mod, sb, td):
    b, t, d = x1.shape
    ne, cap = ye.shape[1], ye.shape[2]
    gf_block0 = (N_MOD - 1) * (d // td)
    grid_spec = pltpu.PrefetchScalarGridSpec(
        num_scalar_prefetch=1,
        grid=(b, d // td, ne),
        in_specs=[pl.BlockSpec((1, 1, 8, cap), lambda bi, dj, e, s: (bi, e, 0, 0)),
                  pl.BlockSpec((1, 1, cap, td), lambda bi, dj, e, s: (bi, e, 0, dj)),
                  pl.BlockSpec((1, t, td), lambda bi, dj, e, s: (bi, 0, dj)),
                  pl.BlockSpec((mod.shape[0], td), lambda bi, dj, e, s: (0, gf_block0 + dj))],
        out_specs=pl.BlockSpec((1, t, td), lambda bi, dj, e, s: (bi, 0, dj)))
    return pl.pallas_call(
        functools.partial(_combine_kernel, sb),
        out_shape=jax.ShapeDtypeStruct((b, t, d), F32),
        grid_spec=grid_spec,
        compiler_params=pltpu.CompilerParams(
            dimension_semantics=("arbitrary", "arbitrary", "arbitrary"), vmem_limit_bytes=VMEM_LIMIT),
        name="combine",
    )(starts.reshape(-1), meta, ye, x1, mod)


def _final_norm_kernel(x_ref, w_ref, out_ref):
    out_ref[0] = _rms(x_ref[0], w_ref[...])


def _final_norm(x, w, tb):
    b, t, d = x.shape
    spec = pl.BlockSpec((1, tb, d), lambda bi, i: (bi, i, 0))
    return pl.pallas_call(
        _final_norm_kernel,
        out_shape=jax.ShapeDtypeStruct((b, t, d), F32),
        grid=(b, t // tb),
        in_specs=[spec, pl.BlockSpec((1, d), lambda bi, i: (0, 0))],
        out_specs=spec,
        compiler_params=pltpu.CompilerParams(dimension_semantics=("arbitrary", "arbitrary"),
                                             vmem_limit_bytes=VMEM_LIMIT),
        name="final_norm",
    )(x, w.reshape(1, d))


def kernel(x, c, ctx, c_ctx, ada_w, ada_b, norm_mix_w, norm_ffn_w, w_in, hgrn_lb_logits, hgrn_norm_w,
           sgu_norm_w, sgu_w, sgu_b, w_branch_a, w_branch_b, w_out, router_w, expert_w_gate,
           expert_w_up, expert_w_down, final_norm_w):
    b, t, d = x.shape
    assert b + 1 <= 8 and t % TOK_BLOCK == 0 and d % LANES == 0
    ne = router_w.shape[-1]
    cap = CAPACITY_FACTOR * t // ne
    sb = min(cap, 256)
    assert cap % sb == 0
    (x1, h2, aff_t), mod = _front(x, c, ctx, c_ctx, ada_w, ada_b, norm_mix_w, norm_ffn_w, w_in,
                                  hgrn_lb_logits, hgrn_norm_w, sgu_norm_w, sgu_w, sgu_b,
                                  w_branch_a, w_branch_b, w_out, router_w, 256)
    pos, gate, starts = _route(aff_t, cap)
    xs, meta = _gather(starts, pos, gate, h2, cap, sb)
    ye = _ffn(xs, expert_w_gate[0], expert_w_up[0], expert_w_down[0], 512)
    x2 = _combine(starts, meta, ye, x1, mod, sb, 256)
    return _final_norm(x2, final_norm_w, 512)
```

```python
import functools

import numpy as np
import jax
import jax.numpy as jnp
from jax import lax
from jax.experimental import pallas as pl
from jax.experimental.pallas import tpu as pltpu

F32 = jnp.float32
BF16 = jnp.bfloat16
I32 = jnp.int32

LANES = 128
EPS = 1e-6
N_MOD = 6
HEADS = 4
HD = 128
WIDTH = HEADS * HD
CHUNK = 128
N_EXPERTS = 16
CAPACITY_FACTOR = 2
LEVELS = tuple(CHUNK >> (i + 1) for i in range(7))
N_DMAT = 2 + len(LEVELS)
VMEM_LIMIT = 52 * 1024 * 1024


def _dot(a, b):
    return jnp.dot(a, b, preferred_element_type=F32)


def _dot_nt(a, b):
    return lax.dot_general(a, b, (((1,), (1,)), ((), ())), preferred_element_type=F32)


def _dot_tn(a, b):
    return lax.dot_general(a, b, (((0,), (0,)), ((), ())), preferred_element_type=F32)


def _split2(x):
    hi = x.astype(BF16)
    lo = (x - hi.astype(F32)).astype(BF16)
    return hi, lo


def _split3(x):
    hi = x.astype(BF16)
    r = x - hi.astype(F32)
    mid = r.astype(BF16)
    lo = (r - mid.astype(F32)).astype(BF16)
    return hi, mid, lo


def _sigmoid(x):
    return 1.0 / (1.0 + jnp.exp(-x))


def _silu(x):
    return x * _sigmoid(x)


def _gelu_tanh(x):
    c = np.sqrt(2.0 / np.pi).astype(np.float32)
    return 0.5 * x * (1.0 + jnp.tanh(c * (x + 0.044715 * (x * x * x))))


def _rms(x, w):
    return x * lax.rsqrt(jnp.mean(x * x, axis=-1, keepdims=True) + EPS) * w


def _lower_bound(lbl, d):
    l0 = lbl[2 * d:2 * d + 1, :]
    l1 = lbl[2 * d + 1:2 * d + 2, :]
    m = jnp.maximum(l0, l1)
    e0 = jnp.exp(l0 - m)
    e1 = jnp.exp(l1 - m)
    return e0 / (e0 + e1)


def _forget(z, lb):
    f = lb + (1.0 - lb) * _sigmoid(z)
    return jnp.log(f), 1.0 - f


def _cumsum_dot(m_bf16, x):
    hi, mid, lo = _split3(x)
    return _dot(m_bf16, hi) + _dot(m_bf16, mid) + _dot(m_bf16, lo)


def _decay_matrices(reverse):
    c = CHUNK
    i = np.arange(c)[:, None]
    m = np.arange(c)[None, :]
    mats = []
    if not reverse:
        mats.append(m <= i)
        mats.append(m > i)
    else:
        mats.append(m >= i)
        mats.append(m < i)
    for half in LEVELS:
        a = (i // (2 * half)) * (2 * half)
        mid = a + half
        if not reverse:
            qside = i >= mid
            mat = np.where(qside, (m >= mid) & (m <= i), (m > i) & (m < mid))
        else:
            qside = i < mid
            mat = np.where(qside, (m >= i) & (m < mid), (m >= mid) & (m < i))
        mats.append(mat)
    return np.concatenate(mats, axis=0).astype(np.float32)


def _block_decay_matrix(n, reverse):
    i = np.arange(n)[:, None]
    m = np.arange(n)[None, :]
    return ((m < i) if reverse else (m > i)).astype(np.float32)


def _adaln_kernel(cond_ref, w_ref, b_ref, out_ref):
    s = _silu(cond_ref[...])
    s_hi, s_lo = _split2(s)
    w_hi, w_lo = _split2(w_ref[...])
    out_ref[...] = _dot(s_hi, w_hi) + _dot(s_hi, w_lo) + _dot(s_lo, w_hi) + b_ref[...]


def _adaln(cond, ada_w, ada_b):
    rows, d = cond.shape
    n = ada_w.shape[1]
    tn = 1024
    return pl.pallas_call(
        _adaln_kernel,
        out_shape=jax.ShapeDtypeStruct((rows, n), F32),
        grid=(n // tn,),
        in_specs=[pl.BlockSpec((rows, d), lambda j: (0, 0)),
                  pl.BlockSpec((d, tn), lambda j: (0, j)),
                  pl.BlockSpec((1, tn), lambda j: (0, j))],
        out_specs=pl.BlockSpec((rows, tn), lambda j: (0, j)),
        compiler_params=pltpu.CompilerParams(dimension_semantics=("arbitrary",),
                                             vmem_limit_bytes=VMEM_LIMIT),
        name="adaln",
    )(cond, ada_w, ada_b.reshape(1, n))


def _mod_rows(mod_ref, row, d):
    return [mod_ref[pl.ds(row, 1), j * d:(j + 1) * d] for j in range(N_MOD)]


def _ctx_kernel(ctx_row, x_ref, mod_ref, nw_ref, w_ref, lbl_ref, mf_ref, mb_ref, sf_ref, sb_ref):
    d = x_ref.shape[-1]
    x = x_ref[0]
    sh, sc = mod_ref[pl.ds(ctx_row, 1), 0:d], mod_ref[pl.ds(ctx_row, 1), d:2 * d]
    h = _rms(x, nw_ref[...]) * (1.0 + sc) + sh
    p = _dot(h.astype(BF16), w_ref[...])
    v = p[:, 2 * WIDTH:3 * WIDTH].astype(BF16)
    lbl = lbl_ref[...]
    for dirn, (m_ref, out_ref) in enumerate(((mf_ref, sf_ref), (mb_ref, sb_ref))):
        logf, k = _forget(p[:, dirn * WIDTH:(dirn + 1) * WIDTH], _lower_bound(lbl, dirn))
        w = (k * jnp.exp(_cumsum_dot(m_ref[...], logf))).astype(BF16)
        for hh in range(HEADS):
            sl = slice(hh * HD, (hh + 1) * HD)
            out_ref[0, hh] = _dot_tn(v[:, sl], w[:, sl])


def _ctx_states(ctx, mod, norm_w, w_c, lbl, ctx_row):
    b, l, d = ctx.shape
    mf = jnp.asarray(_block_decay_matrix(l, False), BF16)
    mb = jnp.asarray(_block_decay_matrix(l, True), BF16)
    full = lambda a: pl.BlockSpec(a.shape, lambda i: (0,) * a.ndim)
    st = jax.ShapeDtypeStruct((b, HEADS, HD, HD), F32)
    st_spec = pl.BlockSpec((1, HEADS, HD, HD), lambda i: (i, 0, 0, 0))
    return pl.pallas_call(
        functools.partial(_ctx_kernel, ctx_row),
        out_shape=(st, st),
        grid=(b,),
        in_specs=[pl.BlockSpec((1, l, d), lambda i: (i, 0, 0)), full(mod), full(norm_w),
                  full(w_c), full(lbl), full(mf), full(mb)],
        out_specs=(st_spec, st_spec),
        compiler_params=pltpu.CompilerParams(dimension_semantics=("arbitrary",),
                                             vmem_limit_bytes=VMEM_LIMIT),
        name="ctx_state",
    )(ctx, mod, norm_w, w_c, lbl, mf, mb)


def _bwd_state_kernel(x_ref, mod_ref, nw_ref, w_ref, lbl_ref, m_ref, s0_ref, out_ref, st_ref):
    b = pl.program_id(0)
    i = pl.program_id(1)
    d = x_ref.shape[-1]

    @pl.when(i == 0)
    def _():
        st_ref[...] = s0_ref[0]

    out_ref[0, 0] = st_ref[...]
    x = x_ref[0]
    sh, sc = mod_ref[pl.ds(b, 1), 0:d], mod_ref[pl.ds(b, 1), d:2 * d]
    h = _rms(x, nw_ref[...]) * (1.0 + sc) + sh
    p = _dot(h.astype(BF16), w_ref[...])
    logf, k = _forget(p[:, 0:WIDTH], _lower_bound(lbl_ref[...], 1))
    v = p[:, WIDTH:2 * WIDTH].astype(BF16)
    w = (k * jnp.exp(_cumsum_dot(m_ref[...], logf))).astype(BF16)
    tot = jnp.exp(jnp.sum(logf, axis=0, keepdims=True))
    for hh in range(HEADS):
        sl = slice(hh * HD, (hh + 1) * HD)
        st_ref[hh] = st_ref[hh] * tot[:, sl] + _dot_tn(v[:, sl], w[:, sl])


def _bwd_states(x, mod, norm_w, w_bv, lbl, s_b0, tb):
    b, t, d = x.shape
    nb = t // tb
    m = jnp.asarray(_block_decay_matrix(tb, True), BF16)
    full = lambda a: pl.BlockSpec(a.shape, lambda bi, i: (0,) * a.ndim)
    return pl.pallas_call(
        _bwd_state_kernel,
        out_shape=jax.ShapeDtypeStruct((b, nb, HEADS, HD, HD), F32),
        grid=(b, nb),
        in_specs=[pl.BlockSpec((1, tb, d), lambda bi, i: (bi, nb - 1 - i, 0)),
                  full(mod), full(norm_w), full(w_bv), full(lbl), full(m),
                  pl.BlockSpec((1, HEADS, HD, HD), lambda bi, i: (bi, 0, 0, 0))],
        out_specs=pl.BlockSpec((1, 1, HEADS, HD, HD), lambda bi, i: (bi, nb - 1 - i, 0, 0, 0)),
        scratch_shapes=[pltpu.VMEM((HEADS, HD, HD), F32)],
        compiler_params=pltpu.CompilerParams(dimension_semantics=("arbitrary", "arbitrary"),
                                             vmem_limit_bytes=VMEM_LIMIT),
        name="bwd_state",
    )(x, mod, norm_w, w_bv, lbl, m, s_b0)


def _level_masks(reverse):
    row = lax.broadcasted_iota(I32, (CHUNK, CHUNK), 0)
    col = lax.broadcasted_iota(I32, (CHUNK, CHUNK), 1)
    x = row ^ col
    out = []
    for half in LEVELS:
        in_pair = jnp.where(x >= half, jnp.where(x < 2 * half, 1.0, 0.0), 0.0)
        bit = (col if reverse else row) & half
        out.append((jnp.where(bit != 0, in_pair, 0.0), (row & half) != 0))
    return out


def _hgrn_chunk(q, k, logf, v, dm_ref, masks, reverse, state_ref, o_ref, r0):
    hi, mid, lo = _split3(logf)
    dm = dm_ref[...]
    dall = _dot(dm, hi) + _dot(dm, mid) + _dot(dm, lo)
    e_in = jnp.exp(dall[0:CHUNK])
    e_st = jnp.exp(dall[CHUNK:2 * CHUNK])
    tot_row = 0 if reverse else CHUNK - 1
    e_tot = e_in[tot_row:tot_row + 1]
    vb = v.astype(BF16)
    for hh in range(HEADS):
        sl = slice(hh * HD, (hh + 1) * HD)
        qh, kh, vh = q[:, sl], k[:, sl], v[:, sl]
        pm = jnp.zeros((CHUNK, CHUNK), F32)
        for li in range(len(LEVELS)):
            mask, row_bit = masks[li]
            qside = jnp.logical_not(row_bit) if reverse else row_bit
            e = jnp.exp(dall[(2 + li) * CHUNK:(3 + li) * CHUNK, sl])
            xm = (jnp.where(qside, qh, kh) * e).astype(BF16)
            pm = pm + _dot_nt(xm, xm) * mask
        diag = jnp.sum(qh * kh, axis=-1, keepdims=True)
        st = state_ref[hh]
        o = (_dot(pm.astype(BF16), vb[:, sl]) + diag * vh
             + _dot_nt((qh * e_in[:, sl]).astype(BF16), st.astype(BF16)))
        o_ref[pl.ds(r0, CHUNK), sl] += o
        state_ref[hh] = st * e_tot[:, sl] + _dot_tn(vb[:, sl], (kh * e_st[:, sl]).astype(BF16))


def _mixer_kernel(x_ref, mod_ref, nmix_ref, nffn_ref, win_ref, lbl_ref, hnw_ref, snw_ref,
                  sw_ref, sbias_ref, wa_ref, wb_ref, wo_ref, rw_ref, dmf_ref, dmb_ref,
                  sf0_ref, bst_ref,
                  x1_ref, h2_ref, aff_ref,
                  p_ref, o_ref, sg_ref, stf_ref, stb_ref):
    b = pl.program_id(0)
    i = pl.program_id(1)
    tb, d = x_ref.shape[1], x_ref.shape[2]
    nch = tb // CHUNK
    sh_m, sc_m, g_m, sh_f, sc_f, _ = _mod_rows(mod_ref, b, d)

    @pl.when(i == 0)
    def _():
        stf_ref[...] = sf0_ref[0]

    stb_ref[...] = bst_ref[0, 0]

    x = x_ref[0]
    h = _rms(x, nmix_ref[...]) * (1.0 + sc_m) + sh_m
    p_ref[...] = _dot(h.astype(BF16), win_ref[...])
    o_ref[...] = jnp.zeros_like(o_ref)

    lbl = lbl_ref[...]
    c_q, c_ff, c_fb, c_v, c_g = (j * WIDTH for j in range(5))
    c_u, c_sv = 5 * WIDTH, 6 * WIDTH
    c_ga, c_gb = 7 * WIDTH, 7 * WIDTH + d

    for reverse, c_f, dm_ref, st_ref in ((False, c_ff, dmf_ref, stf_ref), (True, c_fb, dmb_ref, stb_ref)):
        masks = _level_masks(reverse)
        lb = _lower_bound(lbl, 1 if reverse else 0)
        order = range(nch - 1, -1, -1) if reverse else range(nch)
        for ci in order:
            r0 = ci * CHUNK
            rows = pl.ds(r0, CHUNK)
            q = _silu(p_ref[rows, c_q:c_q + WIDTH])
            logf, k = _forget(p_ref[rows, c_f:c_f + WIDTH], lb)
            v = p_ref[rows, c_v:c_v + WIDTH]
            _hgrn_chunk(q, k, logf, v, dm_ref, masks, reverse, st_ref, o_ref, r0)

    o = o_ref[...]
    on = jnp.concatenate(
        [o[:, hh * HD:(hh + 1) * HD]
         * lax.rsqrt(jnp.mean(o[:, hh * HD:(hh + 1) * HD] ** 2, axis=-1, keepdims=True) + EPS)
         for hh in range(HEADS)], axis=-1)
    a_in = on * hnw_ref[...] * _silu(p_ref[:, c_g:c_g + WIDTH])
    y_a = _dot(a_in.astype(BF16), wa_ref[...])

    u = _gelu_tanh(p_ref[:, c_u:c_u + WIDTH])
    sv = _gelu_tanh(p_ref[:, c_sv:c_sv + WIDTH])
    svn = jnp.concatenate(
        [sv[:, g * HD:(g + 1) * HD]
         * lax.rsqrt(jnp.mean(sv[:, g * HD:(g + 1) * HD] ** 2, axis=-1, keepdims=True) + EPS)
         for g in range(HEADS)], axis=-1)
    svn = (svn * snw_ref[...]).astype(BF16)
    for ci in range(nch):
        for g in range(HEADS):
            sl = slice(g * HD, (g + 1) * HD)
            sg_ref[ci * CHUNK:(ci + 1) * CHUNK, sl] = (
                _dot(sw_ref[g], svn[ci * CHUNK:(ci + 1) * CHUNK, sl]) + sbias_ref[:, sl])
    y_b = _dot((u * sg_ref[...]).astype(BF16), wb_ref[...])

    merged = (_sigmoid(p_ref[:, c_ga:c_ga + d]) * y_a + _sigmoid(p_ref[:, c_gb:c_gb + d]) * y_b)
    y = _dot(merged.astype(BF16), wo_ref[...])
    x1 = x + g_m * y
    x1_ref[0] = x1

    h2 = _rms(x1, nffn_ref[...]) * (1.0 + sc_f) + sh_f
    h2_hi, h2_lo = _split2(h2)
    h2_ref[0] = h2_hi
    rw_hi, rw_lo = _split2(rw_ref[...])
    logits = _dot_nt(rw_hi, h2_hi) + _dot_nt(rw_hi, h2_lo) + _dot_nt(rw_lo, h2_hi)
    mx = jnp.max(logits, axis=0, keepdims=True)
    ex = jnp.exp(logits - mx)
    aff_ref[0] = ex / jnp.sum(ex, axis=0, keepdims=True)


def _mixer(x, mod, nmix, nffn, w_in, lbl, hnw, snw, sgu_w, sgu_bias, w_a, w_b, w_o, rw_t,
           s_f0, bstates, tb):
    b, t, d = x.shape
    nb = t // tb
    ne = rw_t.shape[0]
    dmf = jnp.asarray(_decay_matrices(False), BF16)
    dmb = jnp.asarray(_decay_matrices(True), BF16)
    const = lambda a: pl.BlockSpec(a.shape, lambda bi, i: (0,) * a.ndim, pipeline_mode=pl.Buffered(1))
    in_specs = [pl.BlockSpec((1, tb, d), lambda bi, i: (bi, i, 0)),
                const(mod), const(nmix), const(nffn), const(w_in), const(lbl), const(hnw), const(snw),
                const(sgu_w), const(sgu_bias), const(w_a), const(w_b), const(w_o), const(rw_t),
                const(dmf), const(dmb),
                pl.BlockSpec((1, HEADS, HD, HD), lambda bi, i: (bi, 0, 0, 0)),
                pl.BlockSpec((1, 1, HEADS, HD, HD), lambda bi, i: (bi, i, 0, 0, 0))]
    out_shape = (jax.ShapeDtypeStruct((b, t, d), F32),
                 jax.ShapeDtypeStruct((b, t, d), BF16),
                 jax.ShapeDtypeStruct((b, ne, t), F32))
    out_specs = (pl.BlockSpec((1, tb, d), lambda bi, i: (bi, i, 0)),
                 pl.BlockSpec((1, tb, d), lambda bi, i: (bi, i, 0)),
                 pl.BlockSpec((1, ne, tb), lambda bi, i: (bi, 0, i)))
    return pl.pallas_call(
        _mixer_kernel,
        out_shape=out_shape,
        grid=(b, nb),
        in_specs=in_specs,
        out_specs=out_specs,
        scratch_shapes=[pltpu.VMEM((tb, w_in.shape[1]), F32),
                        pltpu.VMEM((tb, WIDTH), F32),
                        pltpu.VMEM((tb, WIDTH), F32),
                        pltpu.VMEM((HEADS, HD, HD), F32),
                        pltpu.VMEM((HEADS, HD, HD), F32)],
        compiler_params=pltpu.CompilerParams(dimension_semantics=("arbitrary", "arbitrary"),
                                             vmem_limit_bytes=VMEM_LIMIT),
        name="mixer",
    )(x, mod, nmix, nffn, w_in, lbl, hnw, snw, sgu_w, sgu_bias, w_a, w_b, w_o, rw_t,
      dmf, dmb, s_f0, bstates)


def _front(x, c, ctx, c_ctx, ada_w, ada_b, norm_mix_w, norm_ffn_w, w_in, hgrn_lb_logits,
           hgrn_norm_w, sgu_norm_w, sgu_w, sgu_b, w_branch_a, w_branch_b, w_out, router_w, tb):
    b, t, d = x.shape
    layer = 0
    cond = jnp.zeros((8, d), F32).at[0:b].set(c).at[b].set(c_ctx)
    mod = _adaln(cond, ada_w[layer], ada_b[layer])
    lbl = hgrn_lb_logits[:, layer:layer + 2, :].reshape(4, WIDTH)
    nmix = norm_mix_w[layer].reshape(1, d)
    nffn = norm_ffn_w[layer].reshape(1, d)
    w_in_b = w_in[layer].astype(BF16)
    s_f0, s_b0 = _ctx_states(ctx, mod, nmix, w_in_b[:, WIDTH:4 * WIDTH], lbl, b)
    w_bv = jnp.concatenate([w_in_b[:, 2 * WIDTH:3 * WIDTH], w_in_b[:, 3 * WIDTH:4 * WIDTH]], axis=1)
    bstates = _bwd_states(x, mod, nmix, w_bv, lbl, s_b0, tb)
    sgu_bias = jnp.repeat(sgu_b[layer].T, HD, axis=1)
    return _mixer(x, mod, nmix, nffn, w_in_b, lbl, hgrn_norm_w[layer].reshape(1, WIDTH),
                  sgu_norm_w[layer].reshape(1, WIDTH), sgu_w[layer].astype(BF16), sgu_bias,
                  w_branch_a[layer].astype(BF16), w_branch_b[layer].astype(BF16),
                  w_out[layer].astype(BF16), router_w[layer].T, s_f0, bstates, tb), mod


def _route_kernel(cap, aff_ref, pos_ref, gate_ref, starts_ref):
    a = aff_ref[0]
    ne, t = a.shape
    nblk = t // LANES
    bits = pltpu.bitcast(a, I32)

    def search(it, lo):
        cand = lo | (jnp.int32(1) << (30 - it))
        cnt = jnp.sum(jnp.where(bits >= cand, 1.0, 0.0), axis=-1, keepdims=True)
        return jnp.where(cnt >= cap, cand, lo)

    thr = lax.fori_loop(0, 31, search, jnp.zeros((ne, 1), I32))
    gt = bits > thr
    eq = bits == thr
    n_ties_wanted = cap - jnp.sum(jnp.where(gt, 1.0, 0.0), axis=-1, keepdims=True)

    row = lax.broadcasted_iota(I32, (LANES, LANES), 0)
    col = lax.broadcasted_iota(I32, (LANES, LANES), 1)
    upper = jnp.where(row <= col, 1.0, 0.0).astype(BF16)
    lane = lax.broadcasted_iota(I32, (ne, LANES), 1)

    off = jnp.zeros((ne, 1), F32)
    sel_blocks = []
    for j in range(nblk):
        sl = slice(j * LANES, (j + 1) * LANES)
        eqf = jnp.where(eq[:, sl], 1.0, 0.0)
        incl = _dot(eqf.astype(BF16), upper) + off
        keep_tie = jnp.where(incl - eqf < n_ties_wanted, eqf, 0.0)
        sel_blocks.append(jnp.where(gt[:, sl], 1.0, keep_tie))
        off = incl[:, LANES - 1:LANES]

    off = jnp.zeros((ne, 1), F32)
    starts = jnp.zeros((ne, LANES), F32)
    for j in range(nblk):
        sl = slice(j * LANES, (j + 1) * LANES)
        self = sel_blocks[j]
        starts = jnp.where(lane == j, off, starts)
        incl = _dot(self.astype(BF16), upper) + off
        pos_ref[0, :, sl] = jnp.where(self > 0.0, incl - 1.0, -1.0).astype(I32)
        gate_ref[0, :, sl] = jnp.where(self > 0.0, a[:, sl], 0.0)
        off = incl[:, LANES - 1:LANES]
    starts = jnp.where(lane >= nblk, off, starts)
    starts_ref[0] = starts.astype(I32)


def _route(aff_t, cap):
    b, ne, t = aff_t.shape
    spec = pl.BlockSpec((1, ne, t), lambda i: (i, 0, 0))
    return pl.pallas_call(
        functools.partial(_route_kernel, cap),
        out_shape=(jax.ShapeDtypeStruct((b, ne, t), I32),
                   jax.ShapeDtypeStruct((b, ne, t), F32),
                   jax.ShapeDtypeStruct((b, ne, LANES), I32)),
        grid=(b,),
        in_specs=[spec],
        out_specs=(spec, spec, pl.BlockSpec((1, ne, LANES), lambda i: (i, 0, 0))),
        compiler_params=pltpu.CompilerParams(dimension_semantics=("arbitrary",),
                                             vmem_limit_bytes=VMEM_LIMIT),
        name="route",
    )(aff_t)


TOK_BLOCK = 2 * LANES


SLOT_ALIGN = 16
AUX = LANES


def _slot_windows(starts_ref, b, e0, n_exp, ne, j, win):
    per = TOK_BLOCK // LANES
    s_lo, s_hi, a0 = [], [], []
    n_pass = jnp.int32(0)
    for le in range(n_exp):
        base = (b * ne + e0 + le) * LANES
        lo = starts_ref[base + per * j]
        hi = starts_ref[base + per * (j + 1)]
        a = (lo // SLOT_ALIGN) * SLOT_ALIGN
        n_pass = jnp.maximum(n_pass, jnp.where(hi > lo, (hi - a + win - 1) // win, 0))
        s_lo.append(lo)
        s_hi.append(hi)
        a0.append(a)
    return s_lo, s_hi, a0, n_pass


def _window(s_lo, s_hi, a0, p, win, cap):
    a = a0 + p * win
    a_c = pl.multiple_of(jnp.minimum(a, cap - win), SLOT_ALIGN)
    return a_c, jnp.maximum(s_lo, a), jnp.minimum(s_hi, a + win)


def _gather_kernel(win, starts_ref, pos_ref, gate_ref, h2_ref, xs_ref, rhs_ref):
    b = pl.program_id(0)
    g = pl.program_id(1)
    j = pl.program_id(2)
    eg, cap = xs_ref.shape[1], xs_ref.shape[2]
    ne = gate_ref.shape[2]
    d = h2_ref.shape[2]

    @pl.when(j == 0)
    def _():
        xs_ref[...] = jnp.zeros_like(xs_ref)

    pr = lax.broadcasted_iota(I32, (ne, AUX), 0)
    pc = lax.broadcasted_iota(I32, (ne, AUX), 1)
    pieces = _split3(gate_ref[0])
    gcols = sum(_dot(piece, jnp.where(pc == k * ne + pr, 1.0, 0.0).astype(BF16))
                for k, piece in enumerate(pieces))
    rhs_ref[:, 0:d] = h2_ref[0]
    rhs_ref[:, d:d + AUX] = gcols.astype(BF16)

    s_lo, s_hi, a0, n_pass = _slot_windows(starts_ref, b, g * eg, eg, ne, j, win)
    row = lax.broadcasted_iota(I32, (win, TOK_BLOCK), 0)
    row_out = lax.broadcasted_iota(I32, (win, d + AUX), 0)

    def one_pass(p, carry):
        wins = [_window(s_lo[le], s_hi[le], a0[le], p, win, cap) for le in range(eg)]
        onehots = []
        for le, (a_c, lo_i, hi_i) in enumerate(wins):
            prow = pos_ref[0, le, pl.ds(j, 1), :]
            owned = jnp.where(prow >= lo_i, jnp.where(prow < hi_i, prow, -1), -1)
            onehots.append(jnp.where(owned == row + a_c, 1.0, 0.0).astype(BF16))
        res = _dot(jnp.concatenate(onehots, axis=0), rhs_ref[...])
        for le, (a_c, lo_i, hi_i) in enumerate(wins):
            rows = pl.ds(a_c, win)
            xs_ref[0, le, rows, :] = jnp.where(row_out + a_c < lo_i, xs_ref[0, le, rows, :],
                                               res[le * win:(le + 1) * win].astype(BF16))
        return carry

    lax.fori_loop(0, n_pass, one_pass, 0)


def _gather(starts, pos, gate_t, h2, cap, win, eg):
    b, t, d = h2.shape
    ne = pos.shape[1]
    ntb = t // TOK_BLOCK
    pos4 = pos.reshape(b, ne, ntb, TOK_BLOCK)
    grid_spec = pltpu.PrefetchScalarGridSpec(
        num_scalar_prefetch=1,
        grid=(b, ne // eg, ntb),
        in_specs=[pl.BlockSpec((1, eg, ntb, TOK_BLOCK), lambda bi, g, j, s: (bi, g, 0, 0)),
                  pl.BlockSpec((1, TOK_BLOCK, ne), lambda bi, g, j, s: (bi, j, 0)),
                  pl.BlockSpec((1, TOK_BLOCK, d), lambda bi, g, j, s: (bi, j, 0))],
        out_specs=pl.BlockSpec((1, eg, cap, d + AUX), lambda bi, g, j, s: (bi, g, 0, 0)),
        scratch_shapes=[pltpu.VMEM((TOK_BLOCK, d + AUX), BF16)])
    return pl.pallas_call(
        functools.partial(_gather_kernel, win),
        out_shape=jax.ShapeDtypeStruct((b, ne, cap, d + AUX), BF16),
        grid_spec=grid_spec,
        compiler_params=pltpu.CompilerParams(
            dimension_semantics=("arbitrary", "arbitrary", "arbitrary"), vmem_limit_bytes=VMEM_LIMIT),
        name="gather",
    )(starts.reshape(-1), pos4, gate_t, h2)


def _ffn_kernel(xs_ref, wg_ref, wu_ref, wd_ref, gf_ref, ye_ref, acc_ref):
    b = pl.program_id(0)
    e = pl.program_id(1)
    f = pl.program_id(2)
    ne = pl.num_programs(1)
    d = ye_ref.shape[-1]
    xs = xs_ref[0, 0, :, 0:d]
    a = _dot(xs, wg_ref[0].astype(BF16))
    u = _dot(xs, wu_ref[0].astype(BF16))
    contrib = _dot((_silu(a) * u).astype(BF16), wd_ref[0].astype(BF16))

    @pl.when(f == 0)
    def _():
        acc_ref[...] = contrib

    @pl.when(f > 0)
    def _():
        acc_ref[...] += contrib

    @pl.when(f == pl.num_programs(2) - 1)
    def _():
        aux = xs_ref[0, 0, :, d:d + AUX].astype(F32)
        lane = lax.broadcasted_iota(I32, aux.shape, 1)
        mine = jnp.where(lane < 3 * ne, jnp.where(lane % ne == e, aux, 0.0), 0.0)
        gate = jnp.sum(mine, axis=-1, keepdims=True)
        ye_ref[0, 0] = (acc_ref[...] * gate * gf_ref[pl.ds(b, 1), :]).astype(BF16)


def _ffn(xs, w_gate, w_up, w_down, mod, tf):
    b, ne, cap, daux = xs.shape
    d = daux - AUX
    ff = w_gate.shape[2]
    return pl.pallas_call(
        _ffn_kernel,
        out_shape=jax.ShapeDtypeStruct((b, ne, cap, d), BF16),
        grid=(b, ne, ff // tf),
        in_specs=[pl.BlockSpec((1, 1, cap, daux), lambda bi, e, f: (bi, e, 0, 0)),
                  pl.BlockSpec((1, d, tf), lambda bi, e, f: (e, 0, f)),
                  pl.BlockSpec((1, d, tf), lambda bi, e, f: (e, 0, f)),
                  pl.BlockSpec((1, tf, d), lambda bi, e, f: (e, f, 0)),
                  pl.BlockSpec((mod.shape[0], d), lambda bi, e, f: (0, N_MOD - 1))],
        out_specs=pl.BlockSpec((1, 1, cap, d), lambda bi, e, f: (bi, e, 0, 0)),
        scratch_shapes=[pltpu.VMEM((cap, d), F32)],
        compiler_params=pltpu.CompilerParams(dimension_semantics=("arbitrary", "arbitrary", "arbitrary"),
                                             vmem_limit_bytes=VMEM_LIMIT),
        name="ffn",
    )(xs, w_gate, w_up, w_down, mod)


def _combine_kernel(win, starts_ref, post_ref, ye_ref, x1_ref, fw_ref, out_ref, acc_ref, stack_ref):
    b = pl.program_id(0)
    j = pl.program_id(1)
    ne, cap = ye_ref.shape[1], ye_ref.shape[2]
    acc_ref[...] = x1_ref[0]
    s_lo, s_hi, a0, n_pass = _slot_windows(starts_ref, b, 0, ne, ne, j, win)
    col = lax.broadcasted_iota(I32, (TOK_BLOCK, win), 1)
    post = post_ref[0]

    def one_pass(p, carry):
        onehots = []
        for e in range(ne):
            a_c, lo_i, hi_i = _window(s_lo[e], s_hi[e], a0[e], p, win, cap)
            pcol = post[:, e:e + 1]
            owned = jnp.where(pcol >= lo_i, jnp.where(pcol < hi_i, pcol, -1), -1)
            onehots.append(jnp.where(owned == col + a_c, 1.0, 0.0).astype(BF16))
            stack_ref[e * win:(e + 1) * win, :] = ye_ref[0, e, pl.ds(a_c, win), :]
        acc_ref[...] += _dot(jnp.concatenate(onehots, axis=1), stack_ref[...])
        return carry

    lax.fori_loop(0, n_pass, one_pass, 0)
    out_ref[0] = _rms(acc_ref[...], fw_ref[...])


def _combine(starts, pos_t, ye, x1, final_w, win):
    b, t, d = x1.shape
    ne, cap = ye.shape[1], ye.shape[2]
    grid_spec = pltpu.PrefetchScalarGridSpec(
        num_scalar_prefetch=1,
        grid=(b, t // TOK_BLOCK),
        in_specs=[pl.BlockSpec((1, TOK_BLOCK, ne), lambda bi, j, s: (bi, j, 0)),
                  pl.BlockSpec((1, ne, cap, d), lambda bi, j, s: (bi, 0, 0, 0),
                               pipeline_mode=pl.Buffered(1)),
                  pl.BlockSpec((1, TOK_BLOCK, d), lambda bi, j, s: (bi, j, 0)),
                  pl.BlockSpec((1, d), lambda bi, j, s: (0, 0))],
        out_specs=pl.BlockSpec((1, TOK_BLOCK, d), lambda bi, j, s: (bi, j, 0)),
        scratch_shapes=[pltpu.VMEM((TOK_BLOCK, d), F32), pltpu.VMEM((ne * win, d), BF16)])
    return pl.pallas_call(
        functools.partial(_combine_kernel, win),
        out_shape=jax.ShapeDtypeStruct((b, t, d), F32),
        grid_spec=grid_spec,
        compiler_params=pltpu.CompilerParams(
            dimension_semantics=("arbitrary", "arbitrary"), vmem_limit_bytes=VMEM_LIMIT),
        name="combine",
    )(starts.reshape(-1), pos_t, ye, x1, final_w.reshape(1, d))


def kernel(x, c, ctx, c_ctx, ada_w, ada_b, norm_mix_w, norm_ffn_w, w_in, hgrn_lb_logits, hgrn_norm_w,
           sgu_norm_w, sgu_w, sgu_b, w_branch_a, w_branch_b, w_out, router_w, expert_w_gate,
           expert_w_up, expert_w_down, final_norm_w):
    b, t, d = x.shape
    assert b + 1 <= 8 and t % TOK_BLOCK == 0 and d % LANES == 0
    ne = router_w.shape[-1]
    cap = CAPACITY_FACTOR * t // ne
    win = min(cap, 128)
    assert cap % SLOT_ALIGN == 0 and t // LANES < LANES and 3 * ne <= AUX
    (x1, h2, aff_t), mod = _front(x, c, ctx, c_ctx, ada_w, ada_b, norm_mix_w, norm_ffn_w, w_in,
                                  hgrn_lb_logits, hgrn_norm_w, sgu_norm_w, sgu_w, sgu_b,
                                  w_branch_a, w_branch_b, w_out, router_w, 256)
    pos, gate, starts = _route(aff_t, cap)
    xs = _gather(starts, pos, jnp.swapaxes(gate, 1, 2), h2, cap, win, 4)
    ye = _ffn(xs, expert_w_gate[0], expert_w_up[0], expert_w_down[0], mod, 512)
    return _combine(starts, jnp.swapaxes(pos, 1, 2), ye, x1, final_norm_w, win)
```

```python
import functools

import numpy as np
import jax
import jax.numpy as jnp
from jax import lax
from jax.experimental import pallas as pl
from jax.experimental.pallas import tpu as pltpu

F32 = jnp.float32
BF16 = jnp.bfloat16
I32 = jnp.int32

LANES = 128
EPS = 1e-6
N_MOD = 6
HEADS = 4
HD = 128
WIDTH = HEADS * HD
CHUNK = 128
N_EXPERTS = 16
CAPACITY_FACTOR = 2
LEVELS = tuple(CHUNK >> (i + 1) for i in range(7))
N_DMAT = 2 + len(LEVELS)
PAIR = 2 * HD
N_PAIRS = HEADS // 2
VMEM_LIMIT = 52 * 1024 * 1024


def _dot(a, b):
    return jnp.dot(a, b, preferred_element_type=F32)


def _dot_nt(a, b):
    return lax.dot_general(a, b, (((1,), (1,)), ((), ())), preferred_element_type=F32)


def _dot_tn(a, b):
    return lax.dot_general(a, b, (((0,), (0,)), ((), ())), preferred_element_type=F32)


def _split2(x):
    hi = x.astype(BF16)
    lo = (x - hi.astype(F32)).astype(BF16)
    return hi, lo


def _split3(x):
    hi = x.astype(BF16)
    r = x - hi.astype(F32)
    mid = r.astype(BF16)
    lo = (r - mid.astype(F32)).astype(BF16)
    return hi, mid, lo


def _sigmoid(x):
    return 1.0 / (1.0 + jnp.exp(-x))


def _silu(x):
    return x * _sigmoid(x)


def _gelu_tanh(x):
    c = np.sqrt(2.0 / np.pi).astype(np.float32)
    return 0.5 * x * (1.0 + jnp.tanh(c * (x + 0.044715 * (x * x * x))))


def _rms(x, w):
    return x * lax.rsqrt(jnp.mean(x * x, axis=-1, keepdims=True) + EPS) * w


def _lower_bound(lbl, d):
    l0 = lbl[2 * d:2 * d + 1, :]
    l1 = lbl[2 * d + 1:2 * d + 2, :]
    m = jnp.maximum(l0, l1)
    e0 = jnp.exp(l0 - m)
    e1 = jnp.exp(l1 - m)
    return e0 / (e0 + e1)


def _forget(z, lb):
    f = lb + (1.0 - lb) * _sigmoid(z)
    return jnp.log(f), 1.0 - f


def _cumsum_dot(m_bf16, x):
    hi, mid, lo = _split3(x)
    return _dot(m_bf16, hi) + _dot(m_bf16, mid) + _dot(m_bf16, lo)


def _same_head(shape):
    r = lax.broadcasted_iota(I32, shape, 0) < HD
    c = lax.broadcasted_iota(I32, shape, 1) < HD
    return r == c


def _block_diag(x):
    first = lax.broadcasted_iota(I32, x.shape, 1) < HD
    zero = jnp.zeros_like(x)
    return jnp.concatenate([jnp.where(first, x, zero), jnp.where(first, zero, x)], axis=0)


def _pair_outer(v_pair, w_pair):
    full = _dot_tn(v_pair, w_pair)
    return jnp.where(_same_head(full.shape), full, 0.0)


def _decay_matrices(reverse):
    c = CHUNK
    i = np.arange(c)[:, None]
    m = np.arange(c)[None, :]
    mats = []
    if not reverse:
        mats.append(m <= i)
        mats.append(m > i)
    else:
        mats.append(m >= i)
        mats.append(m < i)
    for half in LEVELS:
        a = (i // (2 * half)) * (2 * half)
        mid = a + half
        if not reverse:
            qside = i >= mid
            mat = np.where(qside, (m >= mid) & (m <= i), (m > i) & (m < mid))
        else:
            qside = i < mid
            mat = np.where(qside, (m >= i) & (m < mid), (m >= mid) & (m < i))
        mats.append(mat)
    return np.concatenate(mats, axis=0).astype(np.float32)


def _block_decay_matrix(n, reverse):
    i = np.arange(n)[:, None]
    m = np.arange(n)[None, :]
    return ((m < i) if reverse else (m > i)).astype(np.float32)


def _adaln_kernel(cond_ref, w_ref, b_ref, out_ref):
    s = _silu(cond_ref[...])
    s_hi, s_lo = _split2(s)
    w_hi, w_lo = _split2(w_ref[...])
    out_ref[...] = _dot(s_hi, w_hi) + _dot(s_hi, w_lo) + _dot(s_lo, w_hi) + b_ref[...]


def _adaln(cond, ada_w, ada_b):
    rows, d = cond.shape
    n = ada_w.shape[1]
    tn = 1024
    return pl.pallas_call(
        _adaln_kernel,
        out_shape=jax.ShapeDtypeStruct((rows, n), F32),
        grid=(n // tn,),
        in_specs=[pl.BlockSpec((rows, d), lambda j: (0, 0)),
                  pl.BlockSpec((d, tn), lambda j: (0, j)),
                  pl.BlockSpec((1, tn), lambda j: (0, j))],
        out_specs=pl.BlockSpec((rows, tn), lambda j: (0, j)),
        compiler_params=pltpu.CompilerParams(dimension_semantics=("arbitrary",),
                                             vmem_limit_bytes=VMEM_LIMIT),
        name="adaln",
    )(cond, ada_w, ada_b.reshape(1, n))


def _mod_rows(mod_ref, row, d):
    return [mod_ref[pl.ds(row, 1), j * d:(j + 1) * d] for j in range(N_MOD)]


def _ctx_kernel(ctx_row, x_ref, mod_ref, nw_ref, w_ref, lbl_ref, mf_ref, mb_ref, sf_ref, sb_ref):
    d = x_ref.shape[-1]
    x = x_ref[0]
    sh, sc = mod_ref[pl.ds(ctx_row, 1), 0:d], mod_ref[pl.ds(ctx_row, 1), d:2 * d]
    h = _rms(x, nw_ref[...]) * (1.0 + sc) + sh
    p = _dot(h.astype(BF16), w_ref[...])
    v = p[:, 2 * WIDTH:3 * WIDTH].astype(BF16)
    lbl = lbl_ref[...]
    for dirn, (m_ref, out_ref) in enumerate(((mf_ref, sf_ref), (mb_ref, sb_ref))):
        logf, k = _forget(p[:, dirn * WIDTH:(dirn + 1) * WIDTH], _lower_bound(lbl, dirn))
        w = (k * jnp.exp(_cumsum_dot(m_ref[...], logf))).astype(BF16)
        for pr in range(N_PAIRS):
            sl = slice(pr * PAIR, (pr + 1) * PAIR)
            out_ref[0, pr] = _pair_outer(v[:, sl], w[:, sl])


def _ctx_states(ctx, mod, norm_w, w_c, lbl, ctx_row):
    b, l, d = ctx.shape
    mf = jnp.asarray(_block_decay_matrix(l, False), BF16)
    mb = jnp.asarray(_block_decay_matrix(l, True), BF16)
    full = lambda a: pl.BlockSpec(a.shape, lambda i: (0,) * a.ndim)
    st = jax.ShapeDtypeStruct((b, N_PAIRS, PAIR, PAIR), F32)
    st_spec = pl.BlockSpec((1, N_PAIRS, PAIR, PAIR), lambda i: (i, 0, 0, 0))
    return pl.pallas_call(
        functools.partial(_ctx_kernel, ctx_row),
        out_shape=(st, st),
        grid=(b,),
        in_specs=[pl.BlockSpec((1, l, d), lambda i: (i, 0, 0)), full(mod), full(norm_w),
                  full(w_c), full(lbl), full(mf), full(mb)],
        out_specs=(st_spec, st_spec),
        compiler_params=pltpu.CompilerParams(dimension_semantics=("arbitrary",),
                                             vmem_limit_bytes=VMEM_LIMIT),
        name="ctx_state",
    )(ctx, mod, norm_w, w_c, lbl, mf, mb)


def _bwd_state_kernel(x_ref, mod_ref, nw_ref, w_ref, lbl_ref, m_ref, s0_ref, out_ref, st_ref):
    b = pl.program_id(0)
    i = pl.program_id(1)
    d = x_ref.shape[-1]

    @pl.when(i == 0)
    def _():
        st_ref[...] = s0_ref[0]

    out_ref[0, 0] = st_ref[...]
    x = x_ref[0]
    sh, sc = mod_ref[pl.ds(b, 1), 0:d], mod_ref[pl.ds(b, 1), d:2 * d]
    h = _rms(x, nw_ref[...]) * (1.0 + sc) + sh
    p = _dot(h.astype(BF16), w_ref[...])
    logf, k = _forget(p[:, 0:WIDTH], _lower_bound(lbl_ref[...], 1))
    v = p[:, WIDTH:2 * WIDTH].astype(BF16)
    w = (k * jnp.exp(_cumsum_dot(m_ref[...], logf))).astype(BF16)
    tot = jnp.exp(jnp.sum(logf, axis=0, keepdims=True))
    for pr in range(N_PAIRS):
        sl = slice(pr * PAIR, (pr + 1) * PAIR)
        st_ref[pr] = st_ref[pr] * tot[:, sl] + _pair_outer(v[:, sl], w[:, sl])


def _bwd_states(x, mod, norm_w, w_bv, lbl, s_b0, tb):
    b, t, d = x.shape
    nb = t // tb
    m = jnp.asarray(_block_decay_matrix(tb, True), BF16)
    full = lambda a: pl.BlockSpec(a.shape, lambda bi, i: (0,) * a.ndim)
    return pl.pallas_call(
        _bwd_state_kernel,
        out_shape=jax.ShapeDtypeStruct((b, nb, N_PAIRS, PAIR, PAIR), F32),
        grid=(b, nb),
        in_specs=[pl.BlockSpec((1, tb, d), lambda bi, i: (bi, nb - 1 - i, 0)),
                  full(mod), full(norm_w), full(w_bv), full(lbl), full(m),
                  pl.BlockSpec((1, N_PAIRS, PAIR, PAIR), lambda bi, i: (bi, 0, 0, 0))],
        out_specs=pl.BlockSpec((1, 1, N_PAIRS, PAIR, PAIR), lambda bi, i: (bi, nb - 1 - i, 0, 0, 0)),
        scratch_shapes=[pltpu.VMEM((N_PAIRS, PAIR, PAIR), F32)],
        compiler_params=pltpu.CompilerParams(dimension_semantics=("arbitrary", "arbitrary"),
                                             vmem_limit_bytes=VMEM_LIMIT),
        name="bwd_state",
    )(x, mod, norm_w, w_bv, lbl, m, s_b0)


def _level_masks(reverse):
    row = lax.broadcasted_iota(I32, (CHUNK, PAIR), 0)
    col = lax.broadcasted_iota(I32, (CHUNK, PAIR), 1) & (CHUNK - 1)
    x = row ^ col
    out = []
    for half in LEVELS:
        in_pair = jnp.where(x >= half, jnp.where(x < 2 * half, 1.0, 0.0), 0.0)
        bit = (col if reverse else row) & half
        out.append((jnp.where(bit != 0, in_pair, 0.0), (row & half) != 0))
    return out


def _hgrn_chunk(q, k, logf, v, dm_ref, masks, reverse, state_ref, o_ref, r0):
    hi, lo = _split2(logf)
    dall = _dot(dm_ref[...], jnp.concatenate([hi, lo], axis=0))
    e_in = jnp.exp(dall[0:CHUNK])
    e_st = jnp.exp(dall[CHUNK:2 * CHUNK])
    tot_row = 0 if reverse else CHUNK - 1
    e_tot = e_in[tot_row:tot_row + 1]
    first = lax.broadcasted_iota(I32, (CHUNK, PAIR), 1) < HD
    for pr in range(N_PAIRS):
        sl = slice(pr * PAIR, (pr + 1) * PAIR)
        qp, kp, vp = q[:, sl], k[:, sl], v[:, sl]
        pm = jnp.zeros((CHUNK, PAIR), F32)
        for li in range(len(LEVELS)):
            mask, row_bit = masks[li]
            qside = jnp.logical_not(row_bit) if reverse else row_bit
            e = jnp.exp(dall[(2 + li) * CHUNK:(3 + li) * CHUNK, sl])
            xm = (jnp.where(qside, qp, kp) * e).astype(BF16)
            pm = pm + _dot_nt(xm, _block_diag(xm)) * mask
        qk = qp * kp
        diag = jnp.where(first, jnp.sum(qk[:, 0:HD], axis=-1, keepdims=True),
                         jnp.sum(qk[:, HD:PAIR], axis=-1, keepdims=True))
        st = state_ref[pr]
        o = (_dot(pm.astype(BF16), _block_diag(vp.astype(BF16))) + diag * vp
             + _dot_nt((qp * e_in[:, sl]).astype(BF16), st.astype(BF16)))
        o_ref[pl.ds(r0, CHUNK), sl] += o
        state_ref[pr] = st * e_tot[:, sl] + _pair_outer(vp.astype(BF16), (kp * e_st[:, sl]).astype(BF16))


def _mixer_kernel(x_ref, mod_ref, nmix_ref, nffn_ref, win_ref, lbl_ref, hnw_ref, snw_ref,
                  sw_ref, sbias_ref, wa_ref, wb_ref, wo_ref, rw_ref, dmf_ref, dmb_ref,
                  sf0_ref, bst_ref,
                  x1_ref, h2_ref, aff_ref,
                  p_ref, o_ref, sg_ref, stf_ref, stb_ref):
    b = pl.program_id(0)
    i = pl.program_id(1)
    tb, d = x_ref.shape[1], x_ref.shape[2]
    nch = tb // CHUNK
    sh_m, sc_m, g_m, sh_f, sc_f, _ = _mod_rows(mod_ref, b, d)

    @pl.when(i == 0)
    def _():
        stf_ref[...] = sf0_ref[0]

    stb_ref[...] = bst_ref[0, 0]

    x = x_ref[0]
    h = _rms(x, nmix_ref[...]) * (1.0 + sc_m) + sh_m
    p_ref[...] = _dot(h.astype(BF16), win_ref[...])
    o_ref[...] = jnp.zeros_like(o_ref)

    lbl = lbl_ref[...]
    c_q, c_ff, c_fb, c_v, c_g = (j * WIDTH for j in range(5))
    c_u, c_sv = 5 * WIDTH, 6 * WIDTH
    c_ga, c_gb = 7 * WIDTH, 7 * WIDTH + d

    for reverse, c_f, dm_ref, st_ref in ((False, c_ff, dmf_ref, stf_ref), (True, c_fb, dmb_ref, stb_ref)):
        masks = _level_masks(reverse)
        lb = _lower_bound(lbl, 1 if reverse else 0)
        order = range(nch - 1, -1, -1) if reverse else range(nch)
        for ci in order:
            r0 = ci * CHUNK
            rows = pl.ds(r0, CHUNK)
            q = _silu(p_ref[rows, c_q:c_q + WIDTH])
            logf, k = _forget(p_ref[rows, c_f:c_f + WIDTH], lb)
            v = p_ref[rows, c_v:c_v + WIDTH]
            _hgrn_chunk(q, k, logf, v, dm_ref, masks, reverse, st_ref, o_ref, r0)

    o = o_ref[...]
    on = jnp.concatenate(
        [o[:, hh * HD:(hh + 1) * HD]
         * lax.rsqrt(jnp.mean(o[:, hh * HD:(hh + 1) * HD] ** 2, axis=-1, keepdims=True) + EPS)
         for hh in range(HEADS)], axis=-1)
    a_in = on * hnw_ref[...] * _silu(p_ref[:, c_g:c_g + WIDTH])
    y_a = _dot(a_in.astype(BF16), wa_ref[...])

    u = _gelu_tanh(p_ref[:, c_u:c_u + WIDTH])
    sv = _gelu_tanh(p_ref[:, c_sv:c_sv + WIDTH])
    svn = jnp.concatenate(
        [sv[:, g * HD:(g + 1) * HD]
         * lax.rsqrt(jnp.mean(sv[:, g * HD:(g + 1) * HD] ** 2, axis=-1, keepdims=True) + EPS)
         for g in range(HEADS)], axis=-1)
    svn = (svn * snw_ref[...]).astype(BF16)
    for ci in range(nch):
        for g in range(HEADS):
            sl = slice(g * HD, (g + 1) * HD)
            sg_ref[ci * CHUNK:(ci + 1) * CHUNK, sl] = (
                _dot(sw_ref[g], svn[ci * CHUNK:(ci + 1) * CHUNK, sl]) + sbias_ref[:, sl])
    y_b = _dot((u * sg_ref[...]).astype(BF16), wb_ref[...])

    merged = (_sigmoid(p_ref[:, c_ga:c_ga + d]) * y_a + _sigmoid(p_ref[:, c_gb:c_gb + d]) * y_b)
    y = _dot(merged.astype(BF16), wo_ref[...])
    x1 = x + g_m * y
    x1_ref[0] = x1

    h2 = _rms(x1, nffn_ref[...]) * (1.0 + sc_f) + sh_f
    h2_hi, h2_lo = _split2(h2)
    h2_ref[0] = h2_hi
    rw_hi, rw_lo = _split2(rw_ref[...])
    logits = _dot_nt(rw_hi, h2_hi) + _dot_nt(rw_hi, h2_lo) + _dot_nt(rw_lo, h2_hi)
    mx = jnp.max(logits, axis=0, keepdims=True)
    ex = jnp.exp(logits - mx)
    aff_ref[0] = ex / jnp.sum(ex, axis=0, keepdims=True)


def _mixer(x, mod, nmix, nffn, w_in, lbl, hnw, snw, sgu_w, sgu_bias, w_a, w_b, w_o, rw_t,
           s_f0, bstates, tb):
    b, t, d = x.shape
    nb = t // tb
    ne = rw_t.shape[0]
    dmf = jnp.asarray(np.tile(_decay_matrices(False), (1, 2)), BF16)
    dmb = jnp.asarray(np.tile(_decay_matrices(True), (1, 2)), BF16)
    const = lambda a: pl.BlockSpec(a.shape, lambda bi, i: (0,) * a.ndim, pipeline_mode=pl.Buffered(1))
    in_specs = [pl.BlockSpec((1, tb, d), lambda bi, i: (bi, i, 0)),
                const(mod), const(nmix), const(nffn), const(w_in), const(lbl), const(hnw), const(snw),
                const(sgu_w), const(sgu_bias), const(w_a), const(w_b), const(w_o), const(rw_t),
                const(dmf), const(dmb),
                pl.BlockSpec((1, N_PAIRS, PAIR, PAIR), lambda bi, i: (bi, 0, 0, 0)),
                pl.BlockSpec((1, 1, N_PAIRS, PAIR, PAIR), lambda bi, i: (bi, i, 0, 0, 0))]
    out_shape = (jax.ShapeDtypeStruct((b, t, d), F32),
                 jax.ShapeDtypeStruct((b, t, d), BF16),
                 jax.ShapeDtypeStruct((b, ne, t), F32))
    out_specs = (pl.BlockSpec((1, tb, d), lambda bi, i: (bi, i, 0)),
                 pl.BlockSpec((1, tb, d), lambda bi, i: (bi, i, 0)),
                 pl.BlockSpec((1, ne, tb), lambda bi, i: (bi, 0, i)))
    return pl.pallas_call(
        _mixer_kernel,
        out_shape=out_shape,
        grid=(b, nb),
        in_specs=in_specs,
        out_specs=out_specs,
        scratch_shapes=[pltpu.VMEM((tb, w_in.shape[1]), F32),
                        pltpu.VMEM((tb, WIDTH), F32),
                        pltpu.VMEM((tb, WIDTH), F32),
                        pltpu.VMEM((N_PAIRS, PAIR, PAIR), F32),
                        pltpu.VMEM((N_PAIRS, PAIR, PAIR), F32)],
        compiler_params=pltpu.CompilerParams(dimension_semantics=("arbitrary", "arbitrary"),
                                             vmem_limit_bytes=VMEM_LIMIT),
        name="mixer",
    )(x, mod, nmix, nffn, w_in, lbl, hnw, snw, sgu_w, sgu_bias, w_a, w_b, w_o, rw_t,
      dmf, dmb, s_f0, bstates)


def _front(x, c, ctx, c_ctx, ada_w, ada_b, norm_mix_w, norm_ffn_w, w_in, hgrn_lb_logits,
           hgrn_norm_w, sgu_norm_w, sgu_w, sgu_b, w_branch_a, w_branch_b, w_out, router_w, tb):
    b, t, d = x.shape
    layer = 0
    cond = jnp.zeros((8, d), F32).at[0:b].set(c).at[b].set(c_ctx)
    mod = _adaln(cond, ada_w[layer], ada_b[layer])
    lbl = hgrn_lb_logits[:, layer:layer + 2, :].reshape(4, WIDTH)
    nmix = norm_mix_w[layer].reshape(1, d)
    nffn = norm_ffn_w[layer].reshape(1, d)
    w_in_b = w_in[layer].astype(BF16)
    s_f0, s_b0 = _ctx_states(ctx, mod, nmix, w_in_b[:, WIDTH:4 * WIDTH], lbl, b)
    w_bv = jnp.concatenate([w_in_b[:, 2 * WIDTH:3 * WIDTH], w_in_b[:, 3 * WIDTH:4 * WIDTH]], axis=1)
    bstates = _bwd_states(x, mod, nmix, w_bv, lbl, s_b0, tb)
    sgu_bias = jnp.repeat(sgu_b[layer].T, HD, axis=1)
    return _mixer(x, mod, nmix, nffn, w_in_b, lbl, hgrn_norm_w[layer].reshape(1, WIDTH),
                  sgu_norm_w[layer].reshape(1, WIDTH), sgu_w[layer].astype(BF16), sgu_bias,
                  w_branch_a[layer].astype(BF16), w_branch_b[layer].astype(BF16),
                  w_out[layer].astype(BF16), router_w[layer].T, s_f0, bstates, tb), mod


def _route_kernel(cap, aff_ref, pos_ref, gate_ref, starts_ref):
    a = aff_ref[0]
    ne, t = a.shape
    nblk = t // LANES
    bits = pltpu.bitcast(a, I32)

    def search(it, lo):
        cand = lo | (jnp.int32(1) << (30 - it))
        cnt = jnp.sum(jnp.where(bits >= cand, 1.0, 0.0), axis=-1, keepdims=True)
        return jnp.where(cnt >= cap, cand, lo)

    thr = lax.fori_loop(0, 31, search, jnp.zeros((ne, 1), I32))
    gt = bits > thr
    eq = bits == thr
    n_ties_wanted = cap - jnp.sum(jnp.where(gt, 1.0, 0.0), axis=-1, keepdims=True)

    row = lax.broadcasted_iota(I32, (LANES, LANES), 0)
    col = lax.broadcasted_iota(I32, (LANES, LANES), 1)
    upper = jnp.where(row <= col, 1.0, 0.0).astype(BF16)
    lane = lax.broadcasted_iota(I32, (ne, LANES), 1)

    off = jnp.zeros((ne, 1), F32)
    sel_blocks = []
    for j in range(nblk):
        sl = slice(j * LANES, (j + 1) * LANES)
        eqf = jnp.where(eq[:, sl], 1.0, 0.0)
        incl = _dot(eqf.astype(BF16), upper) + off
        keep_tie = jnp.where(incl - eqf < n_ties_wanted, eqf, 0.0)
        sel_blocks.append(jnp.where(gt[:, sl], 1.0, keep_tie))
        off = incl[:, LANES - 1:LANES]

    off = jnp.zeros((ne, 1), F32)
    starts = jnp.zeros((ne, LANES), F32)
    for j in range(nblk):
        sl = slice(j * LANES, (j + 1) * LANES)
        self = sel_blocks[j]
        starts = jnp.where(lane == j, off, starts)
        incl = _dot(self.astype(BF16), upper) + off
        pos_ref[0, :, sl] = jnp.where(self > 0.0, incl - 1.0, -1.0).astype(I32)
        gate_ref[0, :, sl] = jnp.where(self > 0.0, a[:, sl], 0.0)
        off = incl[:, LANES - 1:LANES]
    starts = jnp.where(lane >= nblk, off, starts)
    starts_ref[0] = starts.astype(I32)


def _route(aff_t, cap):
    b, ne, t = aff_t.shape
    spec = pl.BlockSpec((1, ne, t), lambda i: (i, 0, 0))
    return pl.pallas_call(
        functools.partial(_route_kernel, cap),
        out_shape=(jax.ShapeDtypeStruct((b, ne, t), I32),
                   jax.ShapeDtypeStruct((b, ne, t), F32),
                   jax.ShapeDtypeStruct((b, ne, LANES), I32)),
        grid=(b,),
        in_specs=[spec],
        out_specs=(spec, spec, pl.BlockSpec((1, ne, LANES), lambda i: (i, 0, 0))),
        compiler_params=pltpu.CompilerParams(dimension_semantics=("arbitrary",),
                                             vmem_limit_bytes=VMEM_LIMIT),
        name="route",
    )(aff_t)


TOK_BLOCK = 2 * LANES


SLOT_ALIGN = 16
AUX = LANES


def _slot_windows(starts_ref, b, e0, n_exp, ne, j, win):
    per = TOK_BLOCK // LANES
    s_lo, s_hi, a0 = [], [], []
    n_pass = jnp.int32(0)
    for le in range(n_exp):
        base = (b * ne + e0 + le) * LANES
        lo = starts_ref[base + per * j]
        hi = starts_ref[base + per * (j + 1)]
        a = (lo // SLOT_ALIGN) * SLOT_ALIGN
        n_pass = jnp.maximum(n_pass, jnp.where(hi > lo, (hi - a + win - 1) // win, 0))
        s_lo.append(lo)
        s_hi.append(hi)
        a0.append(a)
    return s_lo, s_hi, a0, n_pass


def _window(s_lo, s_hi, a0, p, win, cap):
    a = a0 + p * win
    a_c = pl.multiple_of(jnp.minimum(a, cap - win), SLOT_ALIGN)
    return a_c, jnp.maximum(s_lo, a), jnp.minimum(s_hi, a + win)


def _gather_kernel(win, starts_ref, pos_ref, gate_ref, h2_ref, xs_ref):
    b = pl.program_id(0)
    g = pl.program_id(1)
    j = pl.program_id(2)
    eg, cap = xs_ref.shape[1], xs_ref.shape[2]
    ne = gate_ref.shape[2]
    d = h2_ref.shape[2]

    @pl.when(j == 0)
    def _():
        xs_ref[...] = jnp.zeros_like(xs_ref)

    pr = lax.broadcasted_iota(I32, (ne, AUX), 0)
    pc = lax.broadcasted_iota(I32, (ne, AUX), 1)
    pieces = _split3(gate_ref[0])
    gcols = sum(_dot(piece, jnp.where(pc == k * ne + pr, 1.0, 0.0).astype(BF16))
                for k, piece in enumerate(pieces)).astype(BF16)

    s_lo, s_hi, a0, n_pass = _slot_windows(starts_ref, b, g * eg, eg, ne, j, win)
    row = lax.broadcasted_iota(I32, (win, TOK_BLOCK), 0)
    row_d = lax.broadcasted_iota(I32, (win, d), 0)
    row_aux = lax.broadcasted_iota(I32, (win, AUX), 0)

    def one_pass(p, carry):
        wins = [_window(s_lo[le], s_hi[le], a0[le], p, win, cap) for le in range(eg)]
        onehots = []
        for le, (a_c, lo_i, hi_i) in enumerate(wins):
            prow = pos_ref[0, le, pl.ds(j, 1), :]
            owned = jnp.where(prow >= lo_i, jnp.where(prow < hi_i, prow, -1), -1)
            onehots.append(jnp.where(owned == row + a_c, 1.0, 0.0).astype(BF16))
        lhs = jnp.concatenate(onehots, axis=0)
        res = _dot(lhs, h2_ref[0])
        res_aux = _dot(lhs, gcols)
        for le, (a_c, lo_i, hi_i) in enumerate(wins):
            rows = pl.ds(a_c, win)
            mine = slice(le * win, (le + 1) * win)
            xs_ref[0, le, rows, 0:d] = jnp.where(row_d + a_c < lo_i, xs_ref[0, le, rows, 0:d],
                                                 res[mine].astype(BF16))
            xs_ref[0, le, rows, d:d + AUX] = jnp.where(row_aux + a_c < lo_i, xs_ref[0, le, rows, d:d + AUX],
                                                       res_aux[mine].astype(BF16))
        return carry

    lax.fori_loop(0, n_pass, one_pass, 0)


def _gather(starts, pos, gate_t, h2, cap, win, eg):
    b, t, d = h2.shape
    ne = pos.shape[1]
    ntb = t // TOK_BLOCK
    pos4 = pos.reshape(b, ne, ntb, TOK_BLOCK)
    grid_spec = pltpu.PrefetchScalarGridSpec(
        num_scalar_prefetch=1,
        grid=(b, ne // eg, ntb),
        in_specs=[pl.BlockSpec((1, eg, ntb, TOK_BLOCK), lambda bi, g, j, s: (bi, g, 0, 0)),
                  pl.BlockSpec((1, TOK_BLOCK, ne), lambda bi, g, j, s: (bi, j, 0)),
                  pl.BlockSpec((1, TOK_BLOCK, d), lambda bi, g, j, s: (bi, j, 0))],
        out_specs=pl.BlockSpec((1, eg, cap, d + AUX), lambda bi, g, j, s: (bi, g, 0, 0)))
    return pl.pallas_call(
        functools.partial(_gather_kernel, win),
        out_shape=jax.ShapeDtypeStruct((b, ne, cap, d + AUX), BF16),
        grid_spec=grid_spec,
        compiler_params=pltpu.CompilerParams(
            dimension_semantics=("arbitrary", "arbitrary", "arbitrary"), vmem_limit_bytes=VMEM_LIMIT),
        name="gather",
    )(starts.reshape(-1), pos4, gate_t, h2)


FF_SUB = 512


def _ffn_kernel(xs_ref, wg_ref, wu_ref, wd_ref, gf_ref, ye_ref, acc_ref):
    b = pl.program_id(0)
    e = pl.program_id(1)
    f = pl.program_id(2)
    ne = pl.num_programs(1)
    d = ye_ref.shape[-1]
    xs = xs_ref[0, 0, :, 0:d]
    tf = wg_ref.shape[2]

    @pl.when(f == 0)
    def _():
        acc_ref[...] = jnp.zeros_like(acc_ref)

    acc = acc_ref[...]
    for c0 in range(0, tf, FF_SUB):
        a = _dot(xs, wg_ref[0, :, c0:c0 + FF_SUB].astype(BF16))
        u = _dot(xs, wu_ref[0, :, c0:c0 + FF_SUB].astype(BF16))
        acc = acc + _dot((_silu(a) * u).astype(BF16), wd_ref[0, c0:c0 + FF_SUB, :].astype(BF16))
    acc_ref[...] = acc
    aux = xs_ref[0, 0, :, d:d + AUX].astype(F32)
    lane = lax.broadcasted_iota(I32, aux.shape, 1)
    mine = jnp.where(lane < 3 * ne, jnp.where(lane % ne == e, aux, 0.0), 0.0)
    gate = jnp.sum(mine, axis=-1, keepdims=True)
    ye_ref[0, 0] = (acc * gate * gf_ref[pl.ds(b, 1), :]).astype(BF16)


def _ffn(xs, w_gate, w_up, w_down, mod, tf):
    b, ne, cap, daux = xs.shape
    d = daux - AUX
    ff = w_gate.shape[2]
    return pl.pallas_call(
        _ffn_kernel,
        out_shape=jax.ShapeDtypeStruct((b, ne, cap, d), BF16),
        grid=(b, ne, ff // tf),
        in_specs=[pl.BlockSpec((1, 1, cap, daux), lambda bi, e, f: (bi, e, 0, 0)),
                  pl.BlockSpec((1, d, tf), lambda bi, e, f: (e, 0, f)),
                  pl.BlockSpec((1, d, tf), lambda bi, e, f: (e, 0, f)),
                  pl.BlockSpec((1, tf, d), lambda bi, e, f: (e, f, 0)),
                  pl.BlockSpec((mod.shape[0], d), lambda bi, e, f: (0, N_MOD - 1))],
        out_specs=pl.BlockSpec((1, 1, cap, d), lambda bi, e, f: (bi, e, 0, 0)),
        scratch_shapes=[pltpu.VMEM((cap, d), F32)],
        compiler_params=pltpu.CompilerParams(dimension_semantics=("arbitrary", "arbitrary", "arbitrary"),
                                             vmem_limit_bytes=VMEM_LIMIT),
        name="ffn",
    )(xs, w_gate, w_up, w_down, mod)


def _combine_kernel(win, starts_ref, post_ref, ye_ref, x1_ref, fw_ref, out_ref, acc_ref, stack_ref):
    b = pl.program_id(0)
    j = pl.program_id(1)
    ne, cap = ye_ref.shape[1], ye_ref.shape[2]
    acc_ref[...] = x1_ref[0]
    s_lo, s_hi, a0, n_pass = _slot_windows(starts_ref, b, 0, ne, ne, j, win)
    col = lax.broadcasted_iota(I32, (TOK_BLOCK, win), 1)
    post = post_ref[0]

    def one_pass(p, carry):
        onehots = []
        for e in range(ne):
            a_c, lo_i, hi_i = _window(s_lo[e], s_hi[e], a0[e], p, win, cap)
            pcol = post[:, e:e + 1]
            owned = jnp.where(pcol >= lo_i, jnp.where(pcol < hi_i, pcol, -1), -1)
            onehots.append(jnp.where(owned == col + a_c, 1.0, 0.0).astype(BF16))
            stack_ref[e * win:(e + 1) * win, :] = ye_ref[0, e, pl.ds(a_c, win), :]
        acc_ref[...] += _dot(jnp.concatenate(onehots, axis=1), stack_ref[...])
        return carry

    lax.fori_loop(0, n_pass, one_pass, 0)
    out_ref[0] = _rms(acc_ref[...], fw_ref[...])


def _combine(starts, pos_t, ye, x1, final_w, win):
    b, t, d = x1.shape
    ne, cap = ye.shape[1], ye.shape[2]
    grid_spec = pltpu.PrefetchScalarGridSpec(
        num_scalar_prefetch=1,
        grid=(b, t // TOK_BLOCK),
        in_specs=[pl.BlockSpec((1, TOK_BLOCK, ne), lambda bi, j, s: (bi, j, 0)),
                  pl.BlockSpec((1, ne, cap, d), lambda bi, j, s: (bi, 0, 0, 0),
                               pipeline_mode=pl.Buffered(1)),
                  pl.BlockSpec((1, TOK_BLOCK, d), lambda bi, j, s: (bi, j, 0)),
                  pl.BlockSpec((1, d), lambda bi, j, s: (0, 0))],
        out_specs=pl.BlockSpec((1, TOK_BLOCK, d), lambda bi, j, s: (bi, j, 0)),
        scratch_shapes=[pltpu.VMEM((TOK_BLOCK, d), F32), pltpu.VMEM((ne * win, d), BF16)])
    return pl.pallas_call(
        functools.partial(_combine_kernel, win),
        out_shape=jax.ShapeDtypeStruct((b, t, d), F32),
        grid_spec=grid_spec,
        compiler_params=pltpu.CompilerParams(
            dimension_semantics=("arbitrary", "arbitrary"), vmem_limit_bytes=VMEM_LIMIT),
        name="combine",
    )(starts.reshape(-1), pos_t, ye, x1, final_w.reshape(1, d))


def kernel(x, c, ctx, c_ctx, ada_w, ada_b, norm_mix_w, norm_ffn_w, w_in, hgrn_lb_logits, hgrn_norm_w,
           sgu_norm_w, sgu_w, sgu_b, w_branch_a, w_branch_b, w_out, router_w, expert_w_gate,
           expert_w_up, expert_w_down, final_norm_w):
    b, t, d = x.shape
    assert b + 1 <= 8 and t % TOK_BLOCK == 0 and d % LANES == 0
    ne = router_w.shape[-1]
    cap = CAPACITY_FACTOR * t // ne
    win = min(cap, 64)
    assert cap % SLOT_ALIGN == 0 and t // LANES < LANES and 3 * ne <= AUX
    (x1, h2, aff_t), mod = _front(x, c, ctx, c_ctx, ada_w, ada_b, norm_mix_w, norm_ffn_w, w_in,
                                  hgrn_lb_logits, hgrn_norm_w, sgu_norm_w, sgu_w, sgu_b,
                                  w_branch_a, w_branch_b, w_out, router_w, 256)
    pos, gate, starts = _route(aff_t, cap)
    xs = _gather(starts, pos, jnp.swapaxes(gate, 1, 2), h2, cap, win, 8)
    ye = _ffn(xs, expert_w_gate[0], expert_w_up[0], expert_w_down[0], mod, 1024)
    return _combine(starts, jnp.swapaxes(pos, 1, 2), ye, x1, final_norm_w, win)
```

```python
import functools

import numpy as np
import jax
import jax.numpy as jnp
from jax import lax
from jax.experimental import pallas as pl
from jax.experimental.pallas import tpu as pltpu

F32 = jnp.float32
BF16 = jnp.bfloat16
I32 = jnp.int32

LANES = 128
EPS = 1e-6
N_MOD = 6
HEADS = 4
HD = 128
WIDTH = HEADS * HD
CHUNK = 128
N_EXPERTS = 16
CAPACITY_FACTOR = 2
LEVELS = tuple(CHUNK >> (i + 1) for i in range(7))
N_DMAT = 2 + len(LEVELS)
PAIR = 2 * HD
N_PAIRS = HEADS // 2
VMEM_LIMIT = 52 * 1024 * 1024


def _dot(a, b):
    return jnp.dot(a, b, preferred_element_type=F32)


def _dot_nt(a, b):
    return lax.dot_general(a, b, (((1,), (1,)), ((), ())), preferred_element_type=F32)


def _dot_tn(a, b):
    return lax.dot_general(a, b, (((0,), (0,)), ((), ())), preferred_element_type=F32)


def _split2(x):
    hi = x.astype(BF16)
    lo = (x - hi.astype(F32)).astype(BF16)
    return hi, lo


def _split3(x):
    hi = x.astype(BF16)
    r = x - hi.astype(F32)
    mid = r.astype(BF16)
    lo = (r - mid.astype(F32)).astype(BF16)
    return hi, mid, lo


def _sigmoid(x):
    return 1.0 / (1.0 + jnp.exp(-x))


def _silu(x):
    return x * _sigmoid(x)


def _gelu_tanh(x):
    c = np.sqrt(2.0 / np.pi).astype(np.float32)
    return 0.5 * x * (1.0 + jnp.tanh(c * (x + 0.044715 * (x * x * x))))


def _rms(x, w):
    return x * lax.rsqrt(jnp.mean(x * x, axis=-1, keepdims=True) + EPS) * w


def _lower_bound(lbl, d):
    l0 = lbl[2 * d:2 * d + 1, :]
    l1 = lbl[2 * d + 1:2 * d + 2, :]
    m = jnp.maximum(l0, l1)
    e0 = jnp.exp(l0 - m)
    e1 = jnp.exp(l1 - m)
    return e0 / (e0 + e1)


def _forget(z, lb):
    f = lb + (1.0 - lb) * _sigmoid(z)
    return jnp.log(f), 1.0 - f


def _cumsum_dot(m_bf16, x):
    hi, mid, lo = _split3(x)
    return _dot(m_bf16, hi) + _dot(m_bf16, mid) + _dot(m_bf16, lo)


def _same_head(shape):
    r = lax.broadcasted_iota(I32, shape, 0) < HD
    c = lax.broadcasted_iota(I32, shape, 1) < HD
    return r == c


def _block_diag(x):
    first = lax.broadcasted_iota(I32, x.shape, 1) < HD
    zero = jnp.zeros_like(x)
    return jnp.concatenate([jnp.where(first, x, zero), jnp.where(first, zero, x)], axis=0)


def _pair_outer(v_pair, w_pair):
    full = _dot_tn(v_pair, w_pair)
    return jnp.where(_same_head(full.shape), full, 0.0)


def _decay_matrices(reverse):
    c = CHUNK
    i = np.arange(c)[:, None]
    m = np.arange(c)[None, :]
    mats = []
    if not reverse:
        mats.append(m <= i)
        mats.append(m > i)
    else:
        mats.append(m >= i)
        mats.append(m < i)
    for half in LEVELS:
        a = (i // (2 * half)) * (2 * half)
        mid = a + half
        if not reverse:
            qside = i >= mid
            mat = np.where(qside, (m >= mid) & (m <= i), (m > i) & (m < mid))
        else:
            qside = i < mid
            mat = np.where(qside, (m >= i) & (m < mid), (m >= mid) & (m < i))
        mats.append(mat)
    return np.concatenate(mats, axis=0).astype(np.float32)


def _block_decay_matrix(n, reverse):
    i = np.arange(n)[:, None]
    m = np.arange(n)[None, :]
    return ((m < i) if reverse else (m > i)).astype(np.float32)


def _adaln_kernel(cond_ref, w_ref, b_ref, out_ref):
    s = _silu(cond_ref[...])
    s_hi, s_lo = _split2(s)
    w_hi, w_lo = _split2(w_ref[...])
    out_ref[...] = _dot(s_hi, w_hi) + _dot(s_hi, w_lo) + _dot(s_lo, w_hi) + b_ref[...]


def _adaln(cond, ada_w, ada_b):
    rows, d = cond.shape
    n = ada_w.shape[1]
    tn = 1024
    return pl.pallas_call(
        _adaln_kernel,
        out_shape=jax.ShapeDtypeStruct((rows, n), F32),
        grid=(n // tn,),
        in_specs=[pl.BlockSpec((rows, d), lambda j: (0, 0)),
                  pl.BlockSpec((d, tn), lambda j: (0, j)),
                  pl.BlockSpec((1, tn), lambda j: (0, j))],
        out_specs=pl.BlockSpec((rows, tn), lambda j: (0, j)),
        compiler_params=pltpu.CompilerParams(dimension_semantics=("arbitrary",),
                                             vmem_limit_bytes=VMEM_LIMIT),
        name="adaln",
    )(cond, ada_w, ada_b.reshape(1, n))


def _mod_rows(mod_ref, row, d):
    return [mod_ref[pl.ds(row, 1), j * d:(j + 1) * d] for j in range(N_MOD)]


def _ctx_kernel(ctx_row, x_ref, mod_ref, nw_ref, w_ref, lbl_ref, mf_ref, mb_ref, sf_ref, sb_ref):
    d = x_ref.shape[-1]
    x = x_ref[0]
    sh, sc = mod_ref[pl.ds(ctx_row, 1), 0:d], mod_ref[pl.ds(ctx_row, 1), d:2 * d]
    h = _rms(x, nw_ref[...]) * (1.0 + sc) + sh
    p = _dot(h.astype(BF16), w_ref[...])
    v = p[:, 2 * WIDTH:3 * WIDTH].astype(BF16)
    lbl = lbl_ref[...]
    for dirn, (m_ref, out_ref) in enumerate(((mf_ref, sf_ref), (mb_ref, sb_ref))):
        logf, k = _forget(p[:, dirn * WIDTH:(dirn + 1) * WIDTH], _lower_bound(lbl, dirn))
        w = (k * jnp.exp(_cumsum_dot(m_ref[...], logf))).astype(BF16)
        for pr in range(N_PAIRS):
            sl = slice(pr * PAIR, (pr + 1) * PAIR)
            out_ref[0, pr] = _pair_outer(v[:, sl], w[:, sl])


def _ctx_states(ctx, mod, norm_w, w_c, lbl, ctx_row):
    b, l, d = ctx.shape
    mf = jnp.asarray(_block_decay_matrix(l, False), BF16)
    mb = jnp.asarray(_block_decay_matrix(l, True), BF16)
    full = lambda a: pl.BlockSpec(a.shape, lambda i: (0,) * a.ndim)
    st = jax.ShapeDtypeStruct((b, N_PAIRS, PAIR, PAIR), F32)
    st_spec = pl.BlockSpec((1, N_PAIRS, PAIR, PAIR), lambda i: (i, 0, 0, 0))
    return pl.pallas_call(
        functools.partial(_ctx_kernel, ctx_row),
        out_shape=(st, st),
        grid=(b,),
        in_specs=[pl.BlockSpec((1, l, d), lambda i: (i, 0, 0)), full(mod), full(norm_w),
                  full(w_c), full(lbl), full(mf), full(mb)],
        out_specs=(st_spec, st_spec),
        compiler_params=pltpu.CompilerParams(dimension_semantics=("arbitrary",),
                                             vmem_limit_bytes=VMEM_LIMIT),
        name="ctx_state",
    )(ctx, mod, norm_w, w_c, lbl, mf, mb)


A_Q, A_LF, A_KF, A_LB, A_KB, A_V, A_G, A_U, A_SV = (j * WIDTH for j in range(9))
A_GATES = 9 * WIDTH


def _in_proj_kernel(x_ref, mod_ref, nw_ref, w_ref, lbl_ref, snw_ref, act_ref):
    b = pl.program_id(0)
    d = x_ref.shape[-1]
    sh, sc = mod_ref[pl.ds(b, 1), 0:d], mod_ref[pl.ds(b, 1), d:2 * d]
    h = (_rms(x_ref[0], nw_ref[...]) * (1.0 + sc) + sh).astype(BF16)
    lbl = lbl_ref[...]

    def proj(j):
        return _dot(h, w_ref[:, j * WIDTH:(j + 1) * WIDTH])

    act_ref[0, :, A_Q:A_Q + WIDTH] = _silu(proj(0))
    for j, a_l, a_k, dirn in ((1, A_LF, A_KF, 0), (2, A_LB, A_KB, 1)):
        logf, k = _forget(proj(j), _lower_bound(lbl, dirn))
        act_ref[0, :, a_l:a_l + WIDTH] = logf
        act_ref[0, :, a_k:a_k + WIDTH] = k
    act_ref[0, :, A_V:A_V + WIDTH] = proj(3)
    act_ref[0, :, A_G:A_G + WIDTH] = _silu(proj(4))
    act_ref[0, :, A_U:A_U + WIDTH] = _gelu_tanh(proj(5))
    act_ref[0, :, A_SV:A_SV + WIDTH] = _group_rms(_gelu_tanh(proj(6))) * snw_ref[...]
    for j in range(2 * d // WIDTH):
        act_ref[0, :, A_GATES + j * WIDTH:A_GATES + (j + 1) * WIDTH] = _sigmoid(proj(7 + j))


def _in_proj(x, mod, norm_w, w_in, lbl, snw, tm):
    b, t, d = x.shape
    full = lambda a: pl.BlockSpec(a.shape, lambda bi, i: (0,) * a.ndim, pipeline_mode=pl.Buffered(1))
    n_act = A_GATES + 2 * d
    return pl.pallas_call(
        _in_proj_kernel,
        out_shape=jax.ShapeDtypeStruct((b, t, n_act), F32),
        grid=(b, t // tm),
        in_specs=[pl.BlockSpec((1, tm, d), lambda bi, i: (bi, i, 0)), full(mod), full(norm_w),
                  full(w_in), full(lbl), full(snw)],
        out_specs=pl.BlockSpec((1, tm, n_act), lambda bi, i: (bi, i, 0)),
        compiler_params=pltpu.CompilerParams(dimension_semantics=("arbitrary", "arbitrary"),
                                             vmem_limit_bytes=VMEM_LIMIT),
        name="in_proj",
    )(x, mod, norm_w, w_in, lbl, snw)


def _bwd_state_kernel(lf_ref, k_ref, v_ref, m_ref, s0_ref, out_ref, st_ref):
    i = pl.program_id(1)

    @pl.when(i == 0)
    def _():
        st_ref[...] = s0_ref[0]

    out_ref[0, 0] = st_ref[...]
    logf, k = lf_ref[0], k_ref[0]
    v = v_ref[0].astype(BF16)
    w = (k * jnp.exp(_cumsum_dot(m_ref[...], logf))).astype(BF16)
    tot = jnp.exp(jnp.sum(logf, axis=0, keepdims=True))
    for pr in range(N_PAIRS):
        sl = slice(pr * PAIR, (pr + 1) * PAIR)
        st_ref[pr] = st_ref[pr] * tot[:, sl] + _pair_outer(v[:, sl], w[:, sl])


def _bwd_states(act, s_b0, tb):
    b, t, _ = act.shape
    nb = t // tb
    m = jnp.asarray(_block_decay_matrix(tb, True), BF16)
    full = lambda a: pl.BlockSpec(a.shape, lambda bi, i: (0,) * a.ndim)
    col = lambda c0: pl.BlockSpec((1, tb, WIDTH), lambda bi, i: (bi, nb - 1 - i, c0 // WIDTH))
    return pl.pallas_call(
        _bwd_state_kernel,
        out_shape=jax.ShapeDtypeStruct((b, nb, N_PAIRS, PAIR, PAIR), F32),
        grid=(b, nb),
        in_specs=[col(A_LB), col(A_KB), col(A_V), full(m),
                  pl.BlockSpec((1, N_PAIRS, PAIR, PAIR), lambda bi, i: (bi, 0, 0, 0))],
        out_specs=pl.BlockSpec((1, 1, N_PAIRS, PAIR, PAIR), lambda bi, i: (bi, nb - 1 - i, 0, 0, 0)),
        scratch_shapes=[pltpu.VMEM((N_PAIRS, PAIR, PAIR), F32)],
        compiler_params=pltpu.CompilerParams(dimension_semantics=("arbitrary", "arbitrary"),
                                             vmem_limit_bytes=VMEM_LIMIT),
        name="bwd_state",
    )(act, act, act, m, s_b0)


def _level_masks(reverse):
    row = lax.broadcasted_iota(I32, (CHUNK, PAIR), 0)
    col = lax.broadcasted_iota(I32, (CHUNK, PAIR), 1) & (CHUNK - 1)
    x = row ^ col
    out = []
    for half in LEVELS:
        in_pair = jnp.where(x >= half, jnp.where(x < 2 * half, 1.0, 0.0), 0.0)
        bit = (col if reverse else row) & half
        out.append((jnp.where(bit != 0, in_pair, 0.0), (row & half) != 0))
    return out


def _hgrn_scores(q, k, logf, dm_ref, masks, reverse, pm_ref, dio_ref):
    hi, lo = _split2(logf)
    dall = _dot(dm_ref[...], jnp.concatenate([hi, lo], axis=0))
    dio_ref[...] = dall[0:2 * CHUNK]
    for pr in range(N_PAIRS):
        sl = slice(pr * PAIR, (pr + 1) * PAIR)
        qp, kp = q[:, sl], k[:, sl]
        pm = jnp.zeros((CHUNK, PAIR), F32)
        for li in range(len(LEVELS)):
            mask, row_bit = masks[li]
            qside = jnp.logical_not(row_bit) if reverse else row_bit
            e = jnp.exp(dall[(2 + li) * CHUNK:(3 + li) * CHUNK, sl])
            xm = (jnp.where(qside, qp, kp) * e).astype(BF16)
            pm = pm + _dot_nt(xm, _block_diag(xm)) * mask
        pm_ref[pr] = pm.astype(BF16)


def _hgrn_apply(q, k, v, reverse, pm_ref, dio_ref, state_ref, o_ref, r0):
    e_in = jnp.exp(dio_ref[0:CHUNK])
    e_st = jnp.exp(dio_ref[CHUNK:2 * CHUNK])
    tot_row = 0 if reverse else CHUNK - 1
    e_tot = e_in[tot_row:tot_row + 1]
    first = lax.broadcasted_iota(I32, (CHUNK, PAIR), 1) < HD
    for pr in range(N_PAIRS):
        sl = slice(pr * PAIR, (pr + 1) * PAIR)
        qp, kp, vp = q[:, sl], k[:, sl], v[:, sl]
        qk = qp * kp
        diag = jnp.where(first, jnp.sum(qk[:, 0:HD], axis=-1, keepdims=True),
                         jnp.sum(qk[:, HD:PAIR], axis=-1, keepdims=True))
        st = state_ref[pr]
        o = (_dot(pm_ref[pr], _block_diag(vp.astype(BF16))) + diag * vp
             + _dot_nt((qp * e_in[:, sl]).astype(BF16), st.astype(BF16)))
        o_ref[pl.ds(r0, CHUNK), sl] += o
        state_ref[pr] = st * e_tot[:, sl] + _pair_outer(vp.astype(BF16), (kp * e_st[:, sl]).astype(BF16))


def _group_rms(x):
    return jnp.concatenate(
        [x[:, g * HD:(g + 1) * HD]
         * lax.rsqrt(jnp.mean(x[:, g * HD:(g + 1) * HD] ** 2, axis=-1, keepdims=True) + EPS)
         for g in range(HEADS)], axis=-1)


def _mixer_kernel(act_ref, x_ref, mod_ref, nffn_ref, hnw_ref,
                  sw_ref, sbias_ref, wa_ref, wb_ref, wo_ref, rw_ref, dmf_ref, dmb_ref,
                  sf0_ref, bst_ref,
                  x1_ref, h2_ref, aff_ref,
                  o_ref, sg_ref, stf_ref, stb_ref, pm_ref, dio_ref):
    b = pl.program_id(0)
    i = pl.program_id(1)
    tb, d = x_ref.shape[1], x_ref.shape[2]
    nch = tb // CHUNK
    _, _, g_m, sh_f, sc_f, _ = _mod_rows(mod_ref, b, d)
    act = lambda rows, c0, width=WIDTH: act_ref[0, rows, c0:c0 + width]
    every = slice(None)

    @pl.when(i == 0)
    def _():
        stf_ref[...] = sf0_ref[0]

    stb_ref[...] = bst_ref[0, 0]

    o_ref[...] = jnp.zeros_like(o_ref)
    work = []
    for reverse in (False, True):
        work += [(reverse, ci) for ci in (range(nch - 1, -1, -1) if reverse else range(nch))]
    masks = {reverse: _level_masks(reverse) for reverse in (False, True)}
    for n, (reverse, ci) in enumerate(work):
        rows = pl.ds(ci * CHUNK, CHUNK)
        a_l, a_k, dm_ref = (A_LB, A_KB, dmb_ref) if reverse else (A_LF, A_KF, dmf_ref)
        _hgrn_scores(act(rows, A_Q), act(rows, a_k), act(rows, a_l), dm_ref, masks[reverse], reverse,
                     pm_ref.at[n], dio_ref.at[n])
    for n, (reverse, ci) in enumerate(work):
        rows = pl.ds(ci * CHUNK, CHUNK)
        a_k, st_ref = (A_KB, stb_ref) if reverse else (A_KF, stf_ref)
        _hgrn_apply(act(rows, A_Q), act(rows, a_k), act(rows, A_V), reverse, pm_ref.at[n], dio_ref.at[n],
                    st_ref, o_ref, ci * CHUNK)

    a_in = _group_rms(o_ref[...]) * hnw_ref[...] * act(every, A_G)
    y_a = _dot(a_in.astype(BF16), wa_ref[...])

    for ci in range(nch):
        rows = slice(ci * CHUNK, (ci + 1) * CHUNK)
        for g in range(HEADS):
            sl = slice(g * HD, (g + 1) * HD)
            sg_ref[rows, sl] = _dot(sw_ref[g], act(rows, A_SV + g * HD, HD).astype(BF16)) + sbias_ref[:, sl]
    y_b = _dot((act(every, A_U) * sg_ref[...]).astype(BF16), wb_ref[...])

    merged = act(every, A_GATES, d) * y_a + act(every, A_GATES + d, d) * y_b
    y = _dot(merged.astype(BF16), wo_ref[...])
    x1 = x_ref[0] + g_m * y
    x1_ref[0] = x1

    h2 = _rms(x1, nffn_ref[...]) * (1.0 + sc_f) + sh_f
    h2_hi, h2_lo = _split2(h2)
    h2_ref[0] = h2_hi
    rw_hi, rw_lo = _split2(rw_ref[...])
    logits = _dot_nt(rw_hi, h2_hi) + _dot_nt(rw_hi, h2_lo) + _dot_nt(rw_lo, h2_hi)
    mx = jnp.max(logits, axis=0, keepdims=True)
    ex = jnp.exp(logits - mx)
    aff_ref[0] = ex / jnp.sum(ex, axis=0, keepdims=True)


def _mixer(act, x, mod, nffn, hnw, sgu_w, sgu_bias, w_a, w_b, w_o, rw_t, s_f0, bstates, tb):
    b, t, d = x.shape
    nb = t // tb
    ne = rw_t.shape[0]
    dmf = jnp.asarray(np.tile(_decay_matrices(False), (1, 2)), BF16)
    dmb = jnp.asarray(np.tile(_decay_matrices(True), (1, 2)), BF16)
    const = lambda a: pl.BlockSpec(a.shape, lambda bi, i: (0,) * a.ndim, pipeline_mode=pl.Buffered(1))
    in_specs = [pl.BlockSpec((1, tb, act.shape[2]), lambda bi, i: (bi, i, 0)),
                pl.BlockSpec((1, tb, d), lambda bi, i: (bi, i, 0)),
                const(mod), const(nffn), const(hnw),
                const(sgu_w), const(sgu_bias), const(w_a), const(w_b), const(w_o), const(rw_t),
                const(dmf), const(dmb),
                pl.BlockSpec((1, N_PAIRS, PAIR, PAIR), lambda bi, i: (bi, 0, 0, 0)),
                pl.BlockSpec((1, 1, N_PAIRS, PAIR, PAIR), lambda bi, i: (bi, i, 0, 0, 0))]
    out_shape = (jax.ShapeDtypeStruct((b, t, d), F32),
                 jax.ShapeDtypeStruct((b, t, d), BF16),
                 jax.ShapeDtypeStruct((b, ne, t), F32))
    out_specs = (pl.BlockSpec((1, tb, d), lambda bi, i: (bi, i, 0)),
                 pl.BlockSpec((1, tb, d), lambda bi, i: (bi, i, 0)),
                 pl.BlockSpec((1, ne, tb), lambda bi, i: (bi, 0, i)))
    return pl.pallas_call(
        _mixer_kernel,
        out_shape=out_shape,
        grid=(b, nb),
        in_specs=in_specs,
        out_specs=out_specs,
        scratch_shapes=[pltpu.VMEM((tb, WIDTH), F32),
                        pltpu.VMEM((tb, WIDTH), F32),
                        pltpu.VMEM((N_PAIRS, PAIR, PAIR), F32),
                        pltpu.VMEM((N_PAIRS, PAIR, PAIR), F32),
                        pltpu.VMEM((2 * (tb // CHUNK), N_PAIRS, CHUNK, PAIR), BF16),
                        pltpu.VMEM((2 * (tb // CHUNK), 2 * CHUNK, WIDTH), F32)],
        compiler_params=pltpu.CompilerParams(dimension_semantics=("arbitrary", "arbitrary"),
                                             vmem_limit_bytes=VMEM_LIMIT),
        name="mixer",
    )(act, x, mod, nffn, hnw, sgu_w, sgu_bias, w_a, w_b, w_o, rw_t, dmf, dmb, s_f0, bstates)


def _front(x, c, ctx, c_ctx, ada_w, ada_b, norm_mix_w, norm_ffn_w, w_in, hgrn_lb_logits,
           hgrn_norm_w, sgu_norm_w, sgu_w, sgu_b, w_branch_a, w_branch_b, w_out, router_w, tb):
    b, t, d = x.shape
    layer = 0
    cond = jnp.zeros((8, d), F32).at[0:b].set(c).at[b].set(c_ctx)
    mod = _adaln(cond, ada_w[layer], ada_b[layer])
    lbl = hgrn_lb_logits[:, layer:layer + 2, :].reshape(4, WIDTH)
    nmix = norm_mix_w[layer].reshape(1, d)
    nffn = norm_ffn_w[layer].reshape(1, d)
    w_in_b = w_in[layer].astype(BF16)
    s_f0, s_b0 = _ctx_states(ctx, mod, nmix, w_in_b[:, WIDTH:4 * WIDTH], lbl, b)
    act = _in_proj(x, mod, nmix, w_in_b, lbl, sgu_norm_w[layer].reshape(1, WIDTH), min(t, 512))
    bstates = _bwd_states(act, s_b0, tb)
    sgu_bias = jnp.repeat(sgu_b[layer].T, HD, axis=1)
    return _mixer(act, x, mod, nffn, hgrn_norm_w[layer].reshape(1, WIDTH),
                  sgu_w[layer].astype(BF16), sgu_bias,
                  w_branch_a[layer].astype(BF16), w_branch_b[layer].astype(BF16),
                  w_out[layer].astype(BF16), router_w[layer].T, s_f0, bstates, tb), mod


def _route_kernel(cap, aff_ref, pos_ref, gate_ref, starts_ref):
    a = aff_ref[0]
    ne, t = a.shape
    nblk = t // LANES
    bits = pltpu.bitcast(a, I32)

    def search(it, lo):
        cand = lo | (jnp.int32(1) << (30 - it))
        cnt = jnp.sum(jnp.where(bits >= cand, 1.0, 0.0), axis=-1, keepdims=True)
        return jnp.where(cnt >= cap, cand, lo)

    thr = lax.fori_loop(0, 31, search, jnp.zeros((ne, 1), I32))
    gt = bits > thr
    eq = bits == thr
    n_ties_wanted = cap - jnp.sum(jnp.where(gt, 1.0, 0.0), axis=-1, keepdims=True)

    row = lax.broadcasted_iota(I32, (LANES, LANES), 0)
    col = lax.broadcasted_iota(I32, (LANES, LANES), 1)
    upper = jnp.where(row <= col, 1.0, 0.0).astype(BF16)
    lane = lax.broadcasted_iota(I32, (ne, LANES), 1)

    off = jnp.zeros((ne, 1), F32)
    sel_blocks = []
    for j in range(nblk):
        sl = slice(j * LANES, (j + 1) * LANES)
        eqf = jnp.where(eq[:, sl], 1.0, 0.0)
        incl = _dot(eqf.astype(BF16), upper) + off
        keep_tie = jnp.where(incl - eqf < n_ties_wanted, eqf, 0.0)
        sel_blocks.append(jnp.where(gt[:, sl], 1.0, keep_tie))
        off = incl[:, LANES - 1:LANES]

    off = jnp.zeros((ne, 1), F32)
    starts = jnp.zeros((ne, LANES), F32)
    for j in range(nblk):
        sl = slice(j * LANES, (j + 1) * LANES)
        self = sel_blocks[j]
        starts = jnp.where(lane == j, off, starts)
        incl = _dot(self.astype(BF16), upper) + off
        pos_ref[0, :, sl] = jnp.where(self > 0.0, incl - 1.0, -1.0).astype(I32)
        gate_ref[0, :, sl] = jnp.where(self > 0.0, a[:, sl], 0.0)
        off = incl[:, LANES - 1:LANES]
    starts = jnp.where(lane >= nblk, off, starts)
    starts_ref[0] = starts.astype(I32)


def _route(aff_t, cap):
    b, ne, t = aff_t.shape
    spec = pl.BlockSpec((1, ne, t), lambda i: (i, 0, 0))
    return pl.pallas_call(
        functools.partial(_route_kernel, cap),
        out_shape=(jax.ShapeDtypeStruct((b, ne, t), I32),
                   jax.ShapeDtypeStruct((b, ne, t), F32),
                   jax.ShapeDtypeStruct((b, ne, LANES), I32)),
        grid=(b,),
        in_specs=[spec],
        out_specs=(spec, spec, pl.BlockSpec((1, ne, LANES), lambda i: (i, 0, 0))),
        compiler_params=pltpu.CompilerParams(dimension_semantics=("arbitrary",),
                                             vmem_limit_bytes=VMEM_LIMIT),
        name="route",
    )(aff_t)


TOK_BLOCK = 2 * LANES


SLOT_ALIGN = 16
AUX = LANES


def _slot_windows(starts_ref, b, e0, n_exp, ne, j, win):
    per = TOK_BLOCK // LANES
    s_lo, s_hi, a0 = [], [], []
    n_pass = jnp.int32(0)
    for le in range(n_exp):
        base = (b * ne + e0 + le) * LANES
        lo = starts_ref[base + per * j]
        hi = starts_ref[base + per * (j + 1)]
        a = (lo // SLOT_ALIGN) * SLOT_ALIGN
        n_pass = jnp.maximum(n_pass, jnp.where(hi > lo, (hi - a + win - 1) // win, 0))
        s_lo.append(lo)
        s_hi.append(hi)
        a0.append(a)
    return s_lo, s_hi, a0, n_pass


def _window(s_lo, s_hi, a0, p, win, cap):
    a = a0 + p * win
    a_c = pl.multiple_of(jnp.minimum(a, cap - win), SLOT_ALIGN)
    return a_c, jnp.maximum(s_lo, a), jnp.minimum(s_hi, a + win)


def _gather_kernel(win, starts_ref, pos_ref, gate_ref, h2_ref, xs_ref):
    b = pl.program_id(0)
    g = pl.program_id(1)
    j = pl.program_id(2)
    eg, cap = xs_ref.shape[1], xs_ref.shape[2]
    ne = gate_ref.shape[2]
    d = h2_ref.shape[2]

    @pl.when(j == 0)
    def _():
        xs_ref[...] = jnp.zeros_like(xs_ref)

    pr = lax.broadcasted_iota(I32, (ne, AUX), 0)
    pc = lax.broadcasted_iota(I32, (ne, AUX), 1)
    pieces = _split3(gate_ref[0])
    gcols = sum(_dot(piece, jnp.where(pc == k * ne + pr, 1.0, 0.0).astype(BF16))
                for k, piece in enumerate(pieces)).astype(BF16)

    s_lo, s_hi, a0, n_pass = _slot_windows(starts_ref, b, g * eg, eg, ne, j, win)
    row = lax.broadcasted_iota(I32, (win, TOK_BLOCK), 0)
    row_d = lax.broadcasted_iota(I32, (win, d), 0)
    row_aux = lax.broadcasted_iota(I32, (win, AUX), 0)

    def one_pass(p, carry):
        wins = [_window(s_lo[le], s_hi[le], a0[le], p, win, cap) for le in range(eg)]
        onehots = []
        for le, (a_c, lo_i, hi_i) in enumerate(wins):
            prow = pos_ref[0, le, pl.ds(j, 1), :]
            owned = jnp.where(prow >= lo_i, jnp.where(prow < hi_i, prow, -1), -1)
            onehots.append(jnp.where(owned == row + a_c, 1.0, 0.0).astype(BF16))
        lhs = jnp.concatenate(onehots, axis=0)
        res = _dot(lhs, h2_ref[0])
        res_aux = _dot(lhs, gcols)
        for le, (a_c, lo_i, hi_i) in enumerate(wins):
            rows = pl.ds(a_c, win)
            mine = slice(le * win, (le + 1) * win)
            xs_ref[0, le, rows, 0:d] = jnp.where(row_d + a_c < lo_i, xs_ref[0, le, rows, 0:d],
                                                 res[mine].astype(BF16))
            xs_ref[0, le, rows, d:d + AUX] = jnp.where(row_aux + a_c < lo_i, xs_ref[0, le, rows, d:d + AUX],
                                                       res_aux[mine].astype(BF16))
        return carry

    lax.fori_loop(0, n_pass, one_pass, 0)


def _gather(starts, pos, gate_t, h2, cap, win, eg):
    b, t, d = h2.shape
    ne = pos.shape[1]
    ntb = t // TOK_BLOCK
    pos4 = pos.reshape(b, ne, ntb, TOK_BLOCK)
    grid_spec = pltpu.PrefetchScalarGridSpec(
        num_scalar_prefetch=1,
        grid=(b, ne // eg, ntb),
        in_specs=[pl.BlockSpec((1, eg, ntb, TOK_BLOCK), lambda bi, g, j, s: (bi, g, 0, 0)),
                  pl.BlockSpec((1, TOK_BLOCK, ne), lambda bi, g, j, s: (bi, j, 0)),
                  pl.BlockSpec((1, TOK_BLOCK, d), lambda bi, g, j, s: (bi, j, 0))],
        out_specs=pl.BlockSpec((1, eg, cap, d + AUX), lambda bi, g, j, s: (bi, g, 0, 0)))
    return pl.pallas_call(
        functools.partial(_gather_kernel, win),
        out_shape=jax.ShapeDtypeStruct((b, ne, cap, d + AUX), BF16),
        grid_spec=grid_spec,
        compiler_params=pltpu.CompilerParams(
            dimension_semantics=("arbitrary", "arbitrary", "arbitrary"), vmem_limit_bytes=VMEM_LIMIT),
        name="gather",
    )(starts.reshape(-1), pos4, gate_t, h2)


FF_SUB = 512


def _ffn_kernel(xs_ref, wg_ref, wu_ref, wd_ref, gf_ref, ye_ref, acc_ref):
    b = pl.program_id(0)
    e = pl.program_id(1)
    f = pl.program_id(2)
    ne = pl.num_programs(1)
    d = ye_ref.shape[-1]
    xs = xs_ref[0, 0, :, 0:d]
    tf = wg_ref.shape[2]

    @pl.when(f == 0)
    def _():
        acc_ref[...] = jnp.zeros_like(acc_ref)

    acc = acc_ref[...]
    for c0 in range(0, tf, FF_SUB):
        a = _dot(xs, wg_ref[0, :, c0:c0 + FF_SUB].astype(BF16))
        u = _dot(xs, wu_ref[0, :, c0:c0 + FF_SUB].astype(BF16))
        acc = acc + _dot((_silu(a) * u).astype(BF16), wd_ref[0, c0:c0 + FF_SUB, :].astype(BF16))
    acc_ref[...] = acc
    aux = xs_ref[0, 0, :, d:d + AUX].astype(F32)
    lane = lax.broadcasted_iota(I32, aux.shape, 1)
    mine = jnp.where(lane < 3 * ne, jnp.where(lane % ne == e, aux, 0.0), 0.0)
    gate = jnp.sum(mine, axis=-1, keepdims=True)
    ye_ref[0, 0] = (acc * gate * gf_ref[pl.ds(b, 1), :]).astype(BF16)


def _ffn(xs, w_gate, w_up, w_down, mod, tf):
    b, ne, cap, daux = xs.shape
    d = daux - AUX
    ff = w_gate.shape[2]
    return pl.pallas_call(
        _ffn_kernel,
        out_shape=jax.ShapeDtypeStruct((b, ne, cap, d), BF16),
        grid=(b, ne, ff // tf),
        in_specs=[pl.BlockSpec((1, 1, cap, daux), lambda bi, e, f: (bi, e, 0, 0)),
                  pl.BlockSpec((1, d, tf), lambda bi, e, f: (e, 0, f)),
                  pl.BlockSpec((1, d, tf), lambda bi, e, f: (e, 0, f)),
                  pl.BlockSpec((1, tf, d), lambda bi, e, f: (e, f, 0)),
                  pl.BlockSpec((mod.shape[0], d), lambda bi, e, f: (0, N_MOD - 1))],
        out_specs=pl.BlockSpec((1, 1, cap, d), lambda bi, e, f: (bi, e, 0, 0)),
        scratch_shapes=[pltpu.VMEM((cap, d), F32)],
        compiler_params=pltpu.CompilerParams(dimension_semantics=("arbitrary", "arbitrary", "arbitrary"),
                                             vmem_limit_bytes=VMEM_LIMIT),
        name="ffn",
    )(xs, w_gate, w_up, w_down, mod)


def _combine_kernel(win, starts_ref, post_ref, ye_ref, x1_ref, fw_ref, out_ref, acc_ref, stack_ref):
    b = pl.program_id(0)
    j = pl.program_id(1)
    ne, cap = ye_ref.shape[1], ye_ref.shape[2]
    acc_ref[...] = x1_ref[0]
    s_lo, s_hi, a0, n_pass = _slot_windows(starts_ref, b, 0, ne, ne, j, win)
    row = lax.broadcasted_iota(I32, (win, TOK_BLOCK), 0)

    def one_pass(p, carry):
        onehots = []
        for e in range(ne):
            a_c, lo_i, hi_i = _window(s_lo[e], s_hi[e], a0[e], p, win, cap)
            prow = post_ref[0, e, pl.ds(j, 1), :]
            owned = jnp.where(prow >= lo_i, jnp.where(prow < hi_i, prow, -1), -1)
            onehots.append(jnp.where(owned == row + a_c, 1.0, 0.0).astype(BF16))
            stack_ref[e * win:(e + 1) * win, :] = ye_ref[0, e, pl.ds(a_c, win), :]
        acc_ref[...] += _dot_tn(jnp.concatenate(onehots, axis=0), stack_ref[...])
        return carry

    lax.fori_loop(0, n_pass, one_pass, 0)
    out_ref[0] = _rms(acc_ref[...], fw_ref[...])


def _combine(starts, pos, ye, x1, final_w, win):
    b, t, d = x1.shape
    ne, cap = ye.shape[1], ye.shape[2]
    ntb = t // TOK_BLOCK
    pos4 = pos.reshape(b, ne, ntb, TOK_BLOCK)
    grid_spec = pltpu.PrefetchScalarGridSpec(
        num_scalar_prefetch=1,
        grid=(b, ntb),
        in_specs=[pl.BlockSpec((1, ne, ntb, TOK_BLOCK), lambda bi, j, s: (bi, 0, 0, 0)),
                  pl.BlockSpec((1, ne, cap, d), lambda bi, j, s: (bi, 0, 0, 0),
                               pipeline_mode=pl.Buffered(1)),
                  pl.BlockSpec((1, TOK_BLOCK, d), lambda bi, j, s: (bi, j, 0)),
                  pl.BlockSpec((1, d), lambda bi, j, s: (0, 0))],
        out_specs=pl.BlockSpec((1, TOK_BLOCK, d), lambda bi, j, s: (bi, j, 0)),
        scratch_shapes=[pltpu.VMEM((TOK_BLOCK, d), F32), pltpu.VMEM((ne * win, d), BF16)])
    return pl.pallas_call(
        functools.partial(_combine_kernel, win),
        out_shape=jax.ShapeDtypeStruct((b, t, d), F32),
        grid_spec=grid_spec,
        compiler_params=pltpu.CompilerParams(
            dimension_semantics=("arbitrary", "arbitrary"), vmem_limit_bytes=VMEM_LIMIT),
        name="combine",
    )(starts.reshape(-1), pos4, ye, x1, final_w.reshape(1, d))


def kernel(x, c, ctx, c_ctx, ada_w, ada_b, norm_mix_w, norm_ffn_w, w_in, hgrn_lb_logits, hgrn_norm_w,
           sgu_norm_w, sgu_w, sgu_b, w_branch_a, w_branch_b, w_out, router_w, expert_w_gate,
           expert_w_up, expert_w_down, final_norm_w):
    b, t, d = x.shape
    assert b + 1 <= 8 and t % TOK_BLOCK == 0 and d % LANES == 0
    ne = router_w.shape[-1]
    cap = CAPACITY_FACTOR * t // ne
    win = min(cap, 64)
    assert cap % SLOT_ALIGN == 0 and t // LANES < LANES and 3 * ne <= AUX
    (x1, h2, aff_t), mod = _front(x, c, ctx, c_ctx, ada_w, ada_b, norm_mix_w, norm_ffn_w, w_in,
                                  hgrn_lb_logits, hgrn_norm_w, sgu_norm_w, sgu_w, sgu_b,
                                  w_branch_a, w_branch_b, w_out, router_w, 256)
    pos, gate, starts = _route(aff_t, cap)
    xs = _gather(starts, pos, jnp.swapaxes(gate, 1, 2), h2, cap, win, 8)
    ye = _ffn(xs, expert_w_gate[0], expert_w_up[0], expert_w_down[0], mod, 1024)
    return _combine(starts, pos, ye, x1, final_norm_w, win)
```

```python
import functools

import numpy as np
import jax
import jax.numpy as jnp
from jax import lax
from jax.experimental import pallas as pl
from jax.experimental.pallas import tpu as pltpu

F32 = jnp.float32
BF16 = jnp.bfloat16
I32 = jnp.int32

LANES = 128
EPS = 1e-6
N_MOD = 6
HEADS = 4
HD = 128
WIDTH = HEADS * HD
CHUNK = 128
N_EXPERTS = 16
CAPACITY_FACTOR = 2
LEVELS = tuple(CHUNK >> (i + 1) for i in range(7))
N_DMAT = 2 + len(LEVELS)
PAIR = 2 * HD
N_PAIRS = HEADS // 2
VMEM_LIMIT = 52 * 1024 * 1024


def _dot(a, b):
    return jnp.dot(a, b, preferred_element_type=F32)


def _dot_nt(a, b):
    return lax.dot_general(a, b, (((1,), (1,)), ((), ())), preferred_element_type=F32)


def _dot_tn(a, b):
    return lax.dot_general(a, b, (((0,), (0,)), ((), ())), preferred_element_type=F32)


def _split2(x):
    hi = x.astype(BF16)
    lo = (x - hi.astype(F32)).astype(BF16)
    return hi, lo


def _split3(x):
    hi = x.astype(BF16)
    r = x - hi.astype(F32)
    mid = r.astype(BF16)
    lo = (r - mid.astype(F32)).astype(BF16)
    return hi, mid, lo


def _sigmoid(x):
    return 1.0 / (1.0 + jnp.exp(-x))


def _silu(x):
    return x * _sigmoid(x)


def _gelu_tanh(x):
    c = np.sqrt(2.0 / np.pi).astype(np.float32)
    return 0.5 * x * (1.0 + jnp.tanh(c * (x + 0.044715 * (x * x * x))))


def _rms(x, w):
    return x * lax.rsqrt(jnp.mean(x * x, axis=-1, keepdims=True) + EPS) * w


def _lower_bound(lbl, d):
    l0 = lbl[2 * d:2 * d + 1, :]
    l1 = lbl[2 * d + 1:2 * d + 2, :]
    m = jnp.maximum(l0, l1)
    e0 = jnp.exp(l0 - m)
    e1 = jnp.exp(l1 - m)
    return e0 / (e0 + e1)


def _forget(z, lb):
    f = lb + (1.0 - lb) * _sigmoid(z)
    return jnp.log(f), 1.0 - f


def _cumsum_dot(m_bf16, x):
    hi, mid, lo = _split3(x)
    return _dot(m_bf16, hi) + _dot(m_bf16, mid) + _dot(m_bf16, lo)


def _same_head(shape):
    r = lax.broadcasted_iota(I32, shape, 0) < HD
    c = lax.broadcasted_iota(I32, shape, 1) < HD
    return r == c


def _block_diag(x):
    first = lax.broadcasted_iota(I32, x.shape, 1) < HD
    zero = jnp.zeros_like(x)
    return jnp.concatenate([jnp.where(first, x, zero), jnp.where(first, zero, x)], axis=0)


def _pair_outer(v_pair, w_pair):
    full = _dot_tn(v_pair, w_pair)
    return jnp.where(_same_head(full.shape), full, 0.0)


def _decay_matrices(reverse):
    c = CHUNK
    i = np.arange(c)[:, None]
    m = np.arange(c)[None, :]
    mats = []
    if not reverse:
        mats.append(m <= i)
        mats.append(m > i)
    else:
        mats.append(m >= i)
        mats.append(m < i)
    for half in LEVELS:
        a = (i // (2 * half)) * (2 * half)
        mid = a + half
        if not reverse:
            qside = i >= mid
            mat = np.where(qside, (m >= mid) & (m <= i), (m > i) & (m < mid))
        else:
            qside = i < mid
            mat = np.where(qside, (m >= i) & (m < mid), (m >= mid) & (m < i))
        mats.append(mat)
    return np.concatenate(mats, axis=0).astype(np.float32)


def _block_decay_matrix(n, reverse):
    i = np.arange(n)[:, None]
    m = np.arange(n)[None, :]
    return ((m < i) if reverse else (m > i)).astype(np.float32)


def _adaln_kernel(cond_ref, w_ref, b_ref, out_ref):
    s = _silu(cond_ref[...])
    s_hi, s_lo = _split2(s)
    w_hi, w_lo = _split2(w_ref[...])
    out_ref[...] = _dot(s_hi, w_hi) + _dot(s_hi, w_lo) + _dot(s_lo, w_hi) + b_ref[...]


def _adaln(cond, ada_w, ada_b):
    rows, d = cond.shape
    n = ada_w.shape[1]
    tn = 1024
    return pl.pallas_call(
        _adaln_kernel,
        out_shape=jax.ShapeDtypeStruct((rows, n), F32),
        grid=(n // tn,),
        in_specs=[pl.BlockSpec((rows, d), lambda j: (0, 0)),
                  pl.BlockSpec((d, tn), lambda j: (0, j)),
                  pl.BlockSpec((1, tn), lambda j: (0, j))],
        out_specs=pl.BlockSpec((rows, tn), lambda j: (0, j)),
        compiler_params=pltpu.CompilerParams(dimension_semantics=("arbitrary",),
                                             vmem_limit_bytes=VMEM_LIMIT),
        name="adaln",
    )(cond, ada_w, ada_b.reshape(1, n))


def _mod_rows(mod_ref, row, d):
    return [mod_ref[pl.ds(row, 1), j * d:(j + 1) * d] for j in range(N_MOD)]


def _ctx_kernel(ctx_row, x_ref, mod_ref, nw_ref, w_ref, lbl_ref, mf_ref, mb_ref, sf_ref, sb_ref):
    d = x_ref.shape[-1]
    x = x_ref[0]
    sh, sc = mod_ref[pl.ds(ctx_row, 1), 0:d], mod_ref[pl.ds(ctx_row, 1), d:2 * d]
    h = _rms(x, nw_ref[...]) * (1.0 + sc) + sh
    p = _dot(h.astype(BF16), w_ref[...])
    v = p[:, 2 * WIDTH:3 * WIDTH].astype(BF16)
    lbl = lbl_ref[...]
    for dirn, (m_ref, out_ref) in enumerate(((mf_ref, sf_ref), (mb_ref, sb_ref))):
        logf, k = _forget(p[:, dirn * WIDTH:(dirn + 1) * WIDTH], _lower_bound(lbl, dirn))
        w = (k * jnp.exp(_cumsum_dot(m_ref[...], logf))).astype(BF16)
        for pr in range(N_PAIRS):
            sl = slice(pr * PAIR, (pr + 1) * PAIR)
            out_ref[0, pr] = _pair_outer(v[:, sl], w[:, sl])


def _ctx_states(ctx, mod, norm_w, w_c, lbl, ctx_row):
    b, l, d = ctx.shape
    mf = jnp.asarray(_block_decay_matrix(l, False), BF16)
    mb = jnp.asarray(_block_decay_matrix(l, True), BF16)
    full = lambda a: pl.BlockSpec(a.shape, lambda i: (0,) * a.ndim)
    st = jax.ShapeDtypeStruct((b, N_PAIRS, PAIR, PAIR), F32)
    st_spec = pl.BlockSpec((1, N_PAIRS, PAIR, PAIR), lambda i: (i, 0, 0, 0))
    return pl.pallas_call(
        functools.partial(_ctx_kernel, ctx_row),
        out_shape=(st, st),
        grid=(b,),
        in_specs=[pl.BlockSpec((1, l, d), lambda i: (i, 0, 0)), full(mod), full(norm_w),
                  full(w_c), full(lbl), full(mf), full(mb)],
        out_specs=(st_spec, st_spec),
        compiler_params=pltpu.CompilerParams(dimension_semantics=("arbitrary",),
                                             vmem_limit_bytes=VMEM_LIMIT),
        name="ctx_state",
    )(ctx, mod, norm_w, w_c, lbl, mf, mb)


A_Q, A_LF, A_KF, A_LB, A_KB, A_V, A_G, A_U, A_SV = (j * WIDTH for j in range(9))
A_GATES = 9 * WIDTH


def _in_proj_kernel(sub, x_ref, mod_ref, nw_ref, w_ref, lbl_ref, snw_ref, m_ref, s0_ref,
                    act_ref, bst_ref, st_ref):
    b = pl.program_id(0)
    d = x_ref.shape[-1]

    @pl.when(pl.program_id(1) == 0)
    def _():
        st_ref[...] = s0_ref[0]

    sh, sc = mod_ref[pl.ds(b, 1), 0:d], mod_ref[pl.ds(b, 1), d:2 * d]
    h = (_rms(x_ref[0], nw_ref[...]) * (1.0 + sc) + sh).astype(BF16)
    lbl = lbl_ref[...]

    def proj(j):
        return _dot(h, w_ref[:, j * WIDTH:(j + 1) * WIDTH])

    act_ref[0, :, A_Q:A_Q + WIDTH] = _silu(proj(0))
    for j, a_l, a_k, dirn in ((1, A_LF, A_KF, 0), (2, A_LB, A_KB, 1)):
        logf, k = _forget(proj(j), _lower_bound(lbl, dirn))
        act_ref[0, :, a_l:a_l + WIDTH] = logf
        act_ref[0, :, a_k:a_k + WIDTH] = k
    act_ref[0, :, A_V:A_V + WIDTH] = proj(3)
    act_ref[0, :, A_G:A_G + WIDTH] = _silu(proj(4))
    act_ref[0, :, A_U:A_U + WIDTH] = _gelu_tanh(proj(5))
    act_ref[0, :, A_SV:A_SV + WIDTH] = _group_rms(_gelu_tanh(proj(6))) * snw_ref[...]
    for j in range(2 * d // WIDTH):
        act_ref[0, :, A_GATES + j * WIDTH:A_GATES + (j + 1) * WIDTH] = _sigmoid(proj(7 + j))

    for piece in range(x_ref.shape[1] // sub - 1, -1, -1):
        rows = slice(piece * sub, (piece + 1) * sub)
        bst_ref[0, piece] = st_ref[...]
        logf = act_ref[0, rows, A_LB:A_LB + WIDTH]
        w = (act_ref[0, rows, A_KB:A_KB + WIDTH] * jnp.exp(_cumsum_dot(m_ref[...], logf))).astype(BF16)
        v = act_ref[0, rows, A_V:A_V + WIDTH].astype(BF16)
        tot = jnp.exp(jnp.sum(logf, axis=0, keepdims=True))
        for pr in range(N_PAIRS):
            sl = slice(pr * PAIR, (pr + 1) * PAIR)
            st_ref[pr] = st_ref[pr] * tot[:, sl] + _pair_outer(v[:, sl], w[:, sl])


def _in_proj(x, mod, norm_w, w_in, lbl, snw, s_b0, tm, sub):
    b, t, d = x.shape
    nb = t // tm
    per = tm // sub
    m = jnp.asarray(_block_decay_matrix(sub, True), BF16)
    full = lambda a: pl.BlockSpec(a.shape, lambda bi, i: (0,) * a.ndim, pipeline_mode=pl.Buffered(1))
    n_act = A_GATES + 2 * d
    return pl.pallas_call(
        functools.partial(_in_proj_kernel, sub),
        out_shape=(jax.ShapeDtypeStruct((b, t, n_act), F32),
                   jax.ShapeDtypeStruct((b, t // sub, N_PAIRS, PAIR, PAIR), F32)),
        grid=(b, nb),
        in_specs=[pl.BlockSpec((1, tm, d), lambda bi, i: (bi, nb - 1 - i, 0)), full(mod), full(norm_w),
                  full(w_in), full(lbl), full(snw), full(m),
                  pl.BlockSpec((1, N_PAIRS, PAIR, PAIR), lambda bi, i: (bi, 0, 0, 0))],
        out_specs=(pl.BlockSpec((1, tm, n_act), lambda bi, i: (bi, nb - 1 - i, 0)),
                   pl.BlockSpec((1, per, N_PAIRS, PAIR, PAIR), lambda bi, i: (bi, nb - 1 - i, 0, 0, 0))),
        scratch_shapes=[pltpu.VMEM((N_PAIRS, PAIR, PAIR), F32)],
        compiler_params=pltpu.CompilerParams(dimension_semantics=("arbitrary", "arbitrary"),
                                             vmem_limit_bytes=VMEM_LIMIT),
        name="in_proj",
    )(x, mod, norm_w, w_in, lbl, snw, m, s_b0)


def _level_masks(reverse):
    row = lax.broadcasted_iota(I32, (CHUNK, PAIR), 0)
    col = lax.broadcasted_iota(I32, (CHUNK, PAIR), 1) & (CHUNK - 1)
    x = row ^ col
    out = []
    for half in LEVELS:
        in_pair = jnp.where(x >= half, jnp.where(x < 2 * half, 1.0, 0.0), 0.0)
        bit = (col if reverse else row) & half
        out.append((jnp.where(bit != 0, in_pair, 0.0), (row & half) != 0))
    return out


def _hgrn_scores(q, k, logf, dm_ref, masks, reverse, pm_ref, dio_ref):
    hi, lo = _split2(logf)
    dall = _dot(dm_ref[...], jnp.concatenate([hi, lo], axis=0))
    dio_ref[...] = dall[0:2 * CHUNK]
    for pr in range(N_PAIRS):
        sl = slice(pr * PAIR, (pr + 1) * PAIR)
        qp, kp = q[:, sl], k[:, sl]
        pm = jnp.zeros((CHUNK, PAIR), F32)
        for li in range(len(LEVELS)):
            mask, row_bit = masks[li]
            qside = jnp.logical_not(row_bit) if reverse else row_bit
            e = jnp.exp(dall[(2 + li) * CHUNK:(3 + li) * CHUNK, sl])
            xm = (jnp.where(qside, qp, kp) * e).astype(BF16)
            pm = pm + _dot_nt(xm, _block_diag(xm)) * mask
        pm_ref[pr] = pm.astype(BF16)


def _hgrn_apply(q, k, v, reverse, pm_ref, dio_ref, state_ref, o_ref, r0):
    e_in = jnp.exp(dio_ref[0:CHUNK])
    e_st = jnp.exp(dio_ref[CHUNK:2 * CHUNK])
    tot_row = 0 if reverse else CHUNK - 1
    e_tot = e_in[tot_row:tot_row + 1]
    first = lax.broadcasted_iota(I32, (CHUNK, PAIR), 1) < HD
    for pr in range(N_PAIRS):
        sl = slice(pr * PAIR, (pr + 1) * PAIR)
        qp, kp, vp = q[:, sl], k[:, sl], v[:, sl]
        qk = qp * kp
        diag = jnp.where(first, jnp.sum(qk[:, 0:HD], axis=-1, keepdims=True),
                         jnp.sum(qk[:, HD:PAIR], axis=-1, keepdims=True))
        st = state_ref[pr]
        o = (_dot(pm_ref[pr], _block_diag(vp.astype(BF16))) + diag * vp
             + _dot_nt((qp * e_in[:, sl]).astype(BF16), st.astype(BF16)))
        o_ref[pl.ds(r0, CHUNK), sl] += o
        state_ref[pr] = st * e_tot[:, sl] + _pair_outer(vp.astype(BF16), (kp * e_st[:, sl]).astype(BF16))


def _group_rms(x):
    return jnp.concatenate(
        [x[:, g * HD:(g + 1) * HD]
         * lax.rsqrt(jnp.mean(x[:, g * HD:(g + 1) * HD] ** 2, axis=-1, keepdims=True) + EPS)
         for g in range(HEADS)], axis=-1)


def _mixer_kernel(act_ref, x_ref, mod_ref, nffn_ref, hnw_ref,
                  sw_ref, sbias_ref, wa_ref, wb_ref, wo_ref, rw_ref, dmf_ref, dmb_ref,
                  sf0_ref, bst_ref,
                  x1_ref, h2_ref, aff_ref,
                  o_ref, sg_ref, stf_ref, stb_ref, pm_ref, dio_ref):
    b = pl.program_id(0)
    i = pl.program_id(1)
    tb, d = x_ref.shape[1], x_ref.shape[2]
    nch = tb // CHUNK
    _, _, g_m, sh_f, sc_f, _ = _mod_rows(mod_ref, b, d)
    act = lambda rows, c0, width=WIDTH: act_ref[0, rows, c0:c0 + width]
    every = slice(None)

    @pl.when(i == 0)
    def _():
        stf_ref[...] = sf0_ref[0]

    stb_ref[...] = bst_ref[0, 0]

    o_ref[...] = jnp.zeros_like(o_ref)
    work = []
    for reverse in (False, True):
        work += [(reverse, ci) for ci in (range(nch - 1, -1, -1) if reverse else range(nch))]
    masks = {reverse: _level_masks(reverse) for reverse in (False, True)}
    for n, (reverse, ci) in enumerate(work):
        rows = pl.ds(ci * CHUNK, CHUNK)
        a_l, a_k, dm_ref = (A_LB, A_KB, dmb_ref) if reverse else (A_LF, A_KF, dmf_ref)
        _hgrn_scores(act(rows, A_Q), act(rows, a_k), act(rows, a_l), dm_ref, masks[reverse], reverse,
                     pm_ref.at[n], dio_ref.at[n])
    for n, (reverse, ci) in enumerate(work):
        rows = pl.ds(ci * CHUNK, CHUNK)
        a_k, st_ref = (A_KB, stb_ref) if reverse else (A_KF, stf_ref)
        _hgrn_apply(act(rows, A_Q), act(rows, a_k), act(rows, A_V), reverse, pm_ref.at[n], dio_ref.at[n],
                    st_ref, o_ref, ci * CHUNK)

    a_in = _group_rms(o_ref[...]) * hnw_ref[...] * act(every, A_G)
    y_a = _dot(a_in.astype(BF16), wa_ref[...])

    for ci in range(nch):
        rows = slice(ci * CHUNK, (ci + 1) * CHUNK)
        for g in range(HEADS):
            sl = slice(g * HD, (g + 1) * HD)
            sg_ref[rows, sl] = _dot(sw_ref[g], act(rows, A_SV + g * HD, HD).astype(BF16)) + sbias_ref[:, sl]
    y_b = _dot((act(every, A_U) * sg_ref[...]).astype(BF16), wb_ref[...])

    merged = act(every, A_GATES, d) * y_a + act(every, A_GATES + d, d) * y_b
    y = _dot(merged.astype(BF16), wo_ref[...])
    x1 = x_ref[0] + g_m * y
    x1_ref[0] = x1

    h2 = _rms(x1, nffn_ref[...]) * (1.0 + sc_f) + sh_f
    h2_hi, h2_lo = _split2(h2)
    h2_ref[0] = h2_hi
    rw_hi, rw_lo = _split2(rw_ref[...])
    logits = _dot_nt(rw_hi, h2_hi) + _dot_nt(rw_hi, h2_lo) + _dot_nt(rw_lo, h2_hi)
    mx = jnp.max(logits, axis=0, keepdims=True)
    ex = jnp.exp(logits - mx)
    aff_ref[0] = ex / jnp.sum(ex, axis=0, keepdims=True)


def _mixer(act, x, mod, nffn, hnw, sgu_w, sgu_bias, w_a, w_b, w_o, rw_t, s_f0, bstates, tb):
    b, t, d = x.shape
    nb = t // tb
    ne = rw_t.shape[0]
    dmf = jnp.asarray(np.tile(_decay_matrices(False), (1, 2)), BF16)
    dmb = jnp.asarray(np.tile(_decay_matrices(True), (1, 2)), BF16)
    const = lambda a: pl.BlockSpec(a.shape, lambda bi, i: (0,) * a.ndim, pipeline_mode=pl.Buffered(1))
    in_specs = [pl.BlockSpec((1, tb, act.shape[2]), lambda bi, i: (bi, i, 0)),
                pl.BlockSpec((1, tb, d), lambda bi, i: (bi, i, 0)),
                const(mod), const(nffn), const(hnw),
                const(sgu_w), const(sgu_bias), const(w_a), const(w_b), const(w_o), const(rw_t),
                const(dmf), const(dmb),
                pl.BlockSpec((1, N_PAIRS, PAIR, PAIR), lambda bi, i: (bi, 0, 0, 0)),
                pl.BlockSpec((1, 1, N_PAIRS, PAIR, PAIR), lambda bi, i: (bi, i, 0, 0, 0))]
    out_shape = (jax.ShapeDtypeStruct((b, t, d), F32),
                 jax.ShapeDtypeStruct((b, t, d), BF16),
                 jax.ShapeDtypeStruct((b, ne, t), F32))
    out_specs = (pl.BlockSpec((1, tb, d), lambda bi, i: (bi, i, 0)),
                 pl.BlockSpec((1, tb, d), lambda bi, i: (bi, i, 0)),
                 pl.BlockSpec((1, ne, tb), lambda bi, i: (bi, 0, i)))
    return pl.pallas_call(
        _mixer_kernel,
        out_shape=out_shape,
        grid=(b, nb),
        in_specs=in_specs,
        out_specs=out_specs,
        scratch_shapes=[pltpu.VMEM((tb, WIDTH), F32),
                        pltpu.VMEM((tb, WIDTH), F32),
                        pltpu.VMEM((N_PAIRS, PAIR, PAIR), F32),
                        pltpu.VMEM((N_PAIRS, PAIR, PAIR), F32),
                        pltpu.VMEM((2 * (tb // CHUNK), N_PAIRS, CHUNK, PAIR), BF16),
                        pltpu.VMEM((2 * (tb // CHUNK), 2 * CHUNK, WIDTH), F32)],
        compiler_params=pltpu.CompilerParams(dimension_semantics=("arbitrary", "arbitrary"),
                                             vmem_limit_bytes=VMEM_LIMIT),
        name="mixer",
    )(act, x, mod, nffn, hnw, sgu_w, sgu_bias, w_a, w_b, w_o, rw_t, dmf, dmb, s_f0, bstates)


def _front(x, c, ctx, c_ctx, ada_w, ada_b, norm_mix_w, norm_ffn_w, w_in, hgrn_lb_logits,
           hgrn_norm_w, sgu_norm_w, sgu_w, sgu_b, w_branch_a, w_branch_b, w_out, router_w, tb):
    b, t, d = x.shape
    layer = 0
    cond = jnp.zeros((8, d), F32).at[0:b].set(c).at[b].set(c_ctx)
    mod = _adaln(cond, ada_w[layer], ada_b[layer])
    lbl = hgrn_lb_logits[:, layer:layer + 2, :].reshape(4, WIDTH)
    nmix = norm_mix_w[layer].reshape(1, d)
    nffn = norm_ffn_w[layer].reshape(1, d)
    w_in_b = w_in[layer].astype(BF16)
    s_f0, s_b0 = _ctx_states(ctx, mod, nmix, w_in_b[:, WIDTH:4 * WIDTH], lbl, b)
    act, bstates = _in_proj(x, mod, nmix, w_in_b, lbl, sgu_norm_w[layer].reshape(1, WIDTH), s_b0,
                            min(t, 512), tb)
    sgu_bias = jnp.repeat(sgu_b[layer].T, HD, axis=1)
    return _mixer(act, x, mod, nffn, hgrn_norm_w[layer].reshape(1, WIDTH),
                  sgu_w[layer].astype(BF16), sgu_bias,
                  w_branch_a[layer].astype(BF16), w_branch_b[layer].astype(BF16),
                  w_out[layer].astype(BF16), router_w[layer].T, s_f0, bstates, tb), mod


def _route_kernel(cap, aff_ref, pos_ref, gate_ref, starts_ref):
    a = aff_ref[0]
    ne, t = a.shape
    nblk = t // LANES
    bits = pltpu.bitcast(a, I32)

    def search(it, lo):
        cand = lo | (jnp.int32(1) << (30 - it))
        cnt = jnp.sum(jnp.where(bits >= cand, 1.0, 0.0), axis=-1, keepdims=True)
        return jnp.where(cnt >= cap, cand, lo)

    thr = lax.fori_loop(0, 31, search, jnp.zeros((ne, 1), I32))
    gt = bits > thr
    eq = bits == thr
    n_ties_wanted = cap - jnp.sum(jnp.where(gt, 1.0, 0.0), axis=-1, keepdims=True)

    row = lax.broadcasted_iota(I32, (LANES, LANES), 0)
    col = lax.broadcasted_iota(I32, (LANES, LANES), 1)
    upper = jnp.where(row <= col, 1.0, 0.0).astype(BF16)
    lane = lax.broadcasted_iota(I32, (ne, LANES), 1)

    off = jnp.zeros((ne, 1), F32)
    sel_blocks = []
    for j in range(nblk):
        sl = slice(j * LANES, (j + 1) * LANES)
        eqf = jnp.where(eq[:, sl], 1.0, 0.0)
        incl = _dot(eqf.astype(BF16), upper) + off
        keep_tie = jnp.where(incl - eqf < n_ties_wanted, eqf, 0.0)
        sel_blocks.append(jnp.where(gt[:, sl], 1.0, keep_tie))
        off = incl[:, LANES - 1:LANES]

    off = jnp.zeros((ne, 1), F32)
    starts = jnp.zeros((ne, LANES), F32)
    for j in range(nblk):
        sl = slice(j * LANES, (j + 1) * LANES)
        self = sel_blocks[j]
        starts = jnp.where(lane == j, off, starts)
        incl = _dot(self.astype(BF16), upper) + off
        pos_ref[0, :, sl] = jnp.where(self > 0.0, incl - 1.0, -1.0).astype(I32)
        gate_ref[0, :, sl] = jnp.where(self > 0.0, a[:, sl], 0.0)
        off = incl[:, LANES - 1:LANES]
    starts = jnp.where(lane >= nblk, off, starts)
    starts_ref[0] = starts.astype(I32)


def _route(aff_t, cap):
    b, ne, t = aff_t.shape
    spec = pl.BlockSpec((1, ne, t), lambda i: (i, 0, 0))
    return pl.pallas_call(
        functools.partial(_route_kernel, cap),
        out_shape=(jax.ShapeDtypeStruct((b, ne, t), I32),
                   jax.ShapeDtypeStruct((b, ne, t), F32),
                   jax.ShapeDtypeStruct((b, ne, LANES), I32)),
        grid=(b,),
        in_specs=[spec],
        out_specs=(spec, spec, pl.BlockSpec((1, ne, LANES), lambda i: (i, 0, 0))),
        compiler_params=pltpu.CompilerParams(dimension_semantics=("arbitrary",),
                                             vmem_limit_bytes=VMEM_LIMIT),
        name="route",
    )(aff_t)


TOK_BLOCK = 2 * LANES


SLOT_ALIGN = 16
AUX = LANES


def _slot_windows(starts_ref, b, e0, n_exp, ne, j, win):
    per = TOK_BLOCK // LANES
    s_lo, s_hi, a0 = [], [], []
    n_pass = jnp.int32(0)
    for le in range(n_exp):
        base = (b * ne + e0 + le) * LANES
        lo = starts_ref[base + per * j]
        hi = starts_ref[base + per * (j + 1)]
        a = (lo // SLOT_ALIGN) * SLOT_ALIGN
        n_pass = jnp.maximum(n_pass, jnp.where(hi > lo, (hi - a + win - 1) // win, 0))
        s_lo.append(lo)
        s_hi.append(hi)
        a0.append(a)
    return s_lo, s_hi, a0, n_pass


def _window(s_lo, s_hi, a0, p, win, cap):
    a = a0 + p * win
    a_c = pl.multiple_of(jnp.minimum(a, cap - win), SLOT_ALIGN)
    return a_c, jnp.maximum(s_lo, a), jnp.minimum(s_hi, a + win)


def _gather_kernel(win, starts_ref, pos_ref, gate_ref, h2_ref, xs_ref):
    b = pl.program_id(0)
    g = pl.program_id(1)
    j = pl.program_id(2)
    eg, cap = xs_ref.shape[1], xs_ref.shape[2]
    ne = gate_ref.shape[2]
    d = h2_ref.shape[2]

    @pl.when(j == 0)
    def _():
        xs_ref[...] = jnp.zeros_like(xs_ref)

    pr = lax.broadcasted_iota(I32, (ne, AUX), 0)
    pc = lax.broadcasted_iota(I32, (ne, AUX), 1)
    pieces = _split3(gate_ref[0])
    gcols = sum(_dot(piece, jnp.where(pc == k * ne + pr, 1.0, 0.0).astype(BF16))
                for k, piece in enumerate(pieces)).astype(BF16)

    s_lo, s_hi, a0, n_pass = _slot_windows(starts_ref, b, g * eg, eg, ne, j, win)
    row = lax.broadcasted_iota(I32, (win, TOK_BLOCK), 0)
    row_d = lax.broadcasted_iota(I32, (win, d), 0)
    row_aux = lax.broadcasted_iota(I32, (win, AUX), 0)

    def one_pass(p, carry):
        wins = [_window(s_lo[le], s_hi[le], a0[le], p, win, cap) for le in range(eg)]
        onehots = []
        for le, (a_c, lo_i, hi_i) in enumerate(wins):
            prow = pos_ref[0, le, pl.ds(j, 1), :]
            owned = jnp.where(prow >= lo_i, jnp.where(prow < hi_i, prow, -1), -1)
            onehots.append(jnp.where(owned == row + a_c, 1.0, 0.0).astype(BF16))
        lhs = jnp.concatenate(onehots, axis=0)
        res = _dot(lhs, h2_ref[0])
        res_aux = _dot(lhs, gcols)
        for le, (a_c, lo_i, hi_i) in enumerate(wins):
            rows = pl.ds(a_c, win)
            mine = slice(le * win, (le + 1) * win)
            xs_ref[0, le, rows, 0:d] = jnp.where(row_d + a_c < lo_i, xs_ref[0, le, rows, 0:d],
                                                 res[mine].astype(BF16))
            xs_ref[0, le, rows, d:d + AUX] = jnp.where(row_aux + a_c < lo_i, xs_ref[0, le, rows, d:d + AUX],
                                                       res_aux[mine].astype(BF16))
        return carry

    lax.fori_loop(0, n_pass, one_pass, 0)


def _gather(starts, pos, gate_t, h2, cap, win, eg):
    b, t, d = h2.shape
    ne = pos.shape[1]
    ntb = t // TOK_BLOCK
    pos4 = pos.reshape(b, ne, ntb, TOK_BLOCK)
    grid_spec = pltpu.PrefetchScalarGridSpec(
        num_scalar_prefetch=1,
        grid=(b, ne // eg, ntb),
        in_specs=[pl.BlockSpec((1, eg, ntb, TOK_BLOCK), lambda bi, g, j, s: (bi, g, 0, 0)),
                  pl.BlockSpec((1, TOK_BLOCK, ne), lambda bi, g, j, s: (bi, j, 0)),
                  pl.BlockSpec((1, TOK_BLOCK, d), lambda bi, g, j, s: (bi, j, 0))],
        out_specs=pl.BlockSpec((1, eg, cap, d + AUX), lambda bi, g, j, s: (bi, g, 0, 0)))
    return pl.pallas_call(
        functools.partial(_gather_kernel, win),
        out_shape=jax.ShapeDtypeStruct((b, ne, cap, d + AUX), BF16),
        grid_spec=grid_spec,
        compiler_params=pltpu.CompilerParams(
            dimension_semantics=("arbitrary", "arbitrary", "arbitrary"), vmem_limit_bytes=VMEM_LIMIT),
        name="gather",
    )(starts.reshape(-1), pos4, gate_t, h2)


FF_SUB = 512


def _ffn_kernel(xs_ref, wg_ref, wu_ref, wd_ref, gf_ref, ye_ref, acc_ref):
    b = pl.program_id(0)
    e = pl.program_id(1)
    f = pl.program_id(2)
    ne = pl.num_programs(1)
    d = ye_ref.shape[-1]
    xs = xs_ref[0, 0, :, 0:d]
    tf = wg_ref.shape[2]

    @pl.when(f == 0)
    def _():
        acc_ref[...] = jnp.zeros_like(acc_ref)

    acc = acc_ref[...]
    for c0 in range(0, tf, FF_SUB):
        a = _dot(xs, wg_ref[0, :, c0:c0 + FF_SUB].astype(BF16))
        u = _dot(xs, wu_ref[0, :, c0:c0 + FF_SUB].astype(BF16))
        acc = acc + _dot((_silu(a) * u).astype(BF16), wd_ref[0, c0:c0 + FF_SUB, :].astype(BF16))
    acc_ref[...] = acc
    aux = xs_ref[0, 0, :, d:d + AUX].astype(F32)
    lane = lax.broadcasted_iota(I32, aux.shape, 1)
    mine = jnp.where(lane < 3 * ne, jnp.where(lane % ne == e, aux, 0.0), 0.0)
    gate = jnp.sum(mine, axis=-1, keepdims=True)
    ye_ref[0, 0] = (acc * gate * gf_ref[pl.ds(b, 1), :]).astype(BF16)


def _ffn(xs, w_gate, w_up, w_down, mod, tf):
    b, ne, cap, daux = xs.shape
    d = daux - AUX
    ff = w_gate.shape[2]
    return pl.pallas_call(
        _ffn_kernel,
        out_shape=jax.ShapeDtypeStruct((b, ne, cap, d), BF16),
        grid=(b, ne, ff // tf),
        in_specs=[pl.BlockSpec((1, 1, cap, daux), lambda bi, e, f: (bi, e, 0, 0)),
                  pl.BlockSpec((1, d, tf), lambda bi, e, f: (e, 0, f)),
                  pl.BlockSpec((1, d, tf), lambda bi, e, f: (e, 0, f)),
                  pl.BlockSpec((1, tf, d), lambda bi, e, f: (e, f, 0)),
                  pl.BlockSpec((mod.shape[0], d), lambda bi, e, f: (0, N_MOD - 1))],
        out_specs=pl.BlockSpec((1, 1, cap, d), lambda bi, e, f: (bi, e, 0, 0)),
        scratch_shapes=[pltpu.VMEM((cap, d), F32)],
        compiler_params=pltpu.CompilerParams(dimension_semantics=("arbitrary", "arbitrary", "arbitrary"),
                                             vmem_limit_bytes=VMEM_LIMIT),
        name="ffn",
    )(xs, w_gate, w_up, w_down, mod)


def _combine_kernel(win, starts_ref, post_ref, ye_ref, x1_ref, fw_ref, out_ref, acc_ref, stack_ref):
    b = pl.program_id(0)
    j = pl.program_id(1)
    ne, cap = ye_ref.shape[1], ye_ref.shape[2]
    acc_ref[...] = x1_ref[0]
    s_lo, s_hi, a0, n_pass = _slot_windows(starts_ref, b, 0, ne, ne, j, win)
    row = lax.broadcasted_iota(I32, (win, TOK_BLOCK), 0)

    def one_pass(p, carry):
        onehots = []
        for e in range(ne):
            a_c, lo_i, hi_i = _window(s_lo[e], s_hi[e], a0[e], p, win, cap)
            prow = post_ref[0, e, pl.ds(j, 1), :]
            owned = jnp.where(prow >= lo_i, jnp.where(prow < hi_i, prow, -1), -1)
            onehots.append(jnp.where(owned == row + a_c, 1.0, 0.0).astype(BF16))
            stack_ref[e * win:(e + 1) * win, :] = ye_ref[0, e, pl.ds(a_c, win), :]
        acc_ref[...] += _dot_tn(jnp.concatenate(onehots, axis=0), stack_ref[...])
        return carry

    lax.fori_loop(0, n_pass, one_pass, 0)
    out_ref[0] = _rms(acc_ref[...], fw_ref[...])


def _combine(starts, pos, ye, x1, final_w, win):
    b, t, d = x1.shape
    ne, cap = ye.shape[1], ye.shape[2]
    ntb = t // TOK_BLOCK
    pos4 = pos.reshape(b, ne, ntb, TOK_BLOCK)
    grid_spec = pltpu.PrefetchScalarGridSpec(
        num_scalar_prefetch=1,
        grid=(b, ntb),
        in_specs=[pl.BlockSpec((1, ne, ntb, TOK_BLOCK), lambda bi, j, s: (bi, 0, 0, 0)),
                  pl.BlockSpec((1, ne, cap, d), lambda bi, j, s: (bi, 0, 0, 0),
                               pipeline_mode=pl.Buffered(1)),
                  pl.BlockSpec((1, TOK_BLOCK, d), lambda bi, j, s: (bi, j, 0)),
                  pl.BlockSpec((1, d), lambda bi, j, s: (0, 0))],
        out_specs=pl.BlockSpec((1, TOK_BLOCK, d), lambda bi, j, s: (bi, j, 0)),
        scratch_shapes=[pltpu.VMEM((TOK_BLOCK, d), F32), pltpu.VMEM((ne * win, d), BF16)])
    return pl.pallas_call(
        functools.partial(_combine_kernel, win),
        out_shape=jax.ShapeDtypeStruct((b, t, d), F32),
        grid_spec=grid_spec,
        compiler_params=pltpu.CompilerParams(
            dimension_semantics=("arbitrary", "arbitrary"), vmem_limit_bytes=VMEM_LIMIT),
        name="combine",
    )(starts.reshape(-1), pos4, ye, x1, final_w.reshape(1, d))


def kernel(x, c, ctx, c_ctx, ada_w, ada_b, norm_mix_w, norm_ffn_w, w_in, hgrn_lb_logits, hgrn_norm_w,
           sgu_norm_w, sgu_w, sgu_b, w_branch_a, w_branch_b, w_out, router_w, expert_w_gate,
           expert_w_up, expert_w_down, final_norm_w):
    b, t, d = x.shape
    assert b + 1 <= 8 and t % TOK_BLOCK == 0 and d % LANES == 0
    ne = router_w.shape[-1]
    cap = CAPACITY_FACTOR * t // ne
    win = min(cap, 64)
    assert cap % SLOT_ALIGN == 0 and t // LANES < LANES and 3 * ne <= AUX
    (x1, h2, aff_t), mod = _front(x, c, ctx, c_ctx, ada_w, ada_b, norm_mix_w, norm_ffn_w, w_in,
                                  hgrn_lb_logits, hgrn_norm_w, sgu_norm_w, sgu_w, sgu_b,
                                  w_branch_a, w_branch_b, w_out, router_w, 256)
    pos, gate, starts = _route(aff_t, cap)
    xs = _gather(starts, pos, jnp.swapaxes(gate, 1, 2), h2, cap, win, 8)
    ye = _ffn(xs, expert_w_gate[0], expert_w_up[0], expert_w_down[0], mod, 1024)
    return _combine(starts, pos, ye, x1, final_norm_w, win)
```

```python
import functools

import numpy as np
import jax
import jax.numpy as jnp
from jax import lax
from jax.experimental import pallas as pl
from jax.experimental.pallas import tpu as pltpu

F32 = jnp.float32
BF16 = jnp.bfloat16
I32 = jnp.int32

LANES = 128
EPS = 1e-6
N_MOD = 6
HEADS = 4
HD = 128
WIDTH = HEADS * HD
CHUNK = 128
N_EXPERTS = 16
CAPACITY_FACTOR = 2
LEVELS = tuple(CHUNK >> (i + 1) for i in range(7))
N_DMAT = 2 + len(LEVELS)
PAIR = 2 * HD
N_PAIRS = HEADS // 2
ROW_GROUP = 16
LOG2E = 1.4426950408889634
VMEM_LIMIT = 52 * 1024 * 1024


def _dot(a, b):
    return jnp.dot(a, b, preferred_element_type=F32)


def _dot_nt(a, b):
    return lax.dot_general(a, b, (((1,), (1,)), ((), ())), preferred_element_type=F32)


def _dot_tn(a, b):
    return lax.dot_general(a, b, (((0,), (0,)), ((), ())), preferred_element_type=F32)


def _split2(x):
    hi = x.astype(BF16)
    lo = (x - hi.astype(F32)).astype(BF16)
    return hi, lo


def _split3(x):
    hi = x.astype(BF16)
    r = x - hi.astype(F32)
    mid = r.astype(BF16)
    lo = (r - mid.astype(F32)).astype(BF16)
    return hi, mid, lo


def _sigmoid(x):
    return 1.0 / (1.0 + jnp.exp(-x))


def _silu(x):
    return x * _sigmoid(x)


def _gelu_tanh(x):
    c = np.sqrt(2.0 / np.pi).astype(np.float32)
    return 0.5 * x * (1.0 + jnp.tanh(c * (x + 0.044715 * (x * x * x))))


def _rms(x, w):
    return x * lax.rsqrt(jnp.mean(x * x, axis=-1, keepdims=True) + EPS) * w


def _lower_bound(lbl, d):
    l0 = lbl[2 * d:2 * d + 1, :]
    l1 = lbl[2 * d + 1:2 * d + 2, :]
    m = jnp.maximum(l0, l1)
    e0 = jnp.exp(l0 - m)
    e1 = jnp.exp(l1 - m)
    return e0 / (e0 + e1)


def _forget(z, lb):
    f = lb + (1.0 - lb) * _sigmoid(z)
    return jnp.log(f), 1.0 - f


def _cumsum_dot(m_bf16, x):
    hi, mid, lo = _split3(x)
    return _dot(m_bf16, hi) + _dot(m_bf16, mid) + _dot(m_bf16, lo)


def _same_head(shape):
    r = lax.broadcasted_iota(I32, shape, 0) < HD
    c = lax.broadcasted_iota(I32, shape, 1) < HD
    return r == c


def _block_diag(x):
    first = lax.broadcasted_iota(I32, x.shape, 1) < HD
    zero = jnp.zeros_like(x)
    return jnp.concatenate([jnp.where(first, x, zero), jnp.where(first, zero, x)], axis=0)


def _pair_outer(v_pair, w_pair):
    full = _dot_tn(v_pair, w_pair)
    return jnp.where(_same_head(full.shape), full, 0.0)


def _decay_matrices(reverse):
    c = CHUNK
    i = np.arange(c)[:, None]
    m = np.arange(c)[None, :]
    mats = []
    if not reverse:
        mats.append(m <= i)
        mats.append(m > i)
    else:
        mats.append(m >= i)
        mats.append(m < i)
    for half in LEVELS:
        a = (i // (2 * half)) * (2 * half)
        mid = a + half
        if not reverse:
            qside = i >= mid
            mat = np.where(qside, (m >= mid) & (m <= i), (m > i) & (m < mid))
        else:
            qside = i < mid
            mat = np.where(qside, (m >= i) & (m < mid), (m >= mid) & (m < i))
        mats.append(mat)
    return np.concatenate(mats, axis=0).astype(np.float32)


def _block_decay_matrix(n, reverse):
    i = np.arange(n)[:, None]
    m = np.arange(n)[None, :]
    return ((m < i) if reverse else (m > i)).astype(np.float32)


def _adaln_kernel(cond_ref, w_ref, b_ref, out_ref):
    s = _silu(cond_ref[...])
    s_hi, s_lo = _split2(s)
    w_hi, w_lo = _split2(w_ref[...])
    out_ref[...] = _dot(s_hi, w_hi) + _dot(s_hi, w_lo) + _dot(s_lo, w_hi) + b_ref[...]


def _adaln(cond, ada_w, ada_b):
    rows, d = cond.shape
    n = ada_w.shape[1]
    tn = 1024
    return pl.pallas_call(
        _adaln_kernel,
        out_shape=jax.ShapeDtypeStruct((rows, n), F32),
        grid=(n // tn,),
        in_specs=[pl.BlockSpec((rows, d), lambda j: (0, 0)),
                  pl.BlockSpec((d, tn), lambda j: (0, j)),
                  pl.BlockSpec((1, tn), lambda j: (0, j))],
        out_specs=pl.BlockSpec((rows, tn), lambda j: (0, j)),
        compiler_params=pltpu.CompilerParams(dimension_semantics=("arbitrary",),
                                             vmem_limit_bytes=VMEM_LIMIT),
        name="adaln",
    )(cond, ada_w, ada_b.reshape(1, n))


def _mod_rows(mod_ref, row, d):
    return [mod_ref[pl.ds(row, 1), j * d:(j + 1) * d] for j in range(N_MOD)]


def _ctx_kernel(ctx_row, x_ref, mod_ref, nw_ref, w_ref, lbl_ref, mf_ref, mb_ref, sf_ref, sb_ref):
    d = x_ref.shape[-1]
    x = x_ref[0]
    sh, sc = mod_ref[pl.ds(ctx_row, 1), 0:d], mod_ref[pl.ds(ctx_row, 1), d:2 * d]
    h = _rms(x, nw_ref[...]) * (1.0 + sc) + sh
    p = _dot(h.astype(BF16), w_ref[...])
    v = p[:, 2 * WIDTH:3 * WIDTH].astype(BF16)
    lbl = lbl_ref[...]
    for dirn, (m_ref, out_ref) in enumerate(((mf_ref, sf_ref), (mb_ref, sb_ref))):
        logf, k = _forget(p[:, dirn * WIDTH:(dirn + 1) * WIDTH], _lower_bound(lbl, dirn))
        w = (k * jnp.exp(_cumsum_dot(m_ref[...], logf))).astype(BF16)
        for pr in range(N_PAIRS):
            sl = slice(pr * PAIR, (pr + 1) * PAIR)
            out_ref[0, pr] = _pair_outer(v[:, sl], w[:, sl])


def _ctx_states(ctx, mod, norm_w, w_c, lbl, ctx_row):
    b, l, d = ctx.shape
    mf = jnp.asarray(_block_decay_matrix(l, False), BF16)
    mb = jnp.asarray(_block_decay_matrix(l, True), BF16)
    full = lambda a: pl.BlockSpec(a.shape, lambda i: (0,) * a.ndim)
    st = jax.ShapeDtypeStruct((b, N_PAIRS, PAIR, PAIR), F32)
    st_spec = pl.BlockSpec((1, N_PAIRS, PAIR, PAIR), lambda i: (i, 0, 0, 0))
    return pl.pallas_call(
        functools.partial(_ctx_kernel, ctx_row),
        out_shape=(st, st),
        grid=(b,),
        in_specs=[pl.BlockSpec((1, l, d), lambda i: (i, 0, 0)), full(mod), full(norm_w),
                  full(w_c), full(lbl), full(mf), full(mb)],
        out_specs=(st_spec, st_spec),
        compiler_params=pltpu.CompilerParams(dimension_semantics=("arbitrary",),
                                             vmem_limit_bytes=VMEM_LIMIT),
        name="ctx_state",
    )(ctx, mod, norm_w, w_c, lbl, mf, mb)


A_Q, A_LF, A_KF, A_LB, A_KB, A_V, A_G, A_U, A_SV = (j * WIDTH for j in range(9))
A_GATES = 9 * WIDTH


def _in_proj_kernel(sub, x_ref, mod_ref, nw_ref, w_ref, lbl_ref, snw_ref, m_ref, s0_ref,
                    act_ref, bst_ref, st_ref):
    b = pl.program_id(0)
    d = x_ref.shape[-1]

    @pl.when(pl.program_id(1) == 0)
    def _():
        st_ref[...] = s0_ref[0]

    sh, sc = mod_ref[pl.ds(b, 1), 0:d], mod_ref[pl.ds(b, 1), d:2 * d]
    h = (_rms(x_ref[0], nw_ref[...]) * (1.0 + sc) + sh).astype(BF16)
    lbl = lbl_ref[...]

    def proj(j):
        return _dot(h, w_ref[:, j * WIDTH:(j + 1) * WIDTH])

    act_ref[0, :, A_Q:A_Q + WIDTH] = _silu(proj(0))
    for j, a_l, a_k, dirn in ((1, A_LF, A_KF, 0), (2, A_LB, A_KB, 1)):
        logf, k = _forget(proj(j), _lower_bound(lbl, dirn))
        act_ref[0, :, a_l:a_l + WIDTH] = logf
        act_ref[0, :, a_k:a_k + WIDTH] = k
    act_ref[0, :, A_V:A_V + WIDTH] = proj(3)
    act_ref[0, :, A_G:A_G + WIDTH] = _silu(proj(4))
    act_ref[0, :, A_U:A_U + WIDTH] = _gelu_tanh(proj(5))
    act_ref[0, :, A_SV:A_SV + WIDTH] = _group_rms(_gelu_tanh(proj(6))) * snw_ref[...]
    for j in range(2 * d // WIDTH):
        act_ref[0, :, A_GATES + j * WIDTH:A_GATES + (j + 1) * WIDTH] = _sigmoid(proj(7 + j))

    for piece in range(x_ref.shape[1] // sub - 1, -1, -1):
        rows = slice(piece * sub, (piece + 1) * sub)
        bst_ref[0, piece] = st_ref[...]
        logf = act_ref[0, rows, A_LB:A_LB + WIDTH]
        w = (act_ref[0, rows, A_KB:A_KB + WIDTH] * jnp.exp(_cumsum_dot(m_ref[...], logf))).astype(BF16)
        v = act_ref[0, rows, A_V:A_V + WIDTH].astype(BF16)
        tot = jnp.exp(jnp.sum(logf, axis=0, keepdims=True))
        for pr in range(N_PAIRS):
            sl = slice(pr * PAIR, (pr + 1) * PAIR)
            st_ref[pr] = st_ref[pr] * tot[:, sl] + _pair_outer(v[:, sl], w[:, sl])


def _in_proj(x, mod, norm_w, w_in, lbl, snw, s_b0, tm, sub):
    b, t, d = x.shape
    nb = t // tm
    per = tm // sub
    m = jnp.asarray(_block_decay_matrix(sub, True), BF16)
    full = lambda a: pl.BlockSpec(a.shape, lambda bi, i: (0,) * a.ndim, pipeline_mode=pl.Buffered(1))
    n_act = A_GATES + 2 * d
    return pl.pallas_call(
        functools.partial(_in_proj_kernel, sub),
        out_shape=(jax.ShapeDtypeStruct((b, t, n_act), F32),
                   jax.ShapeDtypeStruct((b, t // sub, N_PAIRS, PAIR, PAIR), F32)),
        grid=(b, nb),
        in_specs=[pl.BlockSpec((1, tm, d), lambda bi, i: (bi, nb - 1 - i, 0)), full(mod), full(norm_w),
                  full(w_in), full(lbl), full(snw), full(m),
                  pl.BlockSpec((1, N_PAIRS, PAIR, PAIR), lambda bi, i: (bi, 0, 0, 0))],
        out_specs=(pl.BlockSpec((1, tm, n_act), lambda bi, i: (bi, nb - 1 - i, 0)),
                   pl.BlockSpec((1, per, N_PAIRS, PAIR, PAIR), lambda bi, i: (bi, nb - 1 - i, 0, 0, 0))),
        scratch_shapes=[pltpu.VMEM((N_PAIRS, PAIR, PAIR), F32)],
        compiler_params=pltpu.CompilerParams(dimension_semantics=("arbitrary", "arbitrary"),
                                             vmem_limit_bytes=VMEM_LIMIT),
        name="in_proj",
    )(x, mod, norm_w, w_in, lbl, snw, m, s_b0)


def _level_masks(reverse):
    row = lax.broadcasted_iota(I32, (CHUNK, PAIR), 0)
    col = lax.broadcasted_iota(I32, (CHUNK, PAIR), 1) & (CHUNK - 1)
    x = row ^ col
    out = []
    for half in LEVELS:
        in_pair = jnp.where(x >= half, jnp.where(x < 2 * half, 1.0, 0.0), 0.0)
        bit = (col if reverse else row) & half
        out.append((jnp.where(bit != 0, in_pair, 0.0), (row & half) != 0))
    return out


def _hgrn_scores(q, k, logf, dm_ref, masks, reverse, pm_ref, dio_ref):
    hi, lo = _split2(logf * LOG2E)
    dall = _dot(dm_ref[...], jnp.concatenate([hi, lo], axis=0))
    dio_ref[...] = dall[0:2 * CHUNK]
    n_groups = CHUNK // ROW_GROUP
    for pr in range(N_PAIRS):
        sl = slice(pr * PAIR, (pr + 1) * PAIR)
        qp, kp = q[:, sl], k[:, sl]
        pm = [jnp.zeros((ROW_GROUP, PAIR), F32) for _ in range(n_groups)]
        for li, half in enumerate(LEVELS):
            mask, row_bit = masks[li]
            qside = jnp.logical_not(row_bit) if reverse else row_bit
            e = jnp.exp2(dall[(2 + li) * CHUNK:(3 + li) * CHUNK, sl])
            xm = (jnp.where(qside, qp, kp) * e).astype(BF16)
            groups = [g for g in range(n_groups)
                      if half < ROW_GROUP or (((g * ROW_GROUP) & half) != 0) != reverse]
            lhs = xm if len(groups) == n_groups else jnp.concatenate(
                [xm[g * ROW_GROUP:(g + 1) * ROW_GROUP] for g in groups], axis=0)
            gm = _dot_nt(lhs, _block_diag(xm))
            for n, g in enumerate(groups):
                pm[g] = pm[g] + (gm[n * ROW_GROUP:(n + 1) * ROW_GROUP]
                                 * mask[g * ROW_GROUP:(g + 1) * ROW_GROUP])
        pm_ref[pr] = jnp.concatenate(pm, axis=0).astype(BF16)


def _hgrn_apply(q, k, v, reverse, pm_ref, dio_ref, state_ref, o_ref, r0):
    e_in = jnp.exp2(dio_ref[0:CHUNK])
    e_st = jnp.exp2(dio_ref[CHUNK:2 * CHUNK])
    tot_row = 0 if reverse else CHUNK - 1
    e_tot = e_in[tot_row:tot_row + 1]
    first = lax.broadcasted_iota(I32, (CHUNK, PAIR), 1) < HD
    for pr in range(N_PAIRS):
        sl = slice(pr * PAIR, (pr + 1) * PAIR)
        qp, kp, vp = q[:, sl], k[:, sl], v[:, sl]
        qk = qp * kp
        diag = jnp.where(first, jnp.sum(qk[:, 0:HD], axis=-1, keepdims=True),
                         jnp.sum(qk[:, HD:PAIR], axis=-1, keepdims=True))
        st = state_ref[pr]
        o = (_dot(pm_ref[pr], _block_diag(vp.astype(BF16))) + diag * vp
             + _dot_nt((qp * e_in[:, sl]).astype(BF16), st.astype(BF16)))
        o_ref[pl.ds(r0, CHUNK), sl] += o
        state_ref[pr] = st * e_tot[:, sl] + _pair_outer(vp.astype(BF16), (kp * e_st[:, sl]).astype(BF16))


def _group_rms(x):
    return jnp.concatenate(
        [x[:, g * HD:(g + 1) * HD]
         * lax.rsqrt(jnp.mean(x[:, g * HD:(g + 1) * HD] ** 2, axis=-1, keepdims=True) + EPS)
         for g in range(HEADS)], axis=-1)


def _mixer_kernel(act_ref, x_ref, mod_ref, nffn_ref, hnw_ref,
                  sw_ref, sbias_ref, wa_ref, wb_ref, wo_ref, rw_ref, dmf_ref, dmb_ref,
                  sf0_ref, bst_ref,
                  x1_ref, h2_ref, aff_ref,
                  o_ref, sg_ref, stf_ref, stb_ref, pm_ref, dio_ref):
    b = pl.program_id(0)
    i = pl.program_id(1)
    tb, d = x_ref.shape[1], x_ref.shape[2]
    nch = tb // CHUNK
    _, _, g_m, sh_f, sc_f, _ = _mod_rows(mod_ref, b, d)
    act = lambda rows, c0, width=WIDTH: act_ref[0, rows, c0:c0 + width]
    every = slice(None)

    @pl.when(i == 0)
    def _():
        stf_ref[...] = sf0_ref[0]

    stb_ref[...] = bst_ref[0, 0]

    o_ref[...] = jnp.zeros_like(o_ref)
    work = []
    for reverse in (False, True):
        work += [(reverse, ci) for ci in (range(nch - 1, -1, -1) if reverse else range(nch))]
    masks = {reverse: _level_masks(reverse) for reverse in (False, True)}
    for n, (reverse, ci) in enumerate(work):
        rows = pl.ds(ci * CHUNK, CHUNK)
        a_l, a_k, dm_ref = (A_LB, A_KB, dmb_ref) if reverse else (A_LF, A_KF, dmf_ref)
        _hgrn_scores(act(rows, A_Q), act(rows, a_k), act(rows, a_l), dm_ref, masks[reverse], reverse,
                     pm_ref.at[n], dio_ref.at[n])
    for n, (reverse, ci) in enumerate(work):
        rows = pl.ds(ci * CHUNK, CHUNK)
        a_k, st_ref = (A_KB, stb_ref) if reverse else (A_KF, stf_ref)
        _hgrn_apply(act(rows, A_Q), act(rows, a_k), act(rows, A_V), reverse, pm_ref.at[n], dio_ref.at[n],
                    st_ref, o_ref, ci * CHUNK)

    a_in = _group_rms(o_ref[...]) * hnw_ref[...] * act(every, A_G)
    y_a = _dot(a_in.astype(BF16), wa_ref[...])

    for ci in range(nch):
        rows = slice(ci * CHUNK, (ci + 1) * CHUNK)
        for g in range(HEADS):
            sl = slice(g * HD, (g + 1) * HD)
            sg_ref[rows, sl] = _dot(sw_ref[g], act(rows, A_SV + g * HD, HD).astype(BF16)) + sbias_ref[:, sl]
    y_b = _dot((act(every, A_U) * sg_ref[...]).astype(BF16), wb_ref[...])

    merged = act(every, A_GATES, d) * y_a + act(every, A_GATES + d, d) * y_b
    y = _dot(merged.astype(BF16), wo_ref[...])
    x1 = x_ref[0] + g_m * y
    x1_ref[0] = x1

    h2 = _rms(x1, nffn_ref[...]) * (1.0 + sc_f) + sh_f
    h2_hi, h2_lo = _split2(h2)
    h2_ref[0] = h2_hi
    rw_hi, rw_lo = _split2(rw_ref[...])
    logits = _dot_nt(rw_hi, h2_hi) + _dot_nt(rw_hi, h2_lo) + _dot_nt(rw_lo, h2_hi)
    mx = jnp.max(logits, axis=0, keepdims=True)
    ex = jnp.exp(logits - mx)
    aff_ref[0] = ex / jnp.sum(ex, axis=0, keepdims=True)


def _mixer(act, x, mod, nffn, hnw, sgu_w, sgu_bias, w_a, w_b, w_o, rw_t, s_f0, bstates, tb):
    b, t, d = x.shape
    nb = t // tb
    per_state = bstates.shape[1] // nb
    ne = rw_t.shape[0]
    dmf = jnp.asarray(np.tile(_decay_matrices(False), (1, 2)), BF16)
    dmb = jnp.asarray(np.tile(_decay_matrices(True), (1, 2)), BF16)
    const = lambda a: pl.BlockSpec(a.shape, lambda bi, i: (0,) * a.ndim, pipeline_mode=pl.Buffered(1))
    in_specs = [pl.BlockSpec((1, tb, act.shape[2]), lambda bi, i: (bi, i, 0)),
                pl.BlockSpec((1, tb, d), lambda bi, i: (bi, i, 0)),
                const(mod), const(nffn), const(hnw),
                const(sgu_w), const(sgu_bias), const(w_a), const(w_b), const(w_o), const(rw_t),
                const(dmf), const(dmb),
                pl.BlockSpec((1, N_PAIRS, PAIR, PAIR), lambda bi, i: (bi, 0, 0, 0)),
                pl.BlockSpec((1, 1, N_PAIRS, PAIR, PAIR), lambda bi, i: (bi, (i + 1) * per_state - 1, 0, 0, 0))]
    out_shape = (jax.ShapeDtypeStruct((b, t, d), F32),
                 jax.ShapeDtypeStruct((b, t, d), BF16),
                 jax.ShapeDtypeStruct((b, ne, t), F32))
    out_specs = (pl.BlockSpec((1, tb, d), lambda bi, i: (bi, i, 0)),
                 pl.BlockSpec((1, tb, d), lambda bi, i: (bi, i, 0)),
                 pl.BlockSpec((1, ne, tb), lambda bi, i: (bi, 0, i)))
    return pl.pallas_call(
        _mixer_kernel,
        out_shape=out_shape,
        grid=(b, nb),
        in_specs=in_specs,
        out_specs=out_specs,
        scratch_shapes=[pltpu.VMEM((tb, WIDTH), F32),
                        pltpu.VMEM((tb, WIDTH), F32),
                        pltpu.VMEM((N_PAIRS, PAIR, PAIR), F32),
                        pltpu.VMEM((N_PAIRS, PAIR, PAIR), F32),
                        pltpu.VMEM((2 * (tb // CHUNK), N_PAIRS, CHUNK, PAIR), BF16),
                        pltpu.VMEM((2 * (tb // CHUNK), 2 * CHUNK, WIDTH), F32)],
        compiler_params=pltpu.CompilerParams(dimension_semantics=("arbitrary", "arbitrary"),
                                             vmem_limit_bytes=VMEM_LIMIT),
        name="mixer",
    )(act, x, mod, nffn, hnw, sgu_w, sgu_bias, w_a, w_b, w_o, rw_t, dmf, dmb, s_f0, bstates)


def _front(x, c, ctx, c_ctx, ada_w, ada_b, norm_mix_w, norm_ffn_w, w_in, hgrn_lb_logits,
           hgrn_norm_w, sgu_norm_w, sgu_w, sgu_b, w_branch_a, w_branch_b, w_out, router_w, tb):
    b, t, d = x.shape
    layer = 0
    cond = jnp.zeros((8, d), F32).at[0:b].set(c).at[b].set(c_ctx)
    mod = _adaln(cond, ada_w[layer], ada_b[layer])
    lbl = hgrn_lb_logits[:, layer:layer + 2, :].reshape(4, WIDTH)
    nmix = norm_mix_w[layer].reshape(1, d)
    nffn = norm_ffn_w[layer].reshape(1, d)
    w_in_b = w_in[layer].astype(BF16)
    s_f0, s_b0 = _ctx_states(ctx, mod, nmix, w_in_b[:, WIDTH:4 * WIDTH], lbl, b)
    act, bstates = _in_proj(x, mod, nmix, w_in_b, lbl, sgu_norm_w[layer].reshape(1, WIDTH), s_b0,
                            min(t, 512), 256)
    sgu_bias = jnp.repeat(sgu_b[layer].T, HD, axis=1)
    return _mixer(act, x, mod, nffn, hgrn_norm_w[layer].reshape(1, WIDTH),
                  sgu_w[layer].astype(BF16), sgu_bias,
                  w_branch_a[layer].astype(BF16), w_branch_b[layer].astype(BF16),
                  w_out[layer].astype(BF16), router_w[layer].T, s_f0, bstates, tb), mod


def _route_kernel(cap, aff_ref, pos_ref, gate_ref, starts_ref):
    a = aff_ref[0]
    ne, t = a.shape
    nblk = t // LANES
    bits = pltpu.bitcast(a, I32)

    def search(it, lo):
        cand = lo | (jnp.int32(1) << (30 - it))
        cnt = jnp.sum(jnp.where(bits >= cand, 1.0, 0.0), axis=-1, keepdims=True)
        return jnp.where(cnt >= cap, cand, lo)

    thr = lax.fori_loop(0, 31, search, jnp.zeros((ne, 1), I32))
    gt = bits > thr
    eq = bits == thr
    n_ties_wanted = cap - jnp.sum(jnp.where(gt, 1.0, 0.0), axis=-1, keepdims=True)

    row = lax.broadcasted_iota(I32, (LANES, LANES), 0)
    col = lax.broadcasted_iota(I32, (LANES, LANES), 1)
    upper = jnp.where(row <= col, 1.0, 0.0).astype(BF16)
    lane = lax.broadcasted_iota(I32, (ne, LANES), 1)

    off = jnp.zeros((ne, 1), F32)
    sel_blocks = []
    for j in range(nblk):
        sl = slice(j * LANES, (j + 1) * LANES)
        eqf = jnp.where(eq[:, sl], 1.0, 0.0)
        incl = _dot(eqf.astype(BF16), upper) + off
        keep_tie = jnp.where(incl - eqf < n_ties_wanted, eqf, 0.0)
        sel_blocks.append(jnp.where(gt[:, sl], 1.0, keep_tie))
        off = incl[:, LANES - 1:LANES]

    off = jnp.zeros((ne, 1), F32)
    starts = jnp.zeros((ne, LANES), F32)
    for j in range(nblk):
        sl = slice(j * LANES, (j + 1) * LANES)
        self = sel_blocks[j]
        starts = jnp.where(lane == j, off, starts)
        incl = _dot(self.astype(BF16), upper) + off
        pos_ref[0, :, sl] = jnp.where(self > 0.0, incl - 1.0, -1.0).astype(I32)
        gate_ref[0, :, sl] = jnp.where(self > 0.0, a[:, sl], 0.0)
        off = incl[:, LANES - 1:LANES]
    starts = jnp.where(lane >= nblk, off, starts)
    starts_ref[0] = starts.astype(I32)


def _route(aff_t, cap):
    b, ne, t = aff_t.shape
    spec = pl.BlockSpec((1, ne, t), lambda i: (i, 0, 0))
    return pl.pallas_call(
        functools.partial(_route_kernel, cap),
        out_shape=(jax.ShapeDtypeStruct((b, ne, t), I32),
                   jax.ShapeDtypeStruct((b, ne, t), F32),
                   jax.ShapeDtypeStruct((b, ne, LANES), I32)),
        grid=(b,),
        in_specs=[spec],
        out_specs=(spec, spec, pl.BlockSpec((1, ne, LANES), lambda i: (i, 0, 0))),
        compiler_params=pltpu.CompilerParams(dimension_semantics=("arbitrary",),
                                             vmem_limit_bytes=VMEM_LIMIT),
        name="route",
    )(aff_t)


TOK_BLOCK = 2 * LANES


SLOT_ALIGN = 16
AUX = LANES


def _slot_windows(starts_ref, b, e0, n_exp, ne, j, win):
    per = TOK_BLOCK // LANES
    s_lo, s_hi, a0 = [], [], []
    n_pass = jnp.int32(0)
    for le in range(n_exp):
        base = (b * ne + e0 + le) * LANES
        lo = starts_ref[base + per * j]
        hi = starts_ref[base + per * (j + 1)]
        a = (lo // SLOT_ALIGN) * SLOT_ALIGN
        n_pass = jnp.maximum(n_pass, jnp.where(hi > lo, (hi - a + win - 1) // win, 0))
        s_lo.append(lo)
        s_hi.append(hi)
        a0.append(a)
    return s_lo, s_hi, a0, n_pass


def _window(s_lo, s_hi, a0, p, win, cap):
    a = a0 + p * win
    a_c = pl.multiple_of(jnp.minimum(a, cap - win), SLOT_ALIGN)
    return a_c, jnp.maximum(s_lo, a), jnp.minimum(s_hi, a + win)


def _gather_kernel(win, starts_ref, pos_ref, gate_ref, h2_ref, xs_ref):
    b = pl.program_id(0)
    g = pl.program_id(1)
    j = pl.program_id(2)
    eg, cap = xs_ref.shape[1], xs_ref.shape[2]
    ne = gate_ref.shape[2]
    d = h2_ref.shape[2]

    @pl.when(j == 0)
    def _():
        xs_ref[...] = jnp.zeros_like(xs_ref)

    pr = lax.broadcasted_iota(I32, (ne, AUX), 0)
    pc = lax.broadcasted_iota(I32, (ne, AUX), 1)
    pieces = _split3(gate_ref[0])
    gcols = sum(_dot(piece, jnp.where(pc == k * ne + pr, 1.0, 0.0).astype(BF16))
                for k, piece in enumerate(pieces)).astype(BF16)

    s_lo, s_hi, a0, n_pass = _slot_windows(starts_ref, b, g * eg, eg, ne, j, win)
    row = lax.broadcasted_iota(I32, (win, TOK_BLOCK), 0)
    row_d = lax.broadcasted_iota(I32, (win, d), 0)
    row_aux = lax.broadcasted_iota(I32, (win, AUX), 0)

    def one_pass(p, carry):
        wins = [_window(s_lo[le], s_hi[le], a0[le], p, win, cap) for le in range(eg)]
        onehots = []
        for le, (a_c, lo_i, hi_i) in enumerate(wins):
            prow = pos_ref[0, le, pl.ds(j, 1), :]
            owned = jnp.where(prow >= lo_i, jnp.where(prow < hi_i, prow, -1), -1)
            onehots.append(jnp.where(owned == row + a_c, 1.0, 0.0).astype(BF16))
        lhs = jnp.concatenate(onehots, axis=0)
        res = _dot(lhs, h2_ref[0])
        res_aux = _dot(lhs, gcols)
        for le, (a_c, lo_i, hi_i) in enumerate(wins):
            rows = pl.ds(a_c, win)
            mine = slice(le * win, (le + 1) * win)
            xs_ref[0, le, rows, 0:d] = jnp.where(row_d + a_c < lo_i, xs_ref[0, le, rows, 0:d],
                                                 res[mine].astype(BF16))
            xs_ref[0, le, rows, d:d + AUX] = jnp.where(row_aux + a_c < lo_i, xs_ref[0, le, rows, d:d + AUX],
                                                       res_aux[mine].astype(BF16))
        return carry

    lax.fori_loop(0, n_pass, one_pass, 0)


def _gather(starts, pos, gate_t, h2, cap, win, eg):
    b, t, d = h2.shape
    ne = pos.shape[1]
    ntb = t // TOK_BLOCK
    pos4 = pos.reshape(b, ne, ntb, TOK_BLOCK)
    grid_spec = pltpu.PrefetchScalarGridSpec(
        num_scalar_prefetch=1,
        grid=(b, ne // eg, ntb),
        in_specs=[pl.BlockSpec((1, eg, ntb, TOK_BLOCK), lambda bi, g, j, s: (bi, g, 0, 0)),
                  pl.BlockSpec((1, TOK_BLOCK, ne), lambda bi, g, j, s: (bi, j, 0)),
                  pl.BlockSpec((1, TOK_BLOCK, d), lambda bi, g, j, s: (bi, j, 0))],
        out_specs=pl.BlockSpec((1, eg, cap, d + AUX), lambda bi, g, j, s: (bi, g, 0, 0)))
    return pl.pallas_call(
        functools.partial(_gather_kernel, win),
        out_shape=jax.ShapeDtypeStruct((b, ne, cap, d + AUX), BF16),
        grid_spec=grid_spec,
        compiler_params=pltpu.CompilerParams(
            dimension_semantics=("arbitrary", "arbitrary", "arbitrary"), vmem_limit_bytes=VMEM_LIMIT),
        name="gather",
    )(starts.reshape(-1), pos4, gate_t, h2)


FF_SUB = 512


def _ffn_kernel(xs_ref, wg_ref, wu_ref, wd_ref, gf_ref, ye_ref, acc_ref):
    bg = pl.program_id(0)
    e = pl.program_id(1)
    f = pl.program_id(2)
    ne = pl.num_programs(1)
    ns, cap, d = ye_ref.shape[0], ye_ref.shape[2], ye_ref.shape[3]
    xs = xs_ref[:, 0, :, 0:d].reshape(ns * cap, d)
    tf = wg_ref.shape[2]

    @pl.when(f == 0)
    def _():
        acc_ref[...] = jnp.zeros_like(acc_ref)

    acc = acc_ref[...]
    for c0 in range(0, tf, FF_SUB):
        a = _dot(xs, wg_ref[0, :, c0:c0 + FF_SUB].astype(BF16))
        u = _dot(xs, wu_ref[0, :, c0:c0 + FF_SUB].astype(BF16))
        acc = acc + _dot((_silu(a) * u).astype(BF16), wd_ref[0, c0:c0 + FF_SUB, :].astype(BF16))
    acc_ref[...] = acc
    for s in range(ns):
        aux = xs_ref[s, 0, :, d:d + AUX].astype(F32)
        lane = lax.broadcasted_iota(I32, aux.shape, 1)
        mine = jnp.where(lane < 3 * ne, jnp.where(lane % ne == e, aux, 0.0), 0.0)
        gate = jnp.sum(mine, axis=-1, keepdims=True)
        ye_ref[s, 0] = (acc[s * cap:(s + 1) * cap] * gate * gf_ref[pl.ds(bg * ns + s, 1), :]).astype(BF16)


def _ffn(xs, w_gate, w_up, w_down, mod, tf, ns):
    b, ne, cap, daux = xs.shape
    d = daux - AUX
    ff = w_gate.shape[2]
    return pl.pallas_call(
        _ffn_kernel,
        out_shape=jax.ShapeDtypeStruct((b, ne, cap, d), BF16),
        grid=(b // ns, ne, ff // tf),
        in_specs=[pl.BlockSpec((ns, 1, cap, daux), lambda bi, e, f: (bi, e, 0, 0)),
                  pl.BlockSpec((1, d, tf), lambda bi, e, f: (e, 0, f)),
                  pl.BlockSpec((1, d, tf), lambda bi, e, f: (e, 0, f)),
                  pl.BlockSpec((1, tf, d), lambda bi, e, f: (e, f, 0)),
                  pl.BlockSpec((mod.shape[0], d), lambda bi, e, f: (0, N_MOD - 1))],
        out_specs=pl.BlockSpec((ns, 1, cap, d), lambda bi, e, f: (bi, e, 0, 0)),
        scratch_shapes=[pltpu.VMEM((ns * cap, d), F32)],
        compiler_params=pltpu.CompilerParams(dimension_semantics=("arbitrary", "arbitrary", "arbitrary"),
                                             vmem_limit_bytes=VMEM_LIMIT),
        name="ffn",
    )(xs, w_gate, w_up, w_down, mod)


def _combine_kernel(win, starts_ref, post_ref, ye_ref, x1_ref, fw_ref, out_ref, acc_ref, stack_ref):
    b = pl.program_id(0)
    j = pl.program_id(1)
    ne, cap = ye_ref.shape[1], ye_ref.shape[2]
    acc_ref[...] = x1_ref[0]
    s_lo, s_hi, a0, n_pass = _slot_windows(starts_ref, b, 0, ne, ne, j, win)
    row = lax.broadcasted_iota(I32, (win, TOK_BLOCK), 0)

    def one_pass(p, carry):
        onehots = []
        for e in range(ne):
            a_c, lo_i, hi_i = _window(s_lo[e], s_hi[e], a0[e], p, win, cap)
            prow = post_ref[0, e, pl.ds(j, 1), :]
            owned = jnp.where(prow >= lo_i, jnp.where(prow < hi_i, prow, -1), -1)
            onehots.append(jnp.where(owned == row + a_c, 1.0, 0.0).astype(BF16))
            stack_ref[e * win:(e + 1) * win, :] = ye_ref[0, e, pl.ds(a_c, win), :]
        acc_ref[...] += _dot_tn(jnp.concatenate(onehots, axis=0), stack_ref[...])
        return carry

    lax.fori_loop(0, n_pass, one_pass, 0)
    out_ref[0] = _rms(acc_ref[...], fw_ref[...])


def _combine(starts, pos, ye, x1, final_w, win):
    b, t, d = x1.shape
    ne, cap = ye.shape[1], ye.shape[2]
    ntb = t // TOK_BLOCK
    pos4 = pos.reshape(b, ne, ntb, TOK_BLOCK)
    grid_spec = pltpu.PrefetchScalarGridSpec(
        num_scalar_prefetch=1,
        grid=(b, ntb),
        in_specs=[pl.BlockSpec((1, ne, ntb, TOK_BLOCK), lambda bi, j, s: (bi, 0, 0, 0)),
                  pl.BlockSpec((1, ne, cap, d), lambda bi, j, s: (bi, 0, 0, 0),
                               pipeline_mode=pl.Buffered(1)),
                  pl.BlockSpec((1, TOK_BLOCK, d), lambda bi, j, s: (bi, j, 0)),
                  pl.BlockSpec((1, d), lambda bi, j, s: (0, 0))],
        out_specs=pl.BlockSpec((1, TOK_BLOCK, d), lambda bi, j, s: (bi, j, 0)),
        scratch_shapes=[pltpu.VMEM((TOK_BLOCK, d), F32), pltpu.VMEM((ne * win, d), BF16)])
    return pl.pallas_call(
        functools.partial(_combine_kernel, win),
        out_shape=jax.ShapeDtypeStruct((b, t, d), F32),
        grid_spec=grid_spec,
        compiler_params=pltpu.CompilerParams(
            dimension_semantics=("arbitrary", "arbitrary"), vmem_limit_bytes=VMEM_LIMIT),
        name="combine",
    )(starts.reshape(-1), pos4, ye, x1, final_w.reshape(1, d))


def kernel(x, c, ctx, c_ctx, ada_w, ada_b, norm_mix_w, norm_ffn_w, w_in, hgrn_lb_logits, hgrn_norm_w,
           sgu_norm_w, sgu_w, sgu_b, w_branch_a, w_branch_b, w_out, router_w, expert_w_gate,
           expert_w_up, expert_w_down, final_norm_w):
    b, t, d = x.shape
    assert b + 1 <= 8 and t % TOK_BLOCK == 0 and d % LANES == 0
    ne = router_w.shape[-1]
    cap = CAPACITY_FACTOR * t // ne
    win = min(cap, 64)
    assert cap % SLOT_ALIGN == 0 and t // LANES < LANES and 3 * ne <= AUX
    (x1, h2, aff_t), mod = _front(x, c, ctx, c_ctx, ada_w, ada_b, norm_mix_w, norm_ffn_w, w_in,
                                  hgrn_lb_logits, hgrn_norm_w, sgu_norm_w, sgu_w, sgu_b,
                                  w_branch_a, w_branch_b, w_out, router_w, 512)
    pos, gate, starts = _route(aff_t, cap)
    xs = _gather(starts, pos, jnp.swapaxes(gate, 1, 2), h2, cap, win, 8)
    ye = _ffn(xs, expert_w_gate[0], expert_w_up[0], expert_w_down[0], mod, 1024, 1)
    return _combine(starts, pos, ye, x1, final_norm_w, win)
```

```python
import functools

import numpy as np
import jax
import jax.numpy as jnp
from jax import lax
from jax.experimental import pallas as pl
from jax.experimental.pallas import tpu as pltpu

F32 = jnp.float32
BF16 = jnp.bfloat16
I32 = jnp.int32

LANES = 128
EPS = 1e-6
N_MOD = 6
HEADS = 4
HD = 128
WIDTH = HEADS * HD
CHUNK = 128
N_EXPERTS = 16
CAPACITY_FACTOR = 2
LEVELS = tuple(CHUNK >> (i + 1) for i in range(7))
N_DMAT = 2 + len(LEVELS)
PAIR = 2 * HD
N_PAIRS = HEADS // 2
ROW_GROUP = 16
LOG2E = 1.4426950408889634
VMEM_LIMIT = 52 * 1024 * 1024


def _dot(a, b):
    return jnp.dot(a, b, preferred_element_type=F32)


def _dot_nt(a, b):
    return lax.dot_general(a, b, (((1,), (1,)), ((), ())), preferred_element_type=F32)


def _dot_tn(a, b):
    return lax.dot_general(a, b, (((0,), (0,)), ((), ())), preferred_element_type=F32)


def _split2(x):
    hi = x.astype(BF16)
    lo = (x - hi.astype(F32)).astype(BF16)
    return hi, lo


def _split3(x):
    hi = x.astype(BF16)
    r = x - hi.astype(F32)
    mid = r.astype(BF16)
    lo = (r - mid.astype(F32)).astype(BF16)
    return hi, mid, lo


def _sigmoid(x):
    return 1.0 / (1.0 + jnp.exp(-x))


def _silu(x):
    return x * _sigmoid(x)


def _gelu_tanh(x):
    c = np.sqrt(2.0 / np.pi).astype(np.float32)
    return 0.5 * x * (1.0 + jnp.tanh(c * (x + 0.044715 * (x * x * x))))


def _rms(x, w):
    return x * lax.rsqrt(jnp.mean(x * x, axis=-1, keepdims=True) + EPS) * w


def _lower_bound(lbl, d):
    l0 = lbl[2 * d:2 * d + 1, :]
    l1 = lbl[2 * d + 1:2 * d + 2, :]
    m = jnp.maximum(l0, l1)
    e0 = jnp.exp(l0 - m)
    e1 = jnp.exp(l1 - m)
    return e0 / (e0 + e1)


def _forget(z, lb):
    f = lb + (1.0 - lb) * _sigmoid(z)
    return jnp.log(f), 1.0 - f


def _cumsum_dot(m_bf16, x):
    hi, mid, lo = _split3(x)
    return _dot(m_bf16, hi) + _dot(m_bf16, mid) + _dot(m_bf16, lo)


def _same_head(shape):
    r = lax.broadcasted_iota(I32, shape, 0) < HD
    c = lax.broadcasted_iota(I32, shape, 1) < HD
    return r == c


def _block_diag(x):
    first = lax.broadcasted_iota(I32, x.shape, 1) < HD
    zero = jnp.zeros_like(x)
    return jnp.concatenate([jnp.where(first, x, zero), jnp.where(first, zero, x)], axis=0)


def _pair_outer(v_pair, w_pair):
    full = _dot_tn(v_pair, w_pair)
    return jnp.where(_same_head(full.shape), full, 0.0)


def _decay_matrices(reverse):
    c = CHUNK
    i = np.arange(c)[:, None]
    m = np.arange(c)[None, :]
    mats = []
    if not reverse:
        mats.append(m <= i)
        mats.append(m > i)
    else:
        mats.append(m >= i)
        mats.append(m < i)
    for half in LEVELS:
        a = (i // (2 * half)) * (2 * half)
        mid = a + half
        if not reverse:
            qside = i >= mid
            mat = np.where(qside, (m >= mid) & (m <= i), (m > i) & (m < mid))
        else:
            qside = i < mid
            mat = np.where(qside, (m >= i) & (m < mid), (m >= mid) & (m < i))
        mats.append(mat)
    return np.concatenate(mats, axis=0).astype(np.float32)


def _block_decay_matrix(n, reverse):
    i = np.arange(n)[:, None]
    m = np.arange(n)[None, :]
    return ((m < i) if reverse else (m > i)).astype(np.float32)


def _adaln_kernel(cond_ref, w_ref, b_ref, out_ref):
    s = _silu(cond_ref[...])
    s_hi, s_lo = _split2(s)
    w_hi, w_lo = _split2(w_ref[...])
    out_ref[...] = _dot(s_hi, w_hi) + _dot(s_hi, w_lo) + _dot(s_lo, w_hi) + b_ref[...]


def _adaln(cond, ada_w, ada_b):
    rows, d = cond.shape
    n = ada_w.shape[1]
    tn = 1024
    return pl.pallas_call(
        _adaln_kernel,
        out_shape=jax.ShapeDtypeStruct((rows, n), F32),
        grid=(n // tn,),
        in_specs=[pl.BlockSpec((rows, d), lambda j: (0, 0)),
                  pl.BlockSpec((d, tn), lambda j: (0, j)),
                  pl.BlockSpec((1, tn), lambda j: (0, j))],
        out_specs=pl.BlockSpec((rows, tn), lambda j: (0, j)),
        compiler_params=pltpu.CompilerParams(dimension_semantics=("arbitrary",),
                                             vmem_limit_bytes=VMEM_LIMIT),
        name="adaln",
    )(cond, ada_w, ada_b.reshape(1, n))


def _mod_rows(mod_ref, row, d):
    return [mod_ref[pl.ds(row, 1), j * d:(j + 1) * d] for j in range(N_MOD)]


def _ctx_kernel(ctx_row, x_ref, mod_ref, nw_ref, w_ref, lbl_ref, mf_ref, mb_ref, sf_ref, sb_ref):
    d = x_ref.shape[-1]
    x = x_ref[0]
    sh, sc = mod_ref[pl.ds(ctx_row, 1), 0:d], mod_ref[pl.ds(ctx_row, 1), d:2 * d]
    h = _rms(x, nw_ref[...]) * (1.0 + sc) + sh
    p = _dot(h.astype(BF16), w_ref[...])
    v = p[:, 2 * WIDTH:3 * WIDTH].astype(BF16)
    lbl = lbl_ref[...]
    for dirn, (m_ref, out_ref) in enumerate(((mf_ref, sf_ref), (mb_ref, sb_ref))):
        logf, k = _forget(p[:, dirn * WIDTH:(dirn + 1) * WIDTH], _lower_bound(lbl, dirn))
        w = (k * jnp.exp(_cumsum_dot(m_ref[...], logf))).astype(BF16)
        for pr in range(N_PAIRS):
            sl = slice(pr * PAIR, (pr + 1) * PAIR)
            out_ref[0, pr] = _pair_outer(v[:, sl], w[:, sl])


def _ctx_states(ctx, mod, norm_w, w_c, lbl, ctx_row):
    b, l, d = ctx.shape
    mf = jnp.asarray(_block_decay_matrix(l, False), BF16)
    mb = jnp.asarray(_block_decay_matrix(l, True), BF16)
    full = lambda a: pl.BlockSpec(a.shape, lambda i: (0,) * a.ndim)
    st = jax.ShapeDtypeStruct((b, N_PAIRS, PAIR, PAIR), F32)
    st_spec = pl.BlockSpec((1, N_PAIRS, PAIR, PAIR), lambda i: (i, 0, 0, 0))
    return pl.pallas_call(
        functools.partial(_ctx_kernel, ctx_row),
        out_shape=(st, st),
        grid=(b,),
        in_specs=[pl.BlockSpec((1, l, d), lambda i: (i, 0, 0)), full(mod), full(norm_w),
                  full(w_c), full(lbl), full(mf), full(mb)],
        out_specs=(st_spec, st_spec),
        compiler_params=pltpu.CompilerParams(dimension_semantics=("arbitrary",),
                                             vmem_limit_bytes=VMEM_LIMIT),
        name="ctx_state",
    )(ctx, mod, norm_w, w_c, lbl, mf, mb)


A_Q, A_LF, A_LB, A_V, A_G, A_U = (j * WIDTH for j in range(6))
A_COLS = 6 * WIDTH
G_SV, G_GATES = 0, WIDTH


def _in_proj_kernel(sub, x_ref, mod_ref, nw_ref, w_ref, lbl_ref, snw_ref, m_ref, s0_ref,
                    act_ref, gat_ref, bst_ref, st_ref, kb_ref):
    b = pl.program_id(0)
    d = x_ref.shape[-1]

    @pl.when(pl.program_id(1) == 0)
    def _():
        st_ref[...] = s0_ref[0]

    sh, sc = mod_ref[pl.ds(b, 1), 0:d], mod_ref[pl.ds(b, 1), d:2 * d]
    h = (_rms(x_ref[0], nw_ref[...]) * (1.0 + sc) + sh).astype(BF16)
    lbl = lbl_ref[...]

    def proj(j):
        return _dot(h, w_ref[:, j * WIDTH:(j + 1) * WIDTH])

    act_ref[0, :, A_Q:A_Q + WIDTH] = _silu(proj(0))
    act_ref[0, :, A_LF:A_LF + WIDTH] = _forget(proj(1), _lower_bound(lbl, 0))[0]
    logf_b, k_b = _forget(proj(2), _lower_bound(lbl, 1))
    act_ref[0, :, A_LB:A_LB + WIDTH] = logf_b
    kb_ref[...] = k_b
    act_ref[0, :, A_V:A_V + WIDTH] = proj(3)
    act_ref[0, :, A_G:A_G + WIDTH] = _silu(proj(4))
    act_ref[0, :, A_U:A_U + WIDTH] = _gelu_tanh(proj(5))
    gat_ref[0, :, G_SV:G_SV + WIDTH] = (_group_rms(_gelu_tanh(proj(6))) * snw_ref[...]).astype(BF16)
    for j in range(2 * d // WIDTH):
        gat_ref[0, :, G_GATES + j * WIDTH:G_GATES + (j + 1) * WIDTH] = _sigmoid(proj(7 + j)).astype(BF16)

    for piece in range(x_ref.shape[1] // sub - 1, -1, -1):
        rows = slice(piece * sub, (piece + 1) * sub)
        bst_ref[0, piece] = st_ref[...]
        logf = act_ref[0, rows, A_LB:A_LB + WIDTH]
        w = (kb_ref[rows, :] * jnp.exp(_cumsum_dot(m_ref[...], logf))).astype(BF16)
        v = act_ref[0, rows, A_V:A_V + WIDTH].astype(BF16)
        tot = jnp.exp(jnp.sum(logf, axis=0, keepdims=True))
        for pr in range(N_PAIRS):
            sl = slice(pr * PAIR, (pr + 1) * PAIR)
            st_ref[pr] = st_ref[pr] * tot[:, sl] + _pair_outer(v[:, sl], w[:, sl])


def _in_proj(x, mod, norm_w, w_in, lbl, snw, s_b0, tm, sub):
    b, t, d = x.shape
    nb = t // tm
    per = tm // sub
    m = jnp.asarray(_block_decay_matrix(sub, True), BF16)
    full = lambda a: pl.BlockSpec(a.shape, lambda bi, i: (0,) * a.ndim, pipeline_mode=pl.Buffered(1))
    n_gat = G_GATES + 2 * d
    return pl.pallas_call(
        functools.partial(_in_proj_kernel, sub),
        out_shape=(jax.ShapeDtypeStruct((b, t, A_COLS), F32),
                   jax.ShapeDtypeStruct((b, t, n_gat), BF16),
                   jax.ShapeDtypeStruct((b, t // sub, N_PAIRS, PAIR, PAIR), F32)),
        grid=(b, nb),
        in_specs=[pl.BlockSpec((1, tm, d), lambda bi, i: (bi, nb - 1 - i, 0)), full(mod), full(norm_w),
                  full(w_in), full(lbl), full(snw), full(m),
                  pl.BlockSpec((1, N_PAIRS, PAIR, PAIR), lambda bi, i: (bi, 0, 0, 0))],
        out_specs=(pl.BlockSpec((1, tm, A_COLS), lambda bi, i: (bi, nb - 1 - i, 0)),
                   pl.BlockSpec((1, tm, n_gat), lambda bi, i: (bi, nb - 1 - i, 0)),
                   pl.BlockSpec((1, per, N_PAIRS, PAIR, PAIR), lambda bi, i: (bi, nb - 1 - i, 0, 0, 0))),
        scratch_shapes=[pltpu.VMEM((N_PAIRS, PAIR, PAIR), F32), pltpu.VMEM((tm, WIDTH), F32)],
        compiler_params=pltpu.CompilerParams(dimension_semantics=("arbitrary", "arbitrary"),
                                             vmem_limit_bytes=VMEM_LIMIT),
        name="in_proj",
    )(x, mod, norm_w, w_in, lbl, snw, m, s_b0)


def _level_masks(reverse):
    row = lax.broadcasted_iota(I32, (CHUNK, PAIR), 0)
    col = lax.broadcasted_iota(I32, (CHUNK, PAIR), 1) & (CHUNK - 1)
    x = row ^ col
    out = []
    for half in LEVELS:
        in_pair = jnp.where(x >= half, jnp.where(x < 2 * half, 1.0, 0.0), 0.0)
        bit = (col if reverse else row) & half
        out.append((jnp.where(bit != 0, in_pair, 0.0), (row & half) != 0))
    return out


def _hgrn_scores(q, logf, dm_ref, masks, reverse, pm_ref, dio_ref, k_ref):
    lf2 = logf * LOG2E
    k = 1.0 - jnp.exp2(lf2)
    k_ref[...] = k
    hi, lo = _split2(lf2)
    dall = _dot(dm_ref[...], jnp.concatenate([hi, lo], axis=0))
    dio_ref[...] = dall[0:2 * CHUNK]
    n_groups = CHUNK // ROW_GROUP
    for pr in range(N_PAIRS):
        sl = slice(pr * PAIR, (pr + 1) * PAIR)
        qp, kp = q[:, sl], k[:, sl]
        pm = [jnp.zeros((ROW_GROUP, PAIR), F32) for _ in range(n_groups)]
        for li, half in enumerate(LEVELS):
            mask, row_bit = masks[li]
            qside = jnp.logical_not(row_bit) if reverse else row_bit
            e = jnp.exp2(dall[(2 + li) * CHUNK:(3 + li) * CHUNK, sl])
            xm = (jnp.where(qside, qp, kp) * e).astype(BF16)
            groups = [g for g in range(n_groups)
                      if half < ROW_GROUP or (((g * ROW_GROUP) & half) != 0) != reverse]
            lhs = xm if len(groups) == n_groups else jnp.concatenate(
                [xm[g * ROW_GROUP:(g + 1) * ROW_GROUP] for g in groups], axis=0)
            gm = _dot_nt(lhs, _block_diag(xm))
            for n, g in enumerate(groups):
                pm[g] = pm[g] + (gm[n * ROW_GROUP:(n + 1) * ROW_GROUP]
                                 * mask[g * ROW_GROUP:(g + 1) * ROW_GROUP])
        pm_ref[pr] = jnp.concatenate(pm, axis=0).astype(BF16)


def _hgrn_apply(q, v, reverse, pm_ref, dio_ref, k_ref, state_ref, o_ref, r0):
    k = k_ref[...]
    e_in = jnp.exp2(dio_ref[0:CHUNK])
    e_st = jnp.exp2(dio_ref[CHUNK:2 * CHUNK])
    tot_row = 0 if reverse else CHUNK - 1
    e_tot = e_in[tot_row:tot_row + 1]
    first = lax.broadcasted_iota(I32, (CHUNK, PAIR), 1) < HD
    for pr in range(N_PAIRS):
        sl = slice(pr * PAIR, (pr + 1) * PAIR)
        qp, kp, vp = q[:, sl], k[:, sl], v[:, sl]
        qk = qp * kp
        diag = jnp.where(first, jnp.sum(qk[:, 0:HD], axis=-1, keepdims=True),
                         jnp.sum(qk[:, HD:PAIR], axis=-1, keepdims=True))
        st = state_ref[pr]
        o = (_dot(pm_ref[pr], _block_diag(vp.astype(BF16))) + diag * vp
             + _dot_nt((qp * e_in[:, sl]).astype(BF16), st.astype(BF16)))
        o_ref[pl.ds(r0, CHUNK), sl] += o
        state_ref[pr] = st * e_tot[:, sl] + _pair_outer(vp.astype(BF16), (kp * e_st[:, sl]).astype(BF16))


def _group_rms(x):
    return jnp.concatenate(
        [x[:, g * HD:(g + 1) * HD]
         * lax.rsqrt(jnp.mean(x[:, g * HD:(g + 1) * HD] ** 2, axis=-1, keepdims=True) + EPS)
         for g in range(HEADS)], axis=-1)


def _mixer_kernel(act_ref, gat_ref, x_ref, mod_ref, nffn_ref, hnw_ref,
                  sw_ref, sbias_ref, wa_ref, wb_ref, wo_ref, rw_ref, dmf_ref, dmb_ref,
                  sf0_ref, bst_ref,
                  x1_ref, h2_ref, aff_ref,
                  o_ref, sg_ref, stf_ref, stb_ref, pm_ref, dio_ref, k_ref):
    b = pl.program_id(0)
    i = pl.program_id(1)
    tb, d = x_ref.shape[1], x_ref.shape[2]
    nch = tb // CHUNK
    _, _, g_m, sh_f, sc_f, _ = _mod_rows(mod_ref, b, d)
    act = lambda rows, c0: act_ref[0, rows, c0:c0 + WIDTH]
    every = slice(None)

    @pl.when(i == 0)
    def _():
        stf_ref[...] = sf0_ref[0]

    stb_ref[...] = bst_ref[0, 0]

    o_ref[...] = jnp.zeros_like(o_ref)
    work = []
    for reverse in (False, True):
        work += [(reverse, ci) for ci in (range(nch - 1, -1, -1) if reverse else range(nch))]
    masks = {reverse: _level_masks(reverse) for reverse in (False, True)}
    for n, (reverse, ci) in enumerate(work):
        rows = pl.ds(ci * CHUNK, CHUNK)
        a_l, dm_ref = (A_LB, dmb_ref) if reverse else (A_LF, dmf_ref)
        _hgrn_scores(act(rows, A_Q), act(rows, a_l), dm_ref, masks[reverse], reverse,
                     pm_ref.at[n], dio_ref.at[n], k_ref.at[n])
    for n, (reverse, ci) in enumerate(work):
        rows = pl.ds(ci * CHUNK, CHUNK)
        st_ref = stb_ref if reverse else stf_ref
        _hgrn_apply(act(rows, A_Q), act(rows, A_V), reverse, pm_ref.at[n], dio_ref.at[n], k_ref.at[n],
                    st_ref, o_ref, ci * CHUNK)

    a_in = _group_rms(o_ref[...]) * hnw_ref[...] * act(every, A_G)
    y_a = _dot(a_in.astype(BF16), wa_ref[...])

    for ci in range(nch):
        rows = slice(ci * CHUNK, (ci + 1) * CHUNK)
        for g in range(HEADS):
            sl = slice(g * HD, (g + 1) * HD)
            sg_ref[rows, sl] = (_dot(sw_ref[g], gat_ref[0, rows, G_SV + g * HD:G_SV + (g + 1) * HD])
                                + sbias_ref[:, sl])
    y_b = _dot((act(every, A_U) * sg_ref[...]).astype(BF16), wb_ref[...])

    merged = (gat_ref[0, :, G_GATES:G_GATES + d].astype(F32) * y_a
              + gat_ref[0, :, G_GATES + d:G_GATES + 2 * d].astype(F32) * y_b)
    y = _dot(merged.astype(BF16), wo_ref[...])
    x1 = x_ref[0] + g_m * y
    x1_ref[0] = x1

    h2 = _rms(x1, nffn_ref[...]) * (1.0 + sc_f) + sh_f
    h2_hi, h2_lo = _split2(h2)
    h2_ref[0] = h2_hi
    rw_hi, rw_lo = _split2(rw_ref[...])
    logits = _dot_nt(rw_hi, h2_hi) + _dot_nt(rw_hi, h2_lo) + _dot_nt(rw_lo, h2_hi)
    mx = jnp.max(logits, axis=0, keepdims=True)
    ex = jnp.exp(logits - mx)
    aff_ref[0] = ex / jnp.sum(ex, axis=0, keepdims=True)


def _mixer(act, gat, x, mod, nffn, hnw, sgu_w, sgu_bias, w_a, w_b, w_o, rw_t, s_f0, bstates, tb):
    b, t, d = x.shape
    nb = t // tb
    per_state = bstates.shape[1] // nb
    ne = rw_t.shape[0]
    dmf = jnp.asarray(np.tile(_decay_matrices(False), (1, 2)), BF16)
    dmb = jnp.asarray(np.tile(_decay_matrices(True), (1, 2)), BF16)
    const = lambda a: pl.BlockSpec(a.shape, lambda bi, i: (0,) * a.ndim, pipeline_mode=pl.Buffered(1))
    in_specs = [pl.BlockSpec((1, tb, act.shape[2]), lambda bi, i: (bi, i, 0)),
                pl.BlockSpec((1, tb, gat.shape[2]), lambda bi, i: (bi, i, 0)),
                pl.BlockSpec((1, tb, d), lambda bi, i: (bi, i, 0)),
                const(mod), const(nffn), const(hnw),
                const(sgu_w), const(sgu_bias), const(w_a), const(w_b), const(w_o), const(rw_t),
                const(dmf), const(dmb),
                pl.BlockSpec((1, N_PAIRS, PAIR, PAIR), lambda bi, i: (bi, 0, 0, 0)),
                pl.BlockSpec((1, 1, N_PAIRS, PAIR, PAIR), lambda bi, i: (bi, (i + 1) * per_state - 1, 0, 0, 0))]
    out_shape = (jax.ShapeDtypeStruct((b, t, d), F32),
                 jax.ShapeDtypeStruct((b, t, d), BF16),
                 jax.ShapeDtypeStruct((b, ne, t), F32))
    out_specs = (pl.BlockSpec((1, tb, d), lambda bi, i: (bi, i, 0)),
                 pl.BlockSpec((1, tb, d), lambda bi, i: (bi, i, 0)),
                 pl.BlockSpec((1, ne, tb), lambda bi, i: (bi, 0, i)))
    return pl.pallas_call(
        _mixer_kernel,
        out_shape=out_shape,
        grid=(b, nb),
        in_specs=in_specs,
        out_specs=out_specs,
        scratch_shapes=[pltpu.VMEM((tb, WIDTH), F32),
                        pltpu.VMEM((tb, WIDTH), F32),
                        pltpu.VMEM((N_PAIRS, PAIR, PAIR), F32),
                        pltpu.VMEM((N_PAIRS, PAIR, PAIR), F32),
                        pltpu.VMEM((2 * (tb // CHUNK), N_PAIRS, CHUNK, PAIR), BF16),
                        pltpu.VMEM((2 * (tb // CHUNK), 2 * CHUNK, WIDTH), F32),
                        pltpu.VMEM((2 * (tb // CHUNK), CHUNK, WIDTH), F32)],
        compiler_params=pltpu.CompilerParams(dimension_semantics=("arbitrary", "arbitrary"),
                                             vmem_limit_bytes=VMEM_LIMIT),
        name="mixer",
    )(act, gat, x, mod, nffn, hnw, sgu_w, sgu_bias, w_a, w_b, w_o, rw_t, dmf, dmb, s_f0, bstates)


def _front(x, c, ctx, c_ctx, ada_w, ada_b, norm_mix_w, norm_ffn_w, w_in, hgrn_lb_logits,
           hgrn_norm_w, sgu_norm_w, sgu_w, sgu_b, w_branch_a, w_branch_b, w_out, router_w, tb):
    b, t, d = x.shape
    layer = 0
    cond = jnp.zeros((8, d), F32).at[0:b].set(c).at[b].set(c_ctx)
    mod = _adaln(cond, ada_w[layer], ada_b[layer])
    lbl = hgrn_lb_logits[:, layer:layer + 2, :].reshape(4, WIDTH)
    nmix = norm_mix_w[layer].reshape(1, d)
    nffn = norm_ffn_w[layer].reshape(1, d)
    w_in_b = w_in[layer].astype(BF16)
    s_f0, s_b0 = _ctx_states(ctx, mod, nmix, w_in_b[:, WIDTH:4 * WIDTH], lbl, b)
    act, gat, bstates = _in_proj(x, mod, nmix, w_in_b, lbl, sgu_norm_w[layer].reshape(1, WIDTH), s_b0,
                                 min(t, 512), 256)
    sgu_bias = jnp.repeat(sgu_b[layer].T, HD, axis=1)
    return _mixer(act, gat, x, mod, nffn, hgrn_norm_w[layer].reshape(1, WIDTH),
                  sgu_w[layer].astype(BF16), sgu_bias,
                  w_branch_a[layer].astype(BF16), w_branch_b[layer].astype(BF16),
                  w_out[layer].astype(BF16), router_w[layer].T, s_f0, bstates, tb), mod


def _route_kernel(cap, aff_ref, pos_ref, gate_ref, starts_ref):
    a = aff_ref[0]
    ne, t = a.shape
    nblk = t // LANES
    bits = pltpu.bitcast(a, I32)

    def search(it, lo):
        cand = lo | (jnp.int32(1) << (30 - it))
        cnt = jnp.sum(jnp.where(bits >= cand, 1.0, 0.0), axis=-1, keepdims=True)
        return jnp.where(cnt >= cap, cand, lo)

    thr = lax.fori_loop(0, 31, search, jnp.zeros((ne, 1), I32))
    gt = bits > thr
    eq = bits == thr
    n_ties_wanted = cap - jnp.sum(jnp.where(gt, 1.0, 0.0), axis=-1, keepdims=True)

    row = lax.broadcasted_iota(I32, (LANES, LANES), 0)
    col = lax.broadcasted_iota(I32, (LANES, LANES), 1)
    upper = jnp.where(row <= col, 1.0, 0.0).astype(BF16)
    lane = lax.broadcasted_iota(I32, (ne, LANES), 1)

    off = jnp.zeros((ne, 1), F32)
    sel_blocks = []
    for j in range(nblk):
        sl = slice(j * LANES, (j + 1) * LANES)
        eqf = jnp.where(eq[:, sl], 1.0, 0.0)
        incl = _dot(eqf.astype(BF16), upper) + off
        keep_tie = jnp.where(incl - eqf < n_ties_wanted, eqf, 0.0)
        sel_blocks.append(jnp.where(gt[:, sl], 1.0, keep_tie))
        off = incl[:, LANES - 1:LANES]

    off = jnp.zeros((ne, 1), F32)
    starts = jnp.zeros((ne, LANES), F32)
    for j in range(nblk):
        sl = slice(j * LANES, (j + 1) * LANES)
        self = sel_blocks[j]
        starts = jnp.where(lane == j, off, starts)
        incl = _dot(self.astype(BF16), upper) + off
        pos_ref[0, :, sl] = jnp.where(self > 0.0, incl - 1.0, -1.0).astype(I32)
        for k, piece in enumerate(_split3(jnp.where(self > 0.0, a[:, sl], 0.0))):
            gate_ref[0, k * ne:(k + 1) * ne, sl] = piece.astype(F32)
        off = incl[:, LANES - 1:LANES]
    starts = jnp.where(lane >= nblk, off, starts)
    starts_ref[0] = starts.astype(I32)


def _route(aff_t, cap):
    b, ne, t = aff_t.shape
    spec = pl.BlockSpec((1, ne, t), lambda i: (i, 0, 0))
    return pl.pallas_call(
        functools.partial(_route_kernel, cap),
        out_shape=(jax.ShapeDtypeStruct((b, ne, t), I32),
                   jax.ShapeDtypeStruct((b, 3 * ne, t), F32),
                   jax.ShapeDtypeStruct((b, ne, LANES), I32)),
        grid=(b,),
        in_specs=[spec],
        out_specs=(spec, pl.BlockSpec((1, 3 * ne, t), lambda i: (i, 0, 0)),
                   pl.BlockSpec((1, ne, LANES), lambda i: (i, 0, 0))),
        compiler_params=pltpu.CompilerParams(dimension_semantics=("arbitrary",),
                                             vmem_limit_bytes=VMEM_LIMIT),
        name="route",
    )(aff_t)


TOK_BLOCK = 2 * LANES


SLOT_ALIGN = 16
AUX = LANES


def _slot_windows(starts_ref, b, e0, n_exp, ne, j, win):
    per = TOK_BLOCK // LANES
    s_lo, s_hi, a0 = [], [], []
    n_pass = jnp.int32(0)
    for le in range(n_exp):
        base = (b * ne + e0 + le) * LANES
        lo = starts_ref[base + per * j]
        hi = starts_ref[base + per * (j + 1)]
        a = (lo // SLOT_ALIGN) * SLOT_ALIGN
        n_pass = jnp.maximum(n_pass, jnp.where(hi > lo, (hi - a + win - 1) // win, 0))
        s_lo.append(lo)
        s_hi.append(hi)
        a0.append(a)
    return s_lo, s_hi, a0, n_pass


def _window(s_lo, s_hi, a0, p, win, cap):
    a = a0 + p * win
    a_c = pl.multiple_of(jnp.minimum(a, cap - win), SLOT_ALIGN)
    return a_c, jnp.maximum(s_lo, a), jnp.minimum(s_hi, a + win)


def _gather_kernel(win, starts_ref, pos_ref, gate_ref, h2_ref, xs_ref):
    b = pl.program_id(0)
    g = pl.program_id(1)
    j = pl.program_id(2)
    eg, cap = xs_ref.shape[1], xs_ref.shape[2]
    ne = pl.num_programs(1) * eg
    d = h2_ref.shape[2]

    @pl.when(j == 0)
    def _():
        xs_ref[...] = jnp.zeros_like(xs_ref)

    gcols = gate_ref[0]

    s_lo, s_hi, a0, n_pass = _slot_windows(starts_ref, b, g * eg, eg, ne, j, win)
    row = lax.broadcasted_iota(I32, (win, TOK_BLOCK), 0)
    row_d = lax.broadcasted_iota(I32, (win, d), 0)
    row_aux = lax.broadcasted_iota(I32, (win, AUX), 0)

    def one_pass(p, carry):
        wins = [_window(s_lo[le], s_hi[le], a0[le], p, win, cap) for le in range(eg)]
        onehots = []
        for le, (a_c, lo_i, hi_i) in enumerate(wins):
            prow = pos_ref[0, le, pl.ds(j, 1), :]
            owned = jnp.where(prow >= lo_i, jnp.where(prow < hi_i, prow, -1), -1)
            onehots.append(jnp.where(owned == row + a_c, 1.0, 0.0).astype(BF16))
        lhs = jnp.concatenate(onehots, axis=0)
        res = _dot(lhs, h2_ref[0])
        res_aux = _dot(lhs, gcols)
        for le, (a_c, lo_i, hi_i) in enumerate(wins):
            rows = pl.ds(a_c, win)
            mine = slice(le * win, (le + 1) * win)
            xs_ref[0, le, rows, 0:d] = jnp.where(row_d + a_c < lo_i, xs_ref[0, le, rows, 0:d],
                                                 res[mine].astype(BF16))
            xs_ref[0, le, rows, d:d + AUX] = jnp.where(row_aux + a_c < lo_i, xs_ref[0, le, rows, d:d + AUX],
                                                       res_aux[mine].astype(BF16))
        return carry

    lax.fori_loop(0, n_pass, one_pass, 0)


def _gather(starts, pos, gate3, h2, cap, win, eg):
    b, t, d = h2.shape
    ne = pos.shape[1]
    ntb = t // TOK_BLOCK
    pos4 = pos.reshape(b, ne, ntb, TOK_BLOCK)
    gate_t = jnp.pad(jnp.swapaxes(gate3, 1, 2).astype(BF16), ((0, 0), (0, 0), (0, AUX - gate3.shape[1])))
    grid_spec = pltpu.PrefetchScalarGridSpec(
        num_scalar_prefetch=1,
        grid=(b, ne // eg, ntb),
        in_specs=[pl.BlockSpec((1, eg, ntb, TOK_BLOCK), lambda bi, g, j, s: (bi, g, 0, 0)),
                  pl.BlockSpec((1, TOK_BLOCK, AUX), lambda bi, g, j, s: (bi, j, 0)),
                  pl.BlockSpec((1, TOK_BLOCK, d), lambda bi, g, j, s: (bi, j, 0))],
        out_specs=pl.BlockSpec((1, eg, cap, d + AUX), lambda bi, g, j, s: (bi, g, 0, 0)))
    return pl.pallas_call(
        functools.partial(_gather_kernel, win),
        out_shape=jax.ShapeDtypeStruct((b, ne, cap, d + AUX), BF16),
        grid_spec=grid_spec,
        compiler_params=pltpu.CompilerParams(
            dimension_semantics=("arbitrary", "arbitrary", "arbitrary"), vmem_limit_bytes=VMEM_LIMIT),
        name="gather",
    )(starts.reshape(-1), pos4, gate_t, h2)


FF_SUB = 512


def _ffn_kernel(xs_ref, wg_ref, wu_ref, wd_ref, gf_ref, ye_ref, acc_ref):
    bg = pl.program_id(0)
    e = pl.program_id(1)
    f = pl.program_id(2)
    ne = pl.num_programs(1)
    ns, cap, d = ye_ref.shape[0], ye_ref.shape[2], ye_ref.shape[3]
    xs = xs_ref[:, 0, :, 0:d].reshape(ns * cap, d)
    tf = wg_ref.shape[2]

    @pl.when(f == 0)
    def _():
        acc_ref[...] = jnp.zeros_like(acc_ref)

    acc = acc_ref[...]
    for c0 in range(0, tf, FF_SUB):
        a = _dot(xs, wg_ref[0, :, c0:c0 + FF_SUB].astype(BF16))
        u = _dot(xs, wu_ref[0, :, c0:c0 + FF_SUB].astype(BF16))
        acc = acc + _dot((_silu(a) * u).astype(BF16), wd_ref[0, c0:c0 + FF_SUB, :].astype(BF16))
    acc_ref[...] = acc
    for s in range(ns):
        aux = xs_ref[s, 0, :, d:d + AUX].astype(F32)
        lane = lax.broadcasted_iota(I32, aux.shape, 1)
        mine = jnp.where(lane < 3 * ne, jnp.where(lane % ne == e, aux, 0.0), 0.0)
        gate = jnp.sum(mine, axis=-1, keepdims=True)
        ye_ref[s, 0] = (acc[s * cap:(s + 1) * cap] * gate * gf_ref[pl.ds(bg * ns + s, 1), :]).astype(BF16)


def _ffn(xs, w_gate, w_up, w_down, mod, tf, ns):
    b, ne, cap, daux = xs.shape
    d = daux - AUX
    ff = w_gate.shape[2]
    return pl.pallas_call(
        _ffn_kernel,
        out_shape=jax.ShapeDtypeStruct((b, ne, cap, d), BF16),
        grid=(b // ns, ne, ff // tf),
        in_specs=[pl.BlockSpec((ns, 1, cap, daux), lambda bi, e, f: (bi, e, 0, 0)),
                  pl.BlockSpec((1, d, tf), lambda bi, e, f: (e, 0, f)),
                  pl.BlockSpec((1, d, tf), lambda bi, e, f: (e, 0, f)),
                  pl.BlockSpec((1, tf, d), lambda bi, e, f: (e, f, 0)),
                  pl.BlockSpec((mod.shape[0], d), lambda bi, e, f: (0, N_MOD - 1))],
        out_specs=pl.BlockSpec((ns, 1, cap, d), lambda bi, e, f: (bi, e, 0, 0)),
        scratch_shapes=[pltpu.VMEM((ns * cap, d), F32)],
        compiler_params=pltpu.CompilerParams(dimension_semantics=("arbitrary", "arbitrary", "arbitrary"),
                                             vmem_limit_bytes=VMEM_LIMIT),
        name="ffn",
    )(xs, w_gate, w_up, w_down, mod)


def _combine_kernel(win, starts_ref, post_ref, ye_ref, x1_ref, fw_ref, out_ref, acc_ref, stack_ref):
    b = pl.program_id(0)
    j = pl.program_id(1)
    ne, cap = ye_ref.shape[1], ye_ref.shape[2]
    acc_ref[...] = x1_ref[0]
    s_lo, s_hi, a0, n_pass = _slot_windows(starts_ref, b, 0, ne, ne, j, win)
    row = lax.broadcasted_iota(I32, (win, TOK_BLOCK), 0)

    def one_pass(p, carry):
        onehots = []
        for e in range(ne):
            a_c, lo_i, hi_i = _window(s_lo[e], s_hi[e], a0[e], p, win, cap)
            prow = post_ref[0, e, pl.ds(j, 1), :]
            owned = jnp.where(prow >= lo_i, jnp.where(prow < hi_i, prow, -1), -1)
            onehots.append(jnp.where(owned == row + a_c, 1.0, 0.0).astype(BF16))
            stack_ref[e * win:(e + 1) * win, :] = ye_ref[0, e, pl.ds(a_c, win), :]
        acc_ref[...] += _dot_tn(jnp.concatenate(onehots, axis=0), stack_ref[...])
        return carry

    lax.fori_loop(0, n_pass, one_pass, 0)
    out_ref[0] = _rms(acc_ref[...], fw_ref[...])


def _combine(starts, pos, ye, x1, final_w, win):
    b, t, d = x1.shape
    ne, cap = ye.shape[1], ye.shape[2]
    ntb = t // TOK_BLOCK
    pos4 = pos.reshape(b, ne, ntb, TOK_BLOCK)
    grid_spec = pltpu.PrefetchScalarGridSpec(
        num_scalar_prefetch=1,
        grid=(b, ntb),
        in_specs=[pl.BlockSpec((1, ne, ntb, TOK_BLOCK), lambda bi, j, s: (bi, 0, 0, 0)),
                  pl.BlockSpec((1, ne, cap, d), lambda bi, j, s: (bi, 0, 0, 0),
                               pipeline_mode=pl.Buffered(1)),
                  pl.BlockSpec((1, TOK_BLOCK, d), lambda bi, j, s: (bi, j, 0)),
                  pl.BlockSpec((1, d), lambda bi, j, s: (0, 0))],
        out_specs=pl.BlockSpec((1, TOK_BLOCK, d), lambda bi, j, s: (bi, j, 0)),
        scratch_shapes=[pltpu.VMEM((TOK_BLOCK, d), F32), pltpu.VMEM((ne * win, d), BF16)])
    return pl.pallas_call(
        functools.partial(_combine_kernel, win),
        out_shape=jax.ShapeDtypeStruct((b, t, d), F32),
        grid_spec=grid_spec,
        compiler_params=pltpu.CompilerParams(
            dimension_semantics=("arbitrary", "arbitrary"), vmem_limit_bytes=VMEM_LIMIT),
        name="combine",
    )(starts.reshape(-1), pos4, ye, x1, final_w.reshape(1, d))


def kernel(x, c, ctx, c_ctx, ada_w, ada_b, norm_mix_w, norm_ffn_w, w_in, hgrn_lb_logits, hgrn_norm_w,
           sgu_norm_w, sgu_w, sgu_b, w_branch_a, w_branch_b, w_out, router_w, expert_w_gate,
           expert_w_up, expert_w_down, final_norm_w):
    b, t, d = x.shape
    assert b + 1 <= 8 and t % TOK_BLOCK == 0 and d % LANES == 0
    ne = router_w.shape[-1]
    cap = CAPACITY_FACTOR * t // ne
    win = min(cap, 64)
    assert cap % SLOT_ALIGN == 0 and t // LANES < LANES and 3 * ne <= AUX
    (x1, h2, aff_t), mod = _front(x, c, ctx, c_ctx, ada_w, ada_b, norm_mix_w, norm_ffn_w, w_in,
                                  hgrn_lb_logits, hgrn_norm_w, sgu_norm_w, sgu_w, sgu_b,
                                  w_branch_a, w_branch_b, w_out, router_w, 512)
    pos, gate, starts = _route(aff_t, cap)
    xs = _gather(starts, pos, gate, h2, cap, win, 8)
    ye = _ffn(xs, expert_w_gate[0], expert_w_up[0], expert_w_down[0], mod, 1024, 1)
    return _combine(starts, pos, ye, x1, final_norm_w, win)
```

```python
import functools

import numpy as np
import jax
import jax.numpy as jnp
from jax import lax
from jax.experimental import pallas as pl
from jax.experimental.pallas import tpu as pltpu

F32 = jnp.float32
BF16 = jnp.bfloat16
I32 = jnp.int32

LANES = 128
EPS = 1e-6
N_MOD = 6
HEADS = 4
HD = 128
WIDTH = HEADS * HD
CHUNK = 128
N_EXPERTS = 16
CAPACITY_FACTOR = 2
LEVELS = tuple(CHUNK >> (i + 1) for i in range(7))
N_DMAT = 1 + len(LEVELS) - 1
PAIR = 2 * HD
N_PAIRS = HEADS // 2
ROW_GROUP = 16
LOG2E = 1.4426950408889634
VMEM_LIMIT = 52 * 1024 * 1024


def _dot(a, b):
    return jnp.dot(a, b, preferred_element_type=F32)


def _dot_nt(a, b):
    return lax.dot_general(a, b, (((1,), (1,)), ((), ())), preferred_element_type=F32)


def _dot_tn(a, b):
    return lax.dot_general(a, b, (((0,), (0,)), ((), ())), preferred_element_type=F32)


def _split2(x):
    hi = x.astype(BF16)
    lo = (x - hi.astype(F32)).astype(BF16)
    return hi, lo


def _split3(x):
    hi = x.astype(BF16)
    r = x - hi.astype(F32)
    mid = r.astype(BF16)
    lo = (r - mid.astype(F32)).astype(BF16)
    return hi, mid, lo


def _sigmoid(x):
    return 1.0 / (1.0 + jnp.exp(-x))


def _silu(x):
    return x * _sigmoid(x)


def _gelu_tanh(x):
    c = np.sqrt(2.0 / np.pi).astype(np.float32)
    return 0.5 * x * (1.0 + jnp.tanh(c * (x + 0.044715 * (x * x * x))))


def _rms(x, w):
    return x * lax.rsqrt(jnp.mean(x * x, axis=-1, keepdims=True) + EPS) * w


def _lower_bound(lbl, d):
    l0 = lbl[2 * d:2 * d + 1, :]
    l1 = lbl[2 * d + 1:2 * d + 2, :]
    m = jnp.maximum(l0, l1)
    e0 = jnp.exp(l0 - m)
    e1 = jnp.exp(l1 - m)
    return e0 / (e0 + e1)


def _forget(z, lb):
    f = lb + (1.0 - lb) * _sigmoid(z)
    return jnp.log(f), 1.0 - f


def _cumsum_dot(m_bf16, x):
    hi, mid, lo = _split3(x)
    return _dot(m_bf16, hi) + _dot(m_bf16, mid) + _dot(m_bf16, lo)


def _same_head(shape):
    r = lax.broadcasted_iota(I32, shape, 0) < HD
    c = lax.broadcasted_iota(I32, shape, 1) < HD
    return r == c


def _block_diag(x):
    first = lax.broadcasted_iota(I32, x.shape, 1) < HD
    zero = jnp.zeros_like(x)
    return jnp.concatenate([jnp.where(first, x, zero), jnp.where(first, zero, x)], axis=0)


def _pair_outer(v_pair, w_pair):
    full = _dot_tn(v_pair, w_pair)
    return jnp.where(_same_head(full.shape), full, 0.0)


def _decay_matrices(reverse):
    c = CHUNK
    i = np.arange(c)[:, None]
    m = np.arange(c)[None, :]
    mats = []
    mats.append(m >= i if reverse else m <= i)
    for half in LEVELS[:-1]:
        a = (i // (2 * half)) * (2 * half)
        mid = a + half
        if not reverse:
            qside = i >= mid
            mat = np.where(qside, (m >= mid) & (m <= i), (m > i) & (m < mid))
        else:
            qside = i < mid
            mat = np.where(qside, (m >= i) & (m < mid), (m >= mid) & (m < i))
        mats.append(mat)
    return np.concatenate(mats, axis=0).astype(np.float32)


def _block_decay_matrix(n, reverse):
    i = np.arange(n)[:, None]
    m = np.arange(n)[None, :]
    return ((m < i) if reverse else (m > i)).astype(np.float32)


def _adaln_kernel(cond_ref, w_ref, b_ref, out_ref):
    s = _silu(cond_ref[...])
    s_hi, s_lo = _split2(s)
    w_hi, w_lo = _split2(w_ref[...])
    out_ref[...] = _dot(s_hi, w_hi) + _dot(s_hi, w_lo) + _dot(s_lo, w_hi) + b_ref[...]


def _adaln(cond, ada_w, ada_b):
    rows, d = cond.shape
    n = ada_w.shape[1]
    tn = 1024
    return pl.pallas_call(
        _adaln_kernel,
        out_shape=jax.ShapeDtypeStruct((rows, n), F32),
        grid=(n // tn,),
        in_specs=[pl.BlockSpec((rows, d), lambda j: (0, 0)),
                  pl.BlockSpec((d, tn), lambda j: (0, j)),
                  pl.BlockSpec((1, tn), lambda j: (0, j))],
        out_specs=pl.BlockSpec((rows, tn), lambda j: (0, j)),
        compiler_params=pltpu.CompilerParams(dimension_semantics=("arbitrary",),
                                             vmem_limit_bytes=VMEM_LIMIT),
        name="adaln",
    )(cond, ada_w, ada_b.reshape(1, n))


def _mod_rows(mod_ref, row, d):
    return [mod_ref[pl.ds(row, 1), j * d:(j + 1) * d] for j in range(N_MOD)]


def _ctx_kernel(ctx_row, x_ref, mod_ref, nw_ref, w_ref, lbl_ref, mf_ref, mb_ref, sf_ref, sb_ref):
    d = x_ref.shape[-1]
    x = x_ref[0]
    sh, sc = mod_ref[pl.ds(ctx_row, 1), 0:d], mod_ref[pl.ds(ctx_row, 1), d:2 * d]
    h = _rms(x, nw_ref[...]) * (1.0 + sc) + sh
    p = _dot(h.astype(BF16), w_ref[...])
    v = p[:, 2 * WIDTH:3 * WIDTH].astype(BF16)
    lbl = lbl_ref[...]
    for dirn, (m_ref, out_ref) in enumerate(((mf_ref, sf_ref), (mb_ref, sb_ref))):
        logf, k = _forget(p[:, dirn * WIDTH:(dirn + 1) * WIDTH], _lower_bound(lbl, dirn))
        w = (k * jnp.exp(_cumsum_dot(m_ref[...], logf))).astype(BF16)
        for pr in range(N_PAIRS):
            sl = slice(pr * PAIR, (pr + 1) * PAIR)
            out_ref[0, pr] = _pair_outer(v[:, sl], w[:, sl])


def _ctx_states(ctx, mod, norm_w, w_c, lbl, ctx_row):
    b, l, d = ctx.shape
    mf = jnp.asarray(_block_decay_matrix(l, False), BF16)
    mb = jnp.asarray(_block_decay_matrix(l, True), BF16)
    full = lambda a: pl.BlockSpec(a.shape, lambda i: (0,) * a.ndim)
    st = jax.ShapeDtypeStruct((b, N_PAIRS, PAIR, PAIR), F32)
    st_spec = pl.BlockSpec((1, N_PAIRS, PAIR, PAIR), lambda i: (i, 0, 0, 0))
    return pl.pallas_call(
        functools.partial(_ctx_kernel, ctx_row),
        out_shape=(st, st),
        grid=(b,),
        in_specs=[pl.BlockSpec((1, l, d), lambda i: (i, 0, 0)), full(mod), full(norm_w),
                  full(w_c), full(lbl), full(mf), full(mb)],
        out_specs=(st_spec, st_spec),
        compiler_params=pltpu.CompilerParams(dimension_semantics=("arbitrary",),
                                             vmem_limit_bytes=VMEM_LIMIT),
        name="ctx_state",
    )(ctx, mod, norm_w, w_c, lbl, mf, mb)


A_Q, A_LF, A_LB, A_V, A_G, A_U = (j * WIDTH for j in range(6))
A_COLS = 6 * WIDTH
G_SV, G_GATES = 0, WIDTH
PROJ_PIECE = 256


def _in_proj_kernel(sub, x_ref, mod_ref, nw_ref, w_ref, lbl_ref, snw_ref, m_ref, s0_ref,
                    act_ref, gat_ref, bst_ref, st_ref, kb_ref, h_ref):
    b = pl.program_id(0)
    d = x_ref.shape[-1]

    @pl.when(pl.program_id(1) == 0)
    def _():
        st_ref[...] = s0_ref[0]

    sh, sc = mod_ref[pl.ds(b, 1), 0:d], mod_ref[pl.ds(b, 1), d:2 * d]
    h_ref[...] = (_rms(x_ref[0], nw_ref[...]) * (1.0 + sc) + sh).astype(BF16)
    lbl = lbl_ref[...]

    def pieces(j):
        for c in range(0, WIDTH, PROJ_PIECE):
            yield c, _dot(h_ref[...], w_ref[:, j * WIDTH + c:j * WIDTH + c + PROJ_PIECE])

    lb_f, lb_b = _lower_bound(lbl, 0), _lower_bound(lbl, 1)
    for c, p in pieces(0):
        act_ref[0, :, A_Q + c:A_Q + c + PROJ_PIECE] = _silu(p)
    for c, p in pieces(1):
        act_ref[0, :, A_LF + c:A_LF + c + PROJ_PIECE] = _forget(p, lb_f[:, c:c + PROJ_PIECE])[0]
    for c, p in pieces(2):
        logf_b, k_b = _forget(p, lb_b[:, c:c + PROJ_PIECE])
        act_ref[0, :, A_LB + c:A_LB + c + PROJ_PIECE] = logf_b
        kb_ref[:, c:c + PROJ_PIECE] = k_b
    for c, p in pieces(3):
        act_ref[0, :, A_V + c:A_V + c + PROJ_PIECE] = p
    for c, p in pieces(4):
        act_ref[0, :, A_G + c:A_G + c + PROJ_PIECE] = _silu(p)
    for c, p in pieces(5):
        act_ref[0, :, A_U + c:A_U + c + PROJ_PIECE] = _gelu_tanh(p)
    for c, p in pieces(6):
        gat_ref[0, :, G_SV + c:G_SV + c + PROJ_PIECE] = (
            _group_rms(_gelu_tanh(p)) * snw_ref[:, c:c + PROJ_PIECE]).astype(BF16)
    for j in range(2 * d // WIDTH):
        for c, p in pieces(7 + j):
            gat_ref[0, :, G_GATES + j * WIDTH + c:G_GATES + j * WIDTH + c + PROJ_PIECE] = _sigmoid(p).astype(BF16)

    for piece in range(x_ref.shape[1] // sub - 1, -1, -1):
        rows = slice(piece * sub, (piece + 1) * sub)
        bst_ref[0, piece] = st_ref[...]
        logf = act_ref[0, rows, A_LB:A_LB + WIDTH]
        w = (kb_ref[rows, :] * jnp.exp(_cumsum_dot(m_ref[...], logf))).astype(BF16)
        v = act_ref[0, rows, A_V:A_V + WIDTH].astype(BF16)
        tot = jnp.exp(jnp.sum(logf, axis=0, keepdims=True))
        for pr in range(N_PAIRS):
            sl = slice(pr * PAIR, (pr + 1) * PAIR)
            st_ref[pr] = st_ref[pr] * tot[:, sl] + _pair_outer(v[:, sl], w[:, sl])


def _in_proj(x, mod, norm_w, w_in, lbl, snw, s_b0, tm, sub):
    b, t, d = x.shape
    nb = t // tm
    per = tm // sub
    m = jnp.asarray(_block_decay_matrix(sub, True), BF16)
    full = lambda a: pl.BlockSpec(a.shape, lambda bi, i: (0,) * a.ndim, pipeline_mode=pl.Buffered(1))
    n_gat = G_GATES + 2 * d
    return pl.pallas_call(
        functools.partial(_in_proj_kernel, sub),
        out_shape=(jax.ShapeDtypeStruct((b, t, A_COLS), F32),
                   jax.ShapeDtypeStruct((b, t, n_gat), BF16),
                   jax.ShapeDtypeStruct((b, t // sub, N_PAIRS, PAIR, PAIR), F32)),
        grid=(b, nb),
        in_specs=[pl.BlockSpec((1, tm, d), lambda bi, i: (bi, nb - 1 - i, 0)), full(mod), full(norm_w),
                  full(w_in), full(lbl), full(snw), full(m),
                  pl.BlockSpec((1, N_PAIRS, PAIR, PAIR), lambda bi, i: (bi, 0, 0, 0))],
        out_specs=(pl.BlockSpec((1, tm, A_COLS), lambda bi, i: (bi, nb - 1 - i, 0)),
                   pl.BlockSpec((1, tm, n_gat), lambda bi, i: (bi, nb - 1 - i, 0)),
                   pl.BlockSpec((1, per, N_PAIRS, PAIR, PAIR), lambda bi, i: (bi, nb - 1 - i, 0, 0, 0))),
        scratch_shapes=[pltpu.VMEM((N_PAIRS, PAIR, PAIR), F32), pltpu.VMEM((tm, WIDTH), F32),
                        pltpu.VMEM((tm, d), BF16)],
        compiler_params=pltpu.CompilerParams(dimension_semantics=("arbitrary", "arbitrary"),
                                             vmem_limit_bytes=VMEM_LIMIT),
        name="in_proj",
    )(x, mod, norm_w, w_in, lbl, snw, m, s_b0)


def _level_masks(reverse):
    row = lax.broadcasted_iota(I32, (CHUNK, PAIR), 0)
    col = lax.broadcasted_iota(I32, (CHUNK, PAIR), 1) & (CHUNK - 1)
    x = row ^ col
    out = []
    for half in LEVELS:
        in_pair = jnp.where(x >= half, jnp.where(x < 2 * half, 1.0, 0.0), 0.0)
        bit = (col if reverse else row) & half
        out.append((jnp.where(bit != 0, in_pair, 0.0), (row & half) != 0))
    return out


def _hgrn_scores(q, logf, dm_ref, masks, reverse, pm_ref, dio_ref, k_ref):
    lf2 = logf * LOG2E
    f = jnp.exp2(lf2)
    k = 1.0 - f
    k_ref[...] = k
    hi, lo = _split2(lf2)
    dall = _dot(dm_ref[...], jnp.concatenate([hi, lo], axis=0))
    dio_ref[...] = dall[0:CHUNK]
    n_groups = CHUNK // ROW_GROUP
    for pr in range(N_PAIRS):
        sl = slice(pr * PAIR, (pr + 1) * PAIR)
        qp, kp = q[:, sl], k[:, sl]
        pm = [jnp.zeros((ROW_GROUP, PAIR), F32) for _ in range(n_groups)]
        for li, half in enumerate(LEVELS):
            mask, row_bit = masks[li]
            qside = jnp.logical_not(row_bit) if reverse else row_bit
            if half > 1:
                e = jnp.exp2(dall[(1 + li) * CHUNK:(2 + li) * CHUNK, sl])
                xm = (jnp.where(qside, qp, kp) * e).astype(BF16)
            else:
                xm = jnp.where(qside, qp * f[:, sl], kp).astype(BF16)
            groups = [g for g in range(n_groups)
                      if half < ROW_GROUP or (((g * ROW_GROUP) & half) != 0) != reverse]
            lhs = xm if len(groups) == n_groups else jnp.concatenate(
                [xm[g * ROW_GROUP:(g + 1) * ROW_GROUP] for g in groups], axis=0)
            gm = _dot_nt(lhs, _block_diag(xm))
            for n, g in enumerate(groups):
                pm[g] = pm[g] + (gm[n * ROW_GROUP:(n + 1) * ROW_GROUP]
                                 * mask[g * ROW_GROUP:(g + 1) * ROW_GROUP])
        pm_ref[pr] = jnp.concatenate(pm, axis=0).astype(BF16)


def _hgrn_apply(q, v, reverse, pm_ref, dio_ref, k_ref, state_ref, o_ref, r0):
    k = k_ref[...]
    d_in = dio_ref[...]
    tot_row = 0 if reverse else CHUNK - 1
    d_tot = d_in[tot_row:tot_row + 1]
    e_in = jnp.exp2(d_in)
    e_st = jnp.exp2(d_tot - d_in)
    e_tot = e_in[tot_row:tot_row + 1]
    first = lax.broadcasted_iota(I32, (CHUNK, PAIR), 1) < HD
    for pr in range(N_PAIRS):
        sl = slice(pr * PAIR, (pr + 1) * PAIR)
        qp, kp, vp = q[:, sl], k[:, sl], v[:, sl]
        qk = qp * kp
        diag = jnp.where(first, jnp.sum(qk[:, 0:HD], axis=-1, keepdims=True),
                         jnp.sum(qk[:, HD:PAIR], axis=-1, keepdims=True))
        st = state_ref[pr]
        o = (_dot(pm_ref[pr], _block_diag(vp.astype(BF16))) + diag * vp
             + _dot_nt((qp * e_in[:, sl]).astype(BF16), st.astype(BF16)))
        o_ref[pl.ds(r0, CHUNK), sl] += o
        state_ref[pr] = st * e_tot[:, sl] + _pair_outer(vp.astype(BF16), (kp * e_st[:, sl]).astype(BF16))


def _group_rms(x):
    return jnp.concatenate(
        [x[:, g * HD:(g + 1) * HD]
         * lax.rsqrt(jnp.mean(x[:, g * HD:(g + 1) * HD] ** 2, axis=-1, keepdims=True) + EPS)
         for g in range(x.shape[1] // HD)], axis=-1)


def _mixer_kernel(act_ref, gat_ref, x_ref, mod_ref, nffn_ref, hnw_ref,
                  sw_ref, sbias_ref, wa_ref, wb_ref, wo_ref, rw_ref, dmf_ref, dmb_ref,
                  sf0_ref, bst_ref,
                  x1_ref, h2_ref, aff_ref,
                  o_ref, sg_ref, stf_ref, stb_ref, pm_ref, dio_ref, k_ref):
    b = pl.program_id(0)
    i = pl.program_id(1)
    tb, d = x_ref.shape[1], x_ref.shape[2]
    nch = tb // CHUNK
    _, _, g_m, sh_f, sc_f, _ = _mod_rows(mod_ref, b, d)
    act = lambda rows, c0: act_ref[0, rows, c0:c0 + WIDTH]
    every = slice(None)

    @pl.when(i == 0)
    def _():
        stf_ref[...] = sf0_ref[0]

    stb_ref[...] = bst_ref[0, 0]

    o_ref[...] = jnp.zeros_like(o_ref)
    work = []
    for reverse in (False, True):
        work += [(reverse, ci) for ci in (range(nch - 1, -1, -1) if reverse else range(nch))]
    masks = {reverse: _level_masks(reverse) for reverse in (False, True)}
    for n, (reverse, ci) in enumerate(work):
        rows = pl.ds(ci * CHUNK, CHUNK)
        a_l, dm_ref = (A_LB, dmb_ref) if reverse else (A_LF, dmf_ref)
        _hgrn_scores(act(rows, A_Q), act(rows, a_l), dm_ref, masks[reverse], reverse,
                     pm_ref.at[n], dio_ref.at[n], k_ref.at[n])
    for n, (reverse, ci) in enumerate(work):
        rows = pl.ds(ci * CHUNK, CHUNK)
        st_ref = stb_ref if reverse else stf_ref
        _hgrn_apply(act(rows, A_Q), act(rows, A_V), reverse, pm_ref.at[n], dio_ref.at[n], k_ref.at[n],
                    st_ref, o_ref, ci * CHUNK)

    a_in = _group_rms(o_ref[...]) * hnw_ref[...] * act(every, A_G)
    y_a = _dot(a_in.astype(BF16), wa_ref[...])

    for ci in range(nch):
        rows = slice(ci * CHUNK, (ci + 1) * CHUNK)
        for g in range(HEADS):
            sl = slice(g * HD, (g + 1) * HD)
            sg_ref[rows, sl] = (_dot(sw_ref[g], gat_ref[0, rows, G_SV + g * HD:G_SV + (g + 1) * HD])
                                + sbias_ref[:, sl])
    y_b = _dot((act(every, A_U) * sg_ref[...]).astype(BF16), wb_ref[...])

    merged = (gat_ref[0, :, G_GATES:G_GATES + d].astype(F32) * y_a
              + gat_ref[0, :, G_GATES + d:G_GATES + 2 * d].astype(F32) * y_b)
    y = _dot(merged.astype(BF16), wo_ref[...])
    x1 = x_ref[0] + g_m * y
    x1_ref[0] = x1

    h2 = _rms(x1, nffn_ref[...]) * (1.0 + sc_f) + sh_f
    h2_hi, h2_lo = _split2(h2)
    h2_ref[0] = h2_hi
    rw_hi, rw_lo = _split2(rw_ref[...])
    logits = _dot_nt(rw_hi, h2_hi) + _dot_nt(rw_hi, h2_lo) + _dot_nt(rw_lo, h2_hi)
    mx = jnp.max(logits, axis=0, keepdims=True)
    ex = jnp.exp(logits - mx)
    aff_ref[0] = ex / jnp.sum(ex, axis=0, keepdims=True)


def _mixer(act, gat, x, mod, nffn, hnw, sgu_w, sgu_bias, w_a, w_b, w_o, rw_t, s_f0, bstates, tb):
    b, t, d = x.shape
    nb = t // tb
    per_state = bstates.shape[1] // nb
    ne = rw_t.shape[0]
    dmf = jnp.asarray(np.tile(_decay_matrices(False), (1, 2)), BF16)
    dmb = jnp.asarray(np.tile(_decay_matrices(True), (1, 2)), BF16)
    const = lambda a: pl.BlockSpec(a.shape, lambda bi, i: (0,) * a.ndim, pipeline_mode=pl.Buffered(1))
    in_specs = [pl.BlockSpec((1, tb, act.shape[2]), lambda bi, i: (bi, i, 0)),
                pl.BlockSpec((1, tb, gat.shape[2]), lambda bi, i: (bi, i, 0)),
                pl.BlockSpec((1, tb, d), lambda bi, i: (bi, i, 0)),
                const(mod), const(nffn), const(hnw),
                const(sgu_w), const(sgu_bias), const(w_a), const(w_b), const(w_o), const(rw_t),
                const(dmf), const(dmb),
                pl.BlockSpec((1, N_PAIRS, PAIR, PAIR), lambda bi, i: (bi, 0, 0, 0)),
                pl.BlockSpec((1, 1, N_PAIRS, PAIR, PAIR), lambda bi, i: (bi, (i + 1) * per_state - 1, 0, 0, 0))]
    out_shape = (jax.ShapeDtypeStruct((b, t, d), F32),
                 jax.ShapeDtypeStruct((b, t, d), BF16),
                 jax.ShapeDtypeStruct((b, ne, t), F32))
    out_specs = (pl.BlockSpec((1, tb, d), lambda bi, i: (bi, i, 0)),
                 pl.BlockSpec((1, tb, d), lambda bi, i: (bi, i, 0)),
                 pl.BlockSpec((1, ne, tb), lambda bi, i: (bi, 0, i)))
    return pl.pallas_call(
        _mixer_kernel,
        out_shape=out_shape,
        grid=(b, nb),
        in_specs=in_specs,
        out_specs=out_specs,
        scratch_shapes=[pltpu.VMEM((tb, WIDTH), F32),
                        pltpu.VMEM((tb, WIDTH), F32),
                        pltpu.VMEM((N_PAIRS, PAIR, PAIR), F32),
                        pltpu.VMEM((N_PAIRS, PAIR, PAIR), F32),
                        pltpu.VMEM((2 * (tb // CHUNK), N_PAIRS, CHUNK, PAIR), BF16),
                        pltpu.VMEM((2 * (tb // CHUNK), CHUNK, WIDTH), F32),
                        pltpu.VMEM((2 * (tb // CHUNK), CHUNK, WIDTH), F32)],
        compiler_params=pltpu.CompilerParams(dimension_semantics=("arbitrary", "arbitrary"),
                                             vmem_limit_bytes=VMEM_LIMIT),
        name="mixer",
    )(act, gat, x, mod, nffn, hnw, sgu_w, sgu_bias, w_a, w_b, w_o, rw_t, dmf, dmb, s_f0, bstates)


def _front(x, c, ctx, c_ctx, ada_w, ada_b, norm_mix_w, norm_ffn_w, w_in, hgrn_lb_logits,
           hgrn_norm_w, sgu_norm_w, sgu_w, sgu_b, w_branch_a, w_branch_b, w_out, router_w, tb):
    b, t, d = x.shape
    layer = 0
    cond = jnp.zeros((8, d), F32).at[0:b].set(c).at[b].set(c_ctx)
    mod = _adaln(cond, ada_w[layer], ada_b[layer])
    lbl = hgrn_lb_logits[:, layer:layer + 2, :].reshape(4, WIDTH)
    nmix = norm_mix_w[layer].reshape(1, d)
    nffn = norm_ffn_w[layer].reshape(1, d)
    w_in_b = w_in[layer].astype(BF16)
    s_f0, s_b0 = _ctx_states(ctx, mod, nmix, w_in_b[:, WIDTH:4 * WIDTH], lbl, b)
    act, gat, bstates = _in_proj(x, mod, nmix, w_in_b, lbl, sgu_norm_w[layer].reshape(1, WIDTH), s_b0,
                                 min(t, 512), 256)
    sgu_bias = jnp.repeat(sgu_b[layer].T, HD, axis=1)
    return _mixer(act, gat, x, mod, nffn, hgrn_norm_w[layer].reshape(1, WIDTH),
                  sgu_w[layer].astype(BF16), sgu_bias,
                  w_branch_a[layer].astype(BF16), w_branch_b[layer].astype(BF16),
                  w_out[layer].astype(BF16), router_w[layer].T, s_f0, bstates, tb), mod


def _route_kernel(cap, aff_ref, pos_ref, gate_ref, starts_ref):
    a = aff_ref[0]
    ne, t = a.shape
    nblk = t // LANES
    bits = pltpu.bitcast(a, I32)

    def search(it, lo):
        cand = lo | (jnp.int32(1) << (30 - it))
        cnt = jnp.sum(jnp.where(bits >= cand, 1.0, 0.0), axis=-1, keepdims=True)
        return jnp.where(cnt >= cap, cand, lo)

    thr = lax.fori_loop(0, 31, search, jnp.zeros((ne, 1), I32))
    gt = bits > thr
    eq = bits == thr
    n_ties_wanted = cap - jnp.sum(jnp.where(gt, 1.0, 0.0), axis=-1, keepdims=True)

    row = lax.broadcasted_iota(I32, (LANES, LANES), 0)
    col = lax.broadcasted_iota(I32, (LANES, LANES), 1)
    upper = jnp.where(row <= col, 1.0, 0.0).astype(BF16)
    lane = lax.broadcasted_iota(I32, (ne, LANES), 1)

    off = jnp.zeros((ne, 1), F32)
    sel_blocks = []
    for j in range(nblk):
        sl = slice(j * LANES, (j + 1) * LANES)
        eqf = jnp.where(eq[:, sl], 1.0, 0.0)
        incl = _dot(eqf.astype(BF16), upper) + off
        keep_tie = jnp.where(incl - eqf < n_ties_wanted, eqf, 0.0)
        sel_blocks.append(jnp.where(gt[:, sl], 1.0, keep_tie))
        off = incl[:, LANES - 1:LANES]

    off = jnp.zeros((ne, 1), F32)
    starts = jnp.zeros((ne, LANES), F32)
    for j in range(nblk):
        sl = slice(j * LANES, (j + 1) * LANES)
        self = sel_blocks[j]
        starts = jnp.where(lane == j, off, starts)
        incl = _dot(self.astype(BF16), upper) + off
        pos_ref[0, :, sl] = jnp.where(self > 0.0, incl - 1.0, -1.0).astype(I32)
        for k, piece in enumerate(_split3(jnp.where(self > 0.0, a[:, sl], 0.0))):
            gate_ref[0, k * ne:(k + 1) * ne, sl] = piece.astype(F32)
        off = incl[:, LANES - 1:LANES]
    starts = jnp.where(lane >= nblk, off, starts)
    starts_ref[0] = starts.astype(I32)


def _route(aff_t, cap):
    b, ne, t = aff_t.shape
    spec = pl.BlockSpec((1, ne, t), lambda i: (i, 0, 0))
    return pl.pallas_call(
        functools.partial(_route_kernel, cap),
        out_shape=(jax.ShapeDtypeStruct((b, ne, t), I32),
                   jax.ShapeDtypeStruct((b, 3 * ne, t), F32),
                   jax.ShapeDtypeStruct((b, ne, LANES), I32)),
        grid=(b,),
        in_specs=[spec],
        out_specs=(spec, pl.BlockSpec((1, 3 * ne, t), lambda i: (i, 0, 0)),
                   pl.BlockSpec((1, ne, LANES), lambda i: (i, 0, 0))),
        compiler_params=pltpu.CompilerParams(dimension_semantics=("arbitrary",),
                                             vmem_limit_bytes=VMEM_LIMIT),
        name="route",
    )(aff_t)


TOK_BLOCK = 2 * LANES


SLOT_ALIGN = 16
AUX = LANES


def _slot_windows(starts_ref, b, e0, n_exp, ne, j, win):
    per = TOK_BLOCK // LANES
    s_lo, s_hi, a0 = [], [], []
    n_pass = jnp.int32(0)
    for le in range(n_exp):
        base = (b * ne + e0 + le) * LANES
        lo = starts_ref[base + per * j]
        hi = starts_ref[base + per * (j + 1)]
        a = (lo // SLOT_ALIGN) * SLOT_ALIGN
        n_pass = jnp.maximum(n_pass, jnp.where(hi > lo, (hi - a + win - 1) // win, 0))
        s_lo.append(lo)
        s_hi.append(hi)
        a0.append(a)
    return s_lo, s_hi, a0, n_pass


def _window(s_lo, s_hi, a0, p, win, cap):
    a = a0 + p * win
    a_c = pl.multiple_of(jnp.minimum(a, cap - win), SLOT_ALIGN)
    return a_c, jnp.maximum(s_lo, a), jnp.minimum(s_hi, a + win)


def _gather_kernel(win, starts_ref, pos_ref, gate_ref, h2_ref, xs_ref):
    b = pl.program_id(0)
    g = pl.program_id(1)
    j = pl.program_id(2)
    eg, cap = xs_ref.shape[1], xs_ref.shape[2]
    ne = pl.num_programs(1) * eg
    d = h2_ref.shape[2]

    @pl.when(j == 0)
    def _():
        xs_ref[...] = jnp.zeros_like(xs_ref)

    gcols = gate_ref[0]

    s_lo, s_hi, a0, n_pass = _slot_windows(starts_ref, b, g * eg, eg, ne, j, win)
    row = lax.broadcasted_iota(I32, (win, TOK_BLOCK), 0)
    row_d = lax.broadcasted_iota(I32, (win, d), 0)
    row_aux = lax.broadcasted_iota(I32, (win, AUX), 0)

    def one_pass(p, carry):
        wins = [_window(s_lo[le], s_hi[le], a0[le], p, win, cap) for le in range(eg)]
        onehots = []
        for le, (a_c, lo_i, hi_i) in enumerate(wins):
            prow = pos_ref[0, le, pl.ds(j, 1), :]
            owned = jnp.where(prow >= lo_i, jnp.where(prow < hi_i, prow, -1), -1)
            onehots.append(jnp.where(owned == row + a_c, 1.0, 0.0).astype(BF16))
        lhs = jnp.concatenate(onehots, axis=0)
        res = _dot(lhs, h2_ref[0])
        res_aux = _dot(lhs, gcols)
        for le, (a_c, lo_i, hi_i) in enumerate(wins):
            rows = pl.ds(a_c, win)
            mine = slice(le * win, (le + 1) * win)
            xs_ref[0, le, rows, 0:d] = jnp.where(row_d + a_c < lo_i, xs_ref[0, le, rows, 0:d],
                                                 res[mine].astype(BF16))
            xs_ref[0, le, rows, d:d + AUX] = jnp.where(row_aux + a_c < lo_i, xs_ref[0, le, rows, d:d + AUX],
                                                       res_aux[mine].astype(BF16))
        return carry

    lax.fori_loop(0, n_pass, one_pass, 0)


def _gather(starts, pos, gate3, h2, cap, win, eg):
    b, t, d = h2.shape
    ne = pos.shape[1]
    ntb = t // TOK_BLOCK
    pos4 = pos.reshape(b, ne, ntb, TOK_BLOCK)
    gate_t = jnp.pad(jnp.swapaxes(gate3, 1, 2).astype(BF16), ((0, 0), (0, 0), (0, AUX - gate3.shape[1])))
    grid_spec = pltpu.PrefetchScalarGridSpec(
        num_scalar_prefetch=1,
        grid=(b, ne // eg, ntb),
        in_specs=[pl.BlockSpec((1, eg, ntb, TOK_BLOCK), lambda bi, g, j, s: (bi, g, 0, 0)),
                  pl.BlockSpec((1, TOK_BLOCK, AUX), lambda bi, g, j, s: (bi, j, 0)),
                  pl.BlockSpec((1, TOK_BLOCK, d), lambda bi, g, j, s: (bi, j, 0))],
        out_specs=pl.BlockSpec((1, eg, cap, d + AUX), lambda bi, g, j, s: (bi, g, 0, 0)))
    return pl.pallas_call(
        functools.partial(_gather_kernel, win),
        out_shape=jax.ShapeDtypeStruct((b, ne, cap, d + AUX), BF16),
        grid_spec=grid_spec,
        compiler_params=pltpu.CompilerParams(
            dimension_semantics=("arbitrary", "arbitrary", "arbitrary"), vmem_limit_bytes=VMEM_LIMIT),
        name="gather",
    )(starts.reshape(-1), pos4, gate_t, h2)


FF_SUB = 256


def _ffn_kernel(xs_ref, wg_ref, wu_ref, wd_ref, gf_ref, ye_ref, acc_ref):
    bg = pl.program_id(0)
    e = pl.program_id(1)
    f = pl.program_id(2)
    ne = pl.num_programs(1)
    ns, cap, d = ye_ref.shape[0], ye_ref.shape[2], ye_ref.shape[3]
    xs = xs_ref[:, 0, :, 0:d].reshape(ns * cap, d)
    tf = wg_ref.shape[2]

    @pl.when(f == 0)
    def _():
        acc_ref[...] = jnp.zeros_like(acc_ref)

    acc = acc_ref[...]
    for c0 in range(0, tf, FF_SUB):
        a = _dot(xs, wg_ref[0, :, c0:c0 + FF_SUB].astype(BF16))
        u = _dot(xs, wu_ref[0, :, c0:c0 + FF_SUB].astype(BF16))
        acc = acc + _dot((_silu(a) * u).astype(BF16), wd_ref[0, c0:c0 + FF_SUB, :].astype(BF16))
    acc_ref[...] = acc
    for s in range(ns):
        aux = xs_ref[s, 0, :, d:d + AUX].astype(F32)
        lane = lax.broadcasted_iota(I32, aux.shape, 1)
        mine = jnp.where(lane < 3 * ne, jnp.where(lane % ne == e, aux, 0.0), 0.0)
        gate = jnp.sum(mine, axis=-1, keepdims=True)
        ye_ref[s, 0] = (acc[s * cap:(s + 1) * cap] * gate * gf_ref[pl.ds(bg * ns + s, 1), :]).astype(BF16)


def _ffn(xs, w_gate, w_up, w_down, mod, tf, ns):
    b, ne, cap, daux = xs.shape
    d = daux - AUX
    ff = w_gate.shape[2]
    return pl.pallas_call(
        _ffn_kernel,
        out_shape=jax.ShapeDtypeStruct((b, ne, cap, d), BF16),
        grid=(b // ns, ne, ff // tf),
        in_specs=[pl.BlockSpec((ns, 1, cap, daux), lambda bi, e, f: (bi, e, 0, 0)),
                  pl.BlockSpec((1, d, tf), lambda bi, e, f: (e, 0, f)),
                  pl.BlockSpec((1, d, tf), lambda bi, e, f: (e, 0, f)),
                  pl.BlockSpec((1, tf, d), lambda bi, e, f: (e, f, 0)),
                  pl.BlockSpec((mod.shape[0], d), lambda bi, e, f: (0, N_MOD - 1))],
        out_specs=pl.BlockSpec((ns, 1, cap, d), lambda bi, e, f: (bi, e, 0, 0)),
        scratch_shapes=[pltpu.VMEM((ns * cap, d), F32)],
        compiler_params=pltpu.CompilerParams(dimension_semantics=("arbitrary", "arbitrary", "arbitrary"),
                                             vmem_limit_bytes=VMEM_LIMIT),
        name="ffn",
    )(xs, w_gate, w_up, w_down, mod)


def _combine_kernel(win, starts_ref, post_ref, ye_ref, x1_ref, fw_ref, out_ref, acc_ref, stack_ref):
    b = pl.program_id(0)
    j = pl.program_id(1)
    ne, cap = ye_ref.shape[1], ye_ref.shape[2]
    acc_ref[...] = x1_ref[0]
    s_lo, s_hi, a0, n_pass = _slot_windows(starts_ref, b, 0, ne, ne, j, win)
    row = lax.broadcasted_iota(I32, (win, TOK_BLOCK), 0)

    def one_pass(p, carry):
        onehots = []
        for e in range(ne):
            a_c, lo_i, hi_i = _window(s_lo[e], s_hi[e], a0[e], p, win, cap)
            prow = post_ref[0, e, pl.ds(j, 1), :]
            owned = jnp.where(prow >= lo_i, jnp.where(prow < hi_i, prow, -1), -1)
            onehots.append(jnp.where(owned == row + a_c, 1.0, 0.0).astype(BF16))
            stack_ref[e * win:(e + 1) * win, :] = ye_ref[0, e, pl.ds(a_c, win), :]
        acc_ref[...] += _dot_tn(jnp.concatenate(onehots, axis=0), stack_ref[...])
        return carry

    lax.fori_loop(0, n_pass, one_pass, 0)
    out_ref[0] = _rms(acc_ref[...], fw_ref[...])


def _combine(starts, pos, ye, x1, final_w, win):
    b, t, d = x1.shape
    ne, cap = ye.shape[1], ye.shape[2]
    ntb = t // TOK_BLOCK
    pos4 = pos.reshape(b, ne, ntb, TOK_BLOCK)
    grid_spec = pltpu.PrefetchScalarGridSpec(
        num_scalar_prefetch=1,
        grid=(b, ntb),
        in_specs=[pl.BlockSpec((1, ne, ntb, TOK_BLOCK), lambda bi, j, s: (bi, 0, 0, 0)),
                  pl.BlockSpec((1, ne, cap, d), lambda bi, j, s: (bi, 0, 0, 0),
                               pipeline_mode=pl.Buffered(1)),
                  pl.BlockSpec((1, TOK_BLOCK, d), lambda bi, j, s: (bi, j, 0)),
                  pl.BlockSpec((1, d), lambda bi, j, s: (0, 0))],
        out_specs=pl.BlockSpec((1, TOK_BLOCK, d), lambda bi, j, s: (bi, j, 0)),
        scratch_shapes=[pltpu.VMEM((TOK_BLOCK, d), F32), pltpu.VMEM((ne * win, d), BF16)])
    return pl.pallas_call(
        functools.partial(_combine_kernel, win),
        out_shape=jax.ShapeDtypeStruct((b, t, d), F32),
        grid_spec=grid_spec,
        compiler_params=pltpu.CompilerParams(
            dimension_semantics=("arbitrary", "arbitrary"), vmem_limit_bytes=VMEM_LIMIT),
        name="combine",
    )(starts.reshape(-1), pos4, ye, x1, final_w.reshape(1, d))


def kernel(x, c, ctx, c_ctx, ada_w, ada_b, norm_mix_w, norm_ffn_w, w_in, hgrn_lb_logits, hgrn_norm_w,
           sgu_norm_w, sgu_w, sgu_b, w_branch_a, w_branch_b, w_out, router_w, expert_w_gate,
           expert_w_up, expert_w_down, final_norm_w):
    b, t, d = x.shape
    assert b + 1 <= 8 and t % TOK_BLOCK == 0 and d % LANES == 0
    ne = router_w.shape[-1]
    cap = CAPACITY_FACTOR * t // ne
    win = min(cap, 64)
    assert cap % SLOT_ALIGN == 0 and t // LANES < LANES and 3 * ne <= AUX
    (x1, h2, aff_t), mod = _front(x, c, ctx, c_ctx, ada_w, ada_b, norm_mix_w, norm_ffn_w, w_in,
                                  hgrn_lb_logits, hgrn_norm_w, sgu_norm_w, sgu_w, sgu_b,
                                  w_branch_a, w_branch_b, w_out, router_w, 512)
    pos, gate, starts = _route(aff_t, cap)
    xs = _gather(starts, pos, gate, h2, cap, win, 8)
    ye = _ffn(xs, expert_w_gate[0], expert_w_up[0], expert_w_down[0], mod, 1024, 1)
    return _combine(starts, pos, ye, x1, final_norm_w, win)
```

```python
import functools

import numpy as np
import jax
import jax.numpy as jnp
from jax import lax
from jax.experimental import pallas as pl
from jax.experimental.pallas import tpu as pltpu

F32 = jnp.float32
BF16 = jnp.bfloat16
I32 = jnp.int32

LANES = 128
EPS = 1e-6
N_MOD = 6
HEADS = 4
HD = 128
WIDTH = HEADS * HD
CHUNK = 128
N_EXPERTS = 16
CAPACITY_FACTOR = 2
LEVELS = tuple(CHUNK >> (i + 1) for i in range(7))
N_DMAT = 1 + len(LEVELS) - 1
PAIR = 2 * HD
N_PAIRS = HEADS // 2
ROW_GROUP = 16
LOG2E = 1.4426950408889634
VMEM_LIMIT = 52 * 1024 * 1024


def _dot(a, b):
    return jnp.dot(a, b, preferred_element_type=F32)


def _dot_nt(a, b):
    return lax.dot_general(a, b, (((1,), (1,)), ((), ())), preferred_element_type=F32)


def _dot_tn(a, b):
    return lax.dot_general(a, b, (((0,), (0,)), ((), ())), preferred_element_type=F32)


def _split2(x):
    hi = x.astype(BF16)
    lo = (x - hi.astype(F32)).astype(BF16)
    return hi, lo


def _split3(x):
    hi = x.astype(BF16)
    r = x - hi.astype(F32)
    mid = r.astype(BF16)
    lo = (r - mid.astype(F32)).astype(BF16)
    return hi, mid, lo


def _sigmoid(x):
    return 1.0 / (1.0 + jnp.exp(-x))


def _silu(x):
    return x * _sigmoid(x)


def _gelu_tanh(x):
    c = np.sqrt(2.0 / np.pi).astype(np.float32)
    return 0.5 * x * (1.0 + jnp.tanh(c * (x + 0.044715 * (x * x * x))))


def _rms(x, w):
    return x * lax.rsqrt(jnp.mean(x * x, axis=-1, keepdims=True) + EPS) * w


def _lower_bound(lbl, d):
    l0 = lbl[2 * d:2 * d + 1, :]
    l1 = lbl[2 * d + 1:2 * d + 2, :]
    m = jnp.maximum(l0, l1)
    e0 = jnp.exp(l0 - m)
    e1 = jnp.exp(l1 - m)
    return e0 / (e0 + e1)


def _forget(z, lb):
    f = lb + (1.0 - lb) * _sigmoid(z)
    return jnp.log(f), 1.0 - f


def _cumsum_dot(m_bf16, x):
    hi, mid, lo = _split3(x)
    return _dot(m_bf16, hi) + _dot(m_bf16, mid) + _dot(m_bf16, lo)


def _same_head(shape):
    r = lax.broadcasted_iota(I32, shape, 0) < HD
    c = lax.broadcasted_iota(I32, shape, 1) < HD
    return r == c


def _block_diag(x):
    first = lax.broadcasted_iota(I32, x.shape, 1) < HD
    zero = jnp.zeros_like(x)
    return jnp.concatenate([jnp.where(first, x, zero), jnp.where(first, zero, x)], axis=0)


def _pair_outer(v_pair, w_pair):
    full = _dot_tn(v_pair, w_pair)
    return jnp.where(_same_head(full.shape), full, 0.0)


def _decay_matrices(reverse):
    c = CHUNK
    i = np.arange(c)[:, None]
    m = np.arange(c)[None, :]
    mats = []
    mats.append(m >= i if reverse else m <= i)
    for half in LEVELS[:-1]:
        a = (i // (2 * half)) * (2 * half)
        mid = a + half
        if not reverse:
            qside = i >= mid
            mat = np.where(qside, (m >= mid) & (m <= i), (m > i) & (m < mid))
        else:
            qside = i < mid
            mat = np.where(qside, (m >= i) & (m < mid), (m >= mid) & (m < i))
        mats.append(mat)
    return np.concatenate(mats, axis=0).astype(np.float32)


def _block_decay_matrix(n, reverse):
    i = np.arange(n)[:, None]
    m = np.arange(n)[None, :]
    return ((m < i) if reverse else (m > i)).astype(np.float32)


def _adaln_kernel(cond_ref, w_ref, b_ref, out_ref):
    s = _silu(cond_ref[...])
    s_hi, s_lo = _split2(s)
    w_hi, w_lo = _split2(w_ref[...])
    out_ref[...] = _dot(s_hi, w_hi) + _dot(s_hi, w_lo) + _dot(s_lo, w_hi) + b_ref[...]


def _adaln(cond, ada_w, ada_b):
    rows, d = cond.shape
    n = ada_w.shape[1]
    tn = 1024
    return pl.pallas_call(
        _adaln_kernel,
        out_shape=jax.ShapeDtypeStruct((rows, n), F32),
        grid=(n // tn,),
        in_specs=[pl.BlockSpec((rows, d), lambda j: (0, 0)),
                  pl.BlockSpec((d, tn), lambda j: (0, j)),
                  pl.BlockSpec((1, tn), lambda j: (0, j))],
        out_specs=pl.BlockSpec((rows, tn), lambda j: (0, j)),
        compiler_params=pltpu.CompilerParams(dimension_semantics=("arbitrary",),
                                             vmem_limit_bytes=VMEM_LIMIT),
        name="adaln",
    )(cond, ada_w, ada_b.reshape(1, n))


def _mod_rows(mod_ref, row, d):
    return [mod_ref[pl.ds(row, 1), j * d:(j + 1) * d] for j in range(N_MOD)]


def _ctx_kernel(ctx_row, x_ref, mod_ref, nw_ref, w_ref, lbl_ref, mf_ref, mb_ref, sf_ref, sb_ref):
    d = x_ref.shape[-1]
    x = x_ref[0]
    sh, sc = mod_ref[pl.ds(ctx_row, 1), 0:d], mod_ref[pl.ds(ctx_row, 1), d:2 * d]
    h = _rms(x, nw_ref[...]) * (1.0 + sc) + sh
    p = _dot(h.astype(BF16), w_ref[...])
    v = p[:, 2 * WIDTH:3 * WIDTH].astype(BF16)
    lbl = lbl_ref[...]
    for dirn, (m_ref, out_ref) in enumerate(((mf_ref, sf_ref), (mb_ref, sb_ref))):
        logf, k = _forget(p[:, dirn * WIDTH:(dirn + 1) * WIDTH], _lower_bound(lbl, dirn))
        w = (k * jnp.exp(_cumsum_dot(m_ref[...], logf))).astype(BF16)
        for pr in range(N_PAIRS):
            sl = slice(pr * PAIR, (pr + 1) * PAIR)
            out_ref[0, pr] = _pair_outer(v[:, sl], w[:, sl])


def _ctx_states(ctx, mod, norm_w, w_c, lbl, ctx_row):
    b, l, d = ctx.shape
    mf = jnp.asarray(_block_decay_matrix(l, False), BF16)
    mb = jnp.asarray(_block_decay_matrix(l, True), BF16)
    full = lambda a: pl.BlockSpec(a.shape, lambda i: (0,) * a.ndim)
    st = jax.ShapeDtypeStruct((b, N_PAIRS, PAIR, PAIR), F32)
    st_spec = pl.BlockSpec((1, N_PAIRS, PAIR, PAIR), lambda i: (i, 0, 0, 0))
    return pl.pallas_call(
        functools.partial(_ctx_kernel, ctx_row),
        out_shape=(st, st),
        grid=(b,),
        in_specs=[pl.BlockSpec((1, l, d), lambda i: (i, 0, 0)), full(mod), full(norm_w),
                  full(w_c), full(lbl), full(mf), full(mb)],
        out_specs=(st_spec, st_spec),
        compiler_params=pltpu.CompilerParams(dimension_semantics=("arbitrary",),
                                             vmem_limit_bytes=VMEM_LIMIT),
        name="ctx_state",
    )(ctx, mod, norm_w, w_c, lbl, mf, mb)


A_Q, A_LF, A_LB, A_V, A_G, A_U = (j * WIDTH for j in range(6))
A_COLS = 6 * WIDTH
G_SV, G_GATES = 0, WIDTH
PROJ_PIECE = 256


def _in_proj_kernel(sub, x_ref, mod_ref, nw_ref, w_ref, lbl_ref, snw_ref, m_ref, s0_ref,
                    act_ref, gat_ref, bst_ref, st_ref, kb_ref, h_ref):
    b = pl.program_id(0)
    d = x_ref.shape[-1]

    @pl.when(pl.program_id(1) == 0)
    def _():
        st_ref[...] = s0_ref[0]

    sh, sc = mod_ref[pl.ds(b, 1), 0:d], mod_ref[pl.ds(b, 1), d:2 * d]
    h_ref[...] = (_rms(x_ref[0], nw_ref[...]) * (1.0 + sc) + sh).astype(BF16)
    lbl = lbl_ref[...]

    def pieces(j):
        for c in range(0, WIDTH, PROJ_PIECE):
            yield c, _dot(h_ref[...], w_ref[:, j * WIDTH + c:j * WIDTH + c + PROJ_PIECE])

    lb_f, lb_b = _lower_bound(lbl, 0), _lower_bound(lbl, 1)
    for c, p in pieces(0):
        act_ref[0, :, A_Q + c:A_Q + c + PROJ_PIECE] = _silu(p)
    for c, p in pieces(1):
        act_ref[0, :, A_LF + c:A_LF + c + PROJ_PIECE] = _forget(p, lb_f[:, c:c + PROJ_PIECE])[0]
    for c, p in pieces(2):
        logf_b, k_b = _forget(p, lb_b[:, c:c + PROJ_PIECE])
        act_ref[0, :, A_LB + c:A_LB + c + PROJ_PIECE] = logf_b
        kb_ref[:, c:c + PROJ_PIECE] = k_b
    for c, p in pieces(3):
        act_ref[0, :, A_V + c:A_V + c + PROJ_PIECE] = p
    for c, p in pieces(4):
        act_ref[0, :, A_G + c:A_G + c + PROJ_PIECE] = _silu(p)
    for c, p in pieces(5):
        act_ref[0, :, A_U + c:A_U + c + PROJ_PIECE] = _gelu_tanh(p)
    for c, p in pieces(6):
        gat_ref[0, :, G_SV + c:G_SV + c + PROJ_PIECE] = (
            _group_rms(_gelu_tanh(p)) * snw_ref[:, c:c + PROJ_PIECE]).astype(BF16)
    for j in range(2 * d // WIDTH):
        for c, p in pieces(7 + j):
            gat_ref[0, :, G_GATES + j * WIDTH + c:G_GATES + j * WIDTH + c + PROJ_PIECE] = _sigmoid(p).astype(BF16)

    for piece in range(x_ref.shape[1] // sub - 1, -1, -1):
        rows = slice(piece * sub, (piece + 1) * sub)
        bst_ref[0, piece] = st_ref[...]
        logf = act_ref[0, rows, A_LB:A_LB + WIDTH]
        w = (kb_ref[rows, :] * jnp.exp(_cumsum_dot(m_ref[...], logf))).astype(BF16)
        v = act_ref[0, rows, A_V:A_V + WIDTH].astype(BF16)
        tot = jnp.exp(jnp.sum(logf, axis=0, keepdims=True))
        for pr in range(N_PAIRS):
            sl = slice(pr * PAIR, (pr + 1) * PAIR)
            st_ref[pr] = st_ref[pr] * tot[:, sl] + _pair_outer(v[:, sl], w[:, sl])


def _in_proj(x, mod, norm_w, w_in, lbl, snw, s_b0, tm, sub):
    b, t, d = x.shape
    nb = t // tm
    per = tm // sub
    m = jnp.asarray(_block_decay_matrix(sub, True), BF16)
    full = lambda a: pl.BlockSpec(a.shape, lambda bi, i: (0,) * a.ndim, pipeline_mode=pl.Buffered(1))
    n_gat = G_GATES + 2 * d
    return pl.pallas_call(
        functools.partial(_in_proj_kernel, sub),
        out_shape=(jax.ShapeDtypeStruct((b, t, A_COLS), F32),
                   jax.ShapeDtypeStruct((b, t, n_gat), BF16),
                   jax.ShapeDtypeStruct((b, t // sub, N_PAIRS, PAIR, PAIR), F32)),
        grid=(b, nb),
        in_specs=[pl.BlockSpec((1, tm, d), lambda bi, i: (bi, nb - 1 - i, 0)), full(mod), full(norm_w),
                  full(w_in), full(lbl), full(snw), full(m),
                  pl.BlockSpec((1, N_PAIRS, PAIR, PAIR), lambda bi, i: (bi, 0, 0, 0))],
        out_specs=(pl.BlockSpec((1, tm, A_COLS), lambda bi, i: (bi, nb - 1 - i, 0)),
                   pl.BlockSpec((1, tm, n_gat), lambda bi, i: (bi, nb - 1 - i, 0)),
                   pl.BlockSpec((1, per, N_PAIRS, PAIR, PAIR), lambda bi, i: (bi, nb - 1 - i, 0, 0, 0))),
        scratch_shapes=[pltpu.VMEM((N_PAIRS, PAIR, PAIR), F32), pltpu.VMEM((tm, WIDTH), F32),
                        pltpu.VMEM((tm, d), BF16)],
        compiler_params=pltpu.CompilerParams(dimension_semantics=("arbitrary", "arbitrary"),
                                             vmem_limit_bytes=VMEM_LIMIT),
        name="in_proj",
    )(x, mod, norm_w, w_in, lbl, snw, m, s_b0)


def _level_masks(reverse):
    row = lax.broadcasted_iota(I32, (CHUNK, PAIR), 0)
    col = lax.broadcasted_iota(I32, (CHUNK, PAIR), 1) & (CHUNK - 1)
    x = row ^ col
    out = []
    for half in LEVELS:
        in_pair = jnp.where(x >= half, jnp.where(x < 2 * half, 1.0, 0.0), 0.0)
        bit = (col if reverse else row) & half
        out.append((jnp.where(bit != 0, in_pair, 0.0), (row & half) != 0))
    return out


def _hgrn_scores(q, logf, dm_ref, masks, reverse, pm_ref, dio_ref, k_ref):
    lf2 = logf * LOG2E
    f = jnp.exp2(lf2)
    k = 1.0 - f
    k_ref[...] = k
    hi, lo = _split2(lf2)
    dall = _dot(dm_ref[...], jnp.concatenate([hi, lo], axis=0))
    dio_ref[...] = dall[0:CHUNK]
    n_groups = CHUNK // ROW_GROUP
    for pr in range(N_PAIRS):
        sl = slice(pr * PAIR, (pr + 1) * PAIR)
        qp, kp = q[:, sl], k[:, sl]
        pm = [jnp.zeros((ROW_GROUP, PAIR), F32) for _ in range(n_groups)]
        for li, half in enumerate(LEVELS):
            mask, row_bit = masks[li]
            qside = jnp.logical_not(row_bit) if reverse else row_bit
            if half > 1:
                e = jnp.exp2(dall[(1 + li) * CHUNK:(2 + li) * CHUNK, sl])
                xm = (jnp.where(qside, qp, kp) * e).astype(BF16)
            else:
                xm = jnp.where(qside, qp * f[:, sl], kp).astype(BF16)
            groups = [g for g in range(n_groups)
                      if half < ROW_GROUP or (((g * ROW_GROUP) & half) != 0) != reverse]
            lhs = xm if len(groups) == n_groups else jnp.concatenate(
                [xm[g * ROW_GROUP:(g + 1) * ROW_GROUP] for g in groups], axis=0)
            gm = _dot_nt(lhs, _block_diag(xm))
            for n, g in enumerate(groups):
                pm[g] = pm[g] + (gm[n * ROW_GROUP:(n + 1) * ROW_GROUP]
                                 * mask[g * ROW_GROUP:(g + 1) * ROW_GROUP])
        pm_ref[pr] = jnp.concatenate(pm, axis=0).astype(BF16)


def _hgrn_apply(q, v, reverse, pm_ref, dio_ref, k_ref, state_ref, o_ref, r0):
    k = k_ref[...]
    d_in = dio_ref[...]
    tot_row = 0 if reverse else CHUNK - 1
    d_tot = d_in[tot_row:tot_row + 1]
    e_in = jnp.exp2(d_in)
    e_st = jnp.exp2(d_tot - d_in)
    e_tot = e_in[tot_row:tot_row + 1]
    first = lax.broadcasted_iota(I32, (CHUNK, PAIR), 1) < HD
    for pr in range(N_PAIRS):
        sl = slice(pr * PAIR, (pr + 1) * PAIR)
        qp, kp, vp = q[:, sl], k[:, sl], v[:, sl]
        qk = qp * kp
        diag = jnp.where(first, jnp.sum(qk[:, 0:HD], axis=-1, keepdims=True),
                         jnp.sum(qk[:, HD:PAIR], axis=-1, keepdims=True))
        st = state_ref[pr]
        o = (_dot(pm_ref[pr], _block_diag(vp.astype(BF16))) + diag * vp
             + _dot_nt((qp * e_in[:, sl]).astype(BF16), st.astype(BF16)))
        o_ref[pl.ds(r0, CHUNK), sl] += o
        state_ref[pr] = st * e_tot[:, sl] + _pair_outer(vp.astype(BF16), (kp * e_st[:, sl]).astype(BF16))


def _group_rms(x):
    return jnp.concatenate(
        [x[:, g * HD:(g + 1) * HD]
         * lax.rsqrt(jnp.mean(x[:, g * HD:(g + 1) * HD] ** 2, axis=-1, keepdims=True) + EPS)
         for g in range(x.shape[1] // HD)], axis=-1)


def _mixer_kernel(act_ref, gat_ref, x_ref, mod_ref, nffn_ref, hnw_ref,
                  sw_ref, sbias_ref, wa_ref, wb_ref, wo_ref, rw_ref, dmf_ref, dmb_ref,
                  sf0_ref, bst_ref,
                  x1_ref, h2_ref, aff_ref,
                  o_ref, sg_ref, stf_ref, stb_ref, pm_ref, dio_ref, k_ref):
    b = pl.program_id(0)
    i = pl.program_id(1)
    tb, d = x_ref.shape[1], x_ref.shape[2]
    nch = tb // CHUNK
    _, _, g_m, sh_f, sc_f, _ = _mod_rows(mod_ref, b, d)
    act = lambda rows, c0: act_ref[0, rows, c0:c0 + WIDTH]
    every = slice(None)

    @pl.when(i == 0)
    def _():
        stf_ref[...] = sf0_ref[0]

    stb_ref[...] = bst_ref[0, 0]

    o_ref[...] = jnp.zeros_like(o_ref)
    work = []
    for reverse in (False, True):
        work += [(reverse, ci) for ci in (range(nch - 1, -1, -1) if reverse else range(nch))]
    masks = {reverse: _level_masks(reverse) for reverse in (False, True)}
    for n, (reverse, ci) in enumerate(work):
        rows = pl.ds(ci * CHUNK, CHUNK)
        a_l, dm_ref = (A_LB, dmb_ref) if reverse else (A_LF, dmf_ref)
        _hgrn_scores(act(rows, A_Q), act(rows, a_l), dm_ref, masks[reverse], reverse,
                     pm_ref.at[n], dio_ref.at[n], k_ref.at[n])
    for n, (reverse, ci) in enumerate(work):
        rows = pl.ds(ci * CHUNK, CHUNK)
        st_ref = stb_ref if reverse else stf_ref
        _hgrn_apply(act(rows, A_Q), act(rows, A_V), reverse, pm_ref.at[n], dio_ref.at[n], k_ref.at[n],
                    st_ref, o_ref, ci * CHUNK)

    a_in = _group_rms(o_ref[...]) * hnw_ref[...] * act(every, A_G)
    y_a = _dot(a_in.astype(BF16), wa_ref[...])

    for ci in range(nch):
        rows = slice(ci * CHUNK, (ci + 1) * CHUNK)
        for g in range(HEADS):
            sl = slice(g * HD, (g + 1) * HD)
            sg_ref[rows, sl] = (_dot(sw_ref[g], gat_ref[0, rows, G_SV + g * HD:G_SV + (g + 1) * HD])
                                + sbias_ref[:, sl])
    y_b = _dot((act(every, A_U) * sg_ref[...]).astype(BF16), wb_ref[...])

    merged = (gat_ref[0, :, G_GATES:G_GATES + d].astype(F32) * y_a
              + gat_ref[0, :, G_GATES + d:G_GATES + 2 * d].astype(F32) * y_b)
    y = _dot(merged.astype(BF16), wo_ref[...])
    x1 = x_ref[0] + g_m * y
    x1_ref[0] = x1

    h2 = _rms(x1, nffn_ref[...]) * (1.0 + sc_f) + sh_f
    h2_hi, h2_lo = _split2(h2)
    h2_ref[0] = h2_hi
    rw_hi, rw_lo = _split2(rw_ref[...])
    logits = _dot_nt(rw_hi, h2_hi) + _dot_nt(rw_hi, h2_lo) + _dot_nt(rw_lo, h2_hi)
    mx = jnp.max(logits, axis=0, keepdims=True)
    ex = jnp.exp(logits - mx)
    aff_ref[0] = ex / jnp.sum(ex, axis=0, keepdims=True)


def _mixer(act, gat, x, mod, nffn, hnw, sgu_w, sgu_bias, w_a, w_b, w_o, rw_t, s_f0, bstates, tb):
    b, t, d = x.shape
    nb = t // tb
    per_state = bstates.shape[1] // nb
    ne = rw_t.shape[0]
    dmf = jnp.asarray(np.tile(_decay_matrices(False), (1, 2)), BF16)
    dmb = jnp.asarray(np.tile(_decay_matrices(True), (1, 2)), BF16)
    const = lambda a: pl.BlockSpec(a.shape, lambda bi, i: (0,) * a.ndim, pipeline_mode=pl.Buffered(1))
    in_specs = [pl.BlockSpec((1, tb, act.shape[2]), lambda bi, i: (bi, i, 0)),
                pl.BlockSpec((1, tb, gat.shape[2]), lambda bi, i: (bi, i, 0)),
                pl.BlockSpec((1, tb, d), lambda bi, i: (bi, i, 0)),
                const(mod), const(nffn), const(hnw),
                const(sgu_w), const(sgu_bias), const(w_a), const(w_b), const(w_o), const(rw_t),
                const(dmf), const(dmb),
                pl.BlockSpec((1, N_PAIRS, PAIR, PAIR), lambda bi, i: (bi, 0, 0, 0)),
                pl.BlockSpec((1, 1, N_PAIRS, PAIR, PAIR), lambda bi, i: (bi, (i + 1) * per_state - 1, 0, 0, 0))]
    out_shape = (jax.ShapeDtypeStruct((b, t, d), F32),
                 jax.ShapeDtypeStruct((b, t, d), BF16),
                 jax.ShapeDtypeStruct((b, ne, t), F32))
    out_specs = (pl.BlockSpec((1, tb, d), lambda bi, i: (bi, i, 0)),
                 pl.BlockSpec((1, tb, d), lambda bi, i: (bi, i, 0)),
                 pl.BlockSpec((1, ne, tb), lambda bi, i: (bi, 0, i)))
    return pl.pallas_call(
        _mixer_kernel,
        out_shape=out_shape,
        grid=(b, nb),
        in_specs=in_specs,
        out_specs=out_specs,
        scratch_shapes=[pltpu.VMEM((tb, WIDTH), F32),
                        pltpu.VMEM((tb, WIDTH), F32),
                        pltpu.VMEM((N_PAIRS, PAIR, PAIR), F32),
                        pltpu.VMEM((N_PAIRS, PAIR, PAIR), F32),
                        pltpu.VMEM((2 * (tb // CHUNK), N_PAIRS, CHUNK, PAIR), BF16),
                        pltpu.VMEM((2 * (tb // CHUNK), CHUNK, WIDTH), F32),
                        pltpu.VMEM((2 * (tb // CHUNK), CHUNK, WIDTH), F32)],
        compiler_params=pltpu.CompilerParams(dimension_semantics=("arbitrary", "arbitrary"),
                                             vmem_limit_bytes=VMEM_LIMIT),
        name="mixer",
    )(act, gat, x, mod, nffn, hnw, sgu_w, sgu_bias, w_a, w_b, w_o, rw_t, dmf, dmb, s_f0, bstates)


def _front(x, c, ctx, c_ctx, ada_w, ada_b, norm_mix_w, norm_ffn_w, w_in, hgrn_lb_logits,
           hgrn_norm_w, sgu_norm_w, sgu_w, sgu_b, w_branch_a, w_branch_b, w_out, router_w, tb):
    b, t, d = x.shape
    layer = 0
    cond = jnp.zeros((8, d), F32).at[0:b].set(c).at[b].set(c_ctx)
    mod = _adaln(cond, ada_w[layer], ada_b[layer])
    lbl = hgrn_lb_logits[:, layer:layer + 2, :].reshape(4, WIDTH)
    nmix = norm_mix_w[layer].reshape(1, d)
    nffn = norm_ffn_w[layer].reshape(1, d)
    w_in_b = w_in[layer].astype(BF16)
    s_f0, s_b0 = _ctx_states(ctx, mod, nmix, w_in_b[:, WIDTH:4 * WIDTH], lbl, b)
    act, gat, bstates = _in_proj(x, mod, nmix, w_in_b, lbl, sgu_norm_w[layer].reshape(1, WIDTH), s_b0,
                                 min(t, 512), 256)
    sgu_bias = jnp.repeat(sgu_b[layer].T, HD, axis=1)
    return _mixer(act, gat, x, mod, nffn, hgrn_norm_w[layer].reshape(1, WIDTH),
                  sgu_w[layer].astype(BF16), sgu_bias,
                  w_branch_a[layer].astype(BF16), w_branch_b[layer].astype(BF16),
                  w_out[layer].astype(BF16), router_w[layer].T, s_f0, bstates, tb), mod


def _route_kernel(cap, aff_ref, pos_ref, gate_ref, starts_ref):
    a = aff_ref[0]
    ne, t = a.shape
    nblk = t // LANES
    bits = pltpu.bitcast(a, I32)

    def search(it, lo):
        cand = lo | (jnp.int32(1) << (30 - it))
        cnt = jnp.sum(jnp.where(bits >= cand, 1.0, 0.0), axis=-1, keepdims=True)
        return jnp.where(cnt >= cap, cand, lo)

    thr = lax.fori_loop(0, 31, search, jnp.zeros((ne, 1), I32))
    gt = bits > thr
    eq = bits == thr
    n_ties_wanted = cap - jnp.sum(jnp.where(gt, 1.0, 0.0), axis=-1, keepdims=True)

    row = lax.broadcasted_iota(I32, (LANES, LANES), 0)
    col = lax.broadcasted_iota(I32, (LANES, LANES), 1)
    upper = jnp.where(row <= col, 1.0, 0.0).astype(BF16)
    lane = lax.broadcasted_iota(I32, (ne, LANES), 1)

    off = jnp.zeros((ne, 1), F32)
    sel_blocks = []
    for j in range(nblk):
        sl = slice(j * LANES, (j + 1) * LANES)
        eqf = jnp.where(eq[:, sl], 1.0, 0.0)
        incl = _dot(eqf.astype(BF16), upper) + off
        keep_tie = jnp.where(incl - eqf < n_ties_wanted, eqf, 0.0)
        sel_blocks.append(jnp.where(gt[:, sl], 1.0, keep_tie))
        off = incl[:, LANES - 1:LANES]

    off = jnp.zeros((ne, 1), F32)
    starts = jnp.zeros((ne, LANES), F32)
    for j in range(nblk):
        sl = slice(j * LANES, (j + 1) * LANES)
        self = sel_blocks[j]
        starts = jnp.where(lane == j, off, starts)
        incl = _dot(self.astype(BF16), upper) + off
        pos_ref[0, :, sl] = jnp.where(self > 0.0, incl - 1.0, -1.0).astype(I32)
        for k, piece in enumerate(_split3(jnp.where(self > 0.0, a[:, sl], 0.0))):
            gate_ref[0, k * ne:(k + 1) * ne, sl] = piece.astype(F32)
        off = incl[:, LANES - 1:LANES]
    starts = jnp.where(lane >= nblk, off, starts)
    starts_ref[0] = starts.astype(I32)


def _route(aff_t, cap):
    b, ne, t = aff_t.shape
    spec = pl.BlockSpec((1, ne, t), lambda i: (i, 0, 0))
    return pl.pallas_call(
        functools.partial(_route_kernel, cap),
        out_shape=(jax.ShapeDtypeStruct((b, ne, t), I32),
                   jax.ShapeDtypeStruct((b, 3 * ne, t), F32),
                   jax.ShapeDtypeStruct((b, ne, LANES), I32)),
        grid=(b,),
        in_specs=[spec],
        out_specs=(spec, pl.BlockSpec((1, 3 * ne, t), lambda i: (i, 0, 0)),
                   pl.BlockSpec((1, ne, LANES), lambda i: (i, 0, 0))),
        compiler_params=pltpu.CompilerParams(dimension_semantics=("arbitrary",),
                                             vmem_limit_bytes=VMEM_LIMIT),
        name="route",
    )(aff_t)


TOK_BLOCK = 2 * LANES
GATHER_STEP_BLOCKS = 4
COMBINE_STEP_BLOCKS = 2


def _step_tokens(t, blocks):
    return blocks * TOK_BLOCK if t % (blocks * TOK_BLOCK) == 0 else TOK_BLOCK


SLOT_ALIGN = 16
AUX = LANES


def _slot_windows(starts_ref, b, e0, n_exp, ne, j, win):
    per = TOK_BLOCK // LANES
    s_lo, s_hi, a0 = [], [], []
    n_pass = jnp.int32(0)
    for le in range(n_exp):
        base = (b * ne + e0 + le) * LANES
        lo = starts_ref[base + per * j]
        hi = starts_ref[base + per * (j + 1)]
        a = (lo // SLOT_ALIGN) * SLOT_ALIGN
        n_pass = jnp.maximum(n_pass, jnp.where(hi > lo, (hi - a + win - 1) // win, 0))
        s_lo.append(lo)
        s_hi.append(hi)
        a0.append(a)
    return s_lo, s_hi, a0, n_pass


def _window(s_lo, s_hi, a0, p, win, cap):
    a = a0 + p * win
    a_c = pl.multiple_of(jnp.minimum(a, cap - win), SLOT_ALIGN)
    return a_c, jnp.maximum(s_lo, a), jnp.minimum(s_hi, a + win)


def _gather_kernel(win, starts_ref, pos_ref, gate_ref, h2_ref, xs_ref):
    b = pl.program_id(0)
    g = pl.program_id(1)
    eg, cap = xs_ref.shape[1], xs_ref.shape[2]
    ne = pl.num_programs(1) * eg
    d = h2_ref.shape[2]

    @pl.when(pl.program_id(2) == 0)
    def _():
        xs_ref[...] = jnp.zeros_like(xs_ref)

    row = lax.broadcasted_iota(I32, (win, TOK_BLOCK), 0)
    row_d = lax.broadcasted_iota(I32, (win, d), 0)
    row_aux = lax.broadcasted_iota(I32, (win, AUX), 0)

    for blk in range(h2_ref.shape[1] // TOK_BLOCK):
        j = pl.program_id(2) * (h2_ref.shape[1] // TOK_BLOCK) + blk
        toks = slice(blk * TOK_BLOCK, (blk + 1) * TOK_BLOCK)
        s_lo, s_hi, a0, n_pass = _slot_windows(starts_ref, b, g * eg, eg, ne, j, win)

        def one_pass(p, carry, j=j, toks=toks, s_lo=s_lo, s_hi=s_hi, a0=a0):
            wins = [_window(s_lo[le], s_hi[le], a0[le], p, win, cap) for le in range(eg)]
            onehots = []
            for le, (a_c, lo_i, hi_i) in enumerate(wins):
                prow = pos_ref[0, le, pl.ds(j, 1), :]
                owned = jnp.where(prow >= lo_i, jnp.where(prow < hi_i, prow, -1), -1)
                onehots.append(jnp.where(owned == row + a_c, 1.0, 0.0).astype(BF16))
            lhs = jnp.concatenate(onehots, axis=0)
            res = _dot(lhs, h2_ref[0, toks, :])
            res_aux = _dot(lhs, gate_ref[0, toks, :])
            for le, (a_c, lo_i, hi_i) in enumerate(wins):
                rows = pl.ds(a_c, win)
                mine = slice(le * win, (le + 1) * win)
                xs_ref[0, le, rows, 0:d] = jnp.where(row_d + a_c < lo_i, xs_ref[0, le, rows, 0:d],
                                                     res[mine].astype(BF16))
                xs_ref[0, le, rows, d:d + AUX] = jnp.where(row_aux + a_c < lo_i,
                                                           xs_ref[0, le, rows, d:d + AUX],
                                                           res_aux[mine].astype(BF16))
            return carry

        lax.fori_loop(0, n_pass, one_pass, 0)


def _gather(starts, pos, gate3, h2, cap, win, eg):
    b, t, d = h2.shape
    ne = pos.shape[1]
    ntb = t // TOK_BLOCK
    pos4 = pos.reshape(b, ne, ntb, TOK_BLOCK)
    gate_t = jnp.pad(jnp.swapaxes(gate3, 1, 2).astype(BF16), ((0, 0), (0, 0), (0, AUX - gate3.shape[1])))
    step = _step_tokens(t, GATHER_STEP_BLOCKS)
    grid_spec = pltpu.PrefetchScalarGridSpec(
        num_scalar_prefetch=1,
        grid=(b, ne // eg, t // step),
        in_specs=[pl.BlockSpec((1, eg, ntb, TOK_BLOCK), lambda bi, g, j, s: (bi, g, 0, 0)),
                  pl.BlockSpec((1, step, AUX), lambda bi, g, j, s: (bi, j, 0)),
                  pl.BlockSpec((1, step, d), lambda bi, g, j, s: (bi, j, 0))],
        out_specs=pl.BlockSpec((1, eg, cap, d + AUX), lambda bi, g, j, s: (bi, g, 0, 0)))
    return pl.pallas_call(
        functools.partial(_gather_kernel, win),
        out_shape=jax.ShapeDtypeStruct((b, ne, cap, d + AUX), BF16),
        grid_spec=grid_spec,
        compiler_params=pltpu.CompilerParams(
            dimension_semantics=("arbitrary", "arbitrary", "arbitrary"), vmem_limit_bytes=VMEM_LIMIT),
        name="gather",
    )(starts.reshape(-1), pos4, gate_t, h2)


FF_SUB = 256


def _ffn_kernel(xs_ref, wg_ref, wu_ref, wd_ref, gf_ref, ye_ref, acc_ref):
    bg = pl.program_id(0)
    e = pl.program_id(1)
    f = pl.program_id(2)
    ne = pl.num_programs(1)
    ns, cap, d = ye_ref.shape[0], ye_ref.shape[2], ye_ref.shape[3]
    xs = xs_ref[:, 0, :, 0:d].reshape(ns * cap, d)
    tf = wg_ref.shape[2]

    @pl.when(f == 0)
    def _():
        acc_ref[...] = jnp.zeros_like(acc_ref)

    acc = acc_ref[...]
    for c0 in range(0, tf, FF_SUB):
        a = _dot(xs, wg_ref[0, :, c0:c0 + FF_SUB].astype(BF16))
        u = _dot(xs, wu_ref[0, :, c0:c0 + FF_SUB].astype(BF16))
        acc = acc + _dot((_silu(a) * u).astype(BF16), wd_ref[0, c0:c0 + FF_SUB, :].astype(BF16))
    acc_ref[...] = acc
    for s in range(ns):
        aux = xs_ref[s, 0, :, d:d + AUX].astype(F32)
        lane = lax.broadcasted_iota(I32, aux.shape, 1)
        mine = jnp.where(lane < 3 * ne, jnp.where(lane % ne == e, aux, 0.0), 0.0)
        gate = jnp.sum(mine, axis=-1, keepdims=True)
        ye_ref[s, 0] = (acc[s * cap:(s + 1) * cap] * gate * gf_ref[pl.ds(bg * ns + s, 1), :]).astype(BF16)


def _ffn(xs, w_gate, w_up, w_down, mod, tf, ns):
    b, ne, cap, daux = xs.shape
    d = daux - AUX
    ff = w_gate.shape[2]
    return pl.pallas_call(
        _ffn_kernel,
        out_shape=jax.ShapeDtypeStruct((b, ne, cap, d), BF16),
        grid=(b // ns, ne, ff // tf),
        in_specs=[pl.BlockSpec((ns, 1, cap, daux), lambda bi, e, f: (bi, e, 0, 0)),
                  pl.BlockSpec((1, d, tf), lambda bi, e, f: (e, 0, f)),
                  pl.BlockSpec((1, d, tf), lambda bi, e, f: (e, 0, f)),
                  pl.BlockSpec((1, tf, d), lambda bi, e, f: (e, f, 0)),
                  pl.BlockSpec((mod.shape[0], d), lambda bi, e, f: (0, N_MOD - 1))],
        out_specs=pl.BlockSpec((ns, 1, cap, d), lambda bi, e, f: (bi, e, 0, 0)),
        scratch_shapes=[pltpu.VMEM((ns * cap, d), F32)],
        compiler_params=pltpu.CompilerParams(dimension_semantics=("arbitrary", "arbitrary", "arbitrary"),
                                             vmem_limit_bytes=VMEM_LIMIT),
        name="ffn",
    )(xs, w_gate, w_up, w_down, mod)


def _combine_kernel(win, starts_ref, post_ref, ye_ref, x1_ref, fw_ref, out_ref, acc_ref, stack_ref):
    b = pl.program_id(0)
    ne, cap = ye_ref.shape[1], ye_ref.shape[2]
    row = lax.broadcasted_iota(I32, (win, TOK_BLOCK), 0)
    n_blk = x1_ref.shape[1] // TOK_BLOCK

    for blk in range(n_blk):
        j = pl.program_id(1) * n_blk + blk
        toks = slice(blk * TOK_BLOCK, (blk + 1) * TOK_BLOCK)
        acc_ref[...] = x1_ref[0, toks, :]
        s_lo, s_hi, a0, n_pass = _slot_windows(starts_ref, b, 0, ne, ne, j, win)

        def one_pass(p, carry, j=j, s_lo=s_lo, s_hi=s_hi, a0=a0):
            onehots = []
            for e in range(ne):
                a_c, lo_i, hi_i = _window(s_lo[e], s_hi[e], a0[e], p, win, cap)
                prow = post_ref[0, e, pl.ds(j, 1), :]
                owned = jnp.where(prow >= lo_i, jnp.where(prow < hi_i, prow, -1), -1)
                onehots.append(jnp.where(owned == row + a_c, 1.0, 0.0).astype(BF16))
                stack_ref[e * win:(e + 1) * win, :] = ye_ref[0, e, pl.ds(a_c, win), :]
            acc_ref[...] += _dot_tn(jnp.concatenate(onehots, axis=0), stack_ref[...])
            return carry

        lax.fori_loop(0, n_pass, one_pass, 0)
        out_ref[0, toks, :] = _rms(acc_ref[...], fw_ref[...])


def _combine(starts, pos, ye, x1, final_w, win):
    b, t, d = x1.shape
    ne, cap = ye.shape[1], ye.shape[2]
    ntb = t // TOK_BLOCK
    pos4 = pos.reshape(b, ne, ntb, TOK_BLOCK)
    step = _step_tokens(t, COMBINE_STEP_BLOCKS)
    grid_spec = pltpu.PrefetchScalarGridSpec(
        num_scalar_prefetch=1,
        grid=(b, t // step),
        in_specs=[pl.BlockSpec((1, ne, ntb, TOK_BLOCK), lambda bi, j, s: (bi, 0, 0, 0)),
                  pl.BlockSpec((1, ne, cap, d), lambda bi, j, s: (bi, 0, 0, 0),
                               pipeline_mode=pl.Buffered(1)),
                  pl.BlockSpec((1, step, d), lambda bi, j, s: (bi, j, 0)),
                  pl.BlockSpec((1, d), lambda bi, j, s: (0, 0))],
        out_specs=pl.BlockSpec((1, step, d), lambda bi, j, s: (bi, j, 0)),
        scratch_shapes=[pltpu.VMEM((TOK_BLOCK, d), F32), pltpu.VMEM((ne * win, d), BF16)])
    return pl.pallas_call(
        functools.partial(_combine_kernel, win),
        out_shape=jax.ShapeDtypeStruct((b, t, d), F32),
        grid_spec=grid_spec,
        compiler_params=pltpu.CompilerParams(
            dimension_semantics=("arbitrary", "arbitrary"), vmem_limit_bytes=VMEM_LIMIT),
        name="combine",
    )(starts.reshape(-1), pos4, ye, x1, final_w.reshape(1, d))


def kernel(x, c, ctx, c_ctx, ada_w, ada_b, norm_mix_w, norm_ffn_w, w_in, hgrn_lb_logits, hgrn_norm_w,
           sgu_norm_w, sgu_w, sgu_b, w_branch_a, w_branch_b, w_out, router_w, expert_w_gate,
           expert_w_up, expert_w_down, final_norm_w):
    b, t, d = x.shape
    assert b + 1 <= 8 and t % TOK_BLOCK == 0 and d % LANES == 0
    ne = router_w.shape[-1]
    cap = CAPACITY_FACTOR * t // ne
    win = min(cap, 64)
    assert cap % SLOT_ALIGN == 0 and t // LANES < LANES and 3 * ne <= AUX
    (x1, h2, aff_t), mod = _front(x, c, ctx, c_ctx, ada_w, ada_b, norm_mix_w, norm_ffn_w, w_in,
                                  hgrn_lb_logits, hgrn_norm_w, sgu_norm_w, sgu_w, sgu_b,
                                  w_branch_a, w_branch_b, w_out, router_w, 512)
    pos, gate, starts = _route(aff_t, cap)
    xs = _gather(starts, pos, gate, h2, cap, win, 8)
    ye = _ffn(xs, expert_w_gate[0], expert_w_up[0], expert_w_down[0], mod, 1024, 1)
    return _combine(starts, pos, ye, x1, final_norm_w, win)
```

```python
import functools

import numpy as np
import jax
import jax.numpy as jnp
from jax import lax
from jax.experimental import pallas as pl
from jax.experimental.pallas import tpu as pltpu

F32 = jnp.float32
BF16 = jnp.bfloat16
I32 = jnp.int32

LANES = 128
EPS = 1e-6
N_MOD = 6
HEADS = 4
HD = 128
WIDTH = HEADS * HD
CHUNK = 128
N_EXPERTS = 16
CAPACITY_FACTOR = 2
LEVELS = tuple(CHUNK >> (i + 1) for i in range(7))
N_DMAT = 1 + len(LEVELS) - 1
PAIR = 2 * HD
N_PAIRS = HEADS // 2
ROW_GROUP = 16
LOG2E = 1.4426950408889634
VMEM_LIMIT = 52 * 1024 * 1024


def _dot(a, b):
    return jnp.dot(a, b, preferred_element_type=F32)


def _dot_nt(a, b):
    return lax.dot_general(a, b, (((1,), (1,)), ((), ())), preferred_element_type=F32)


def _dot_tn(a, b):
    return lax.dot_general(a, b, (((0,), (0,)), ((), ())), preferred_element_type=F32)


def _split2(x):
    hi = x.astype(BF16)
    lo = (x - hi.astype(F32)).astype(BF16)
    return hi, lo


def _split3(x):
    hi = x.astype(BF16)
    r = x - hi.astype(F32)
    mid = r.astype(BF16)
    lo = (r - mid.astype(F32)).astype(BF16)
    return hi, mid, lo


def _sigmoid(x):
    return 1.0 / (1.0 + jnp.exp(-x))


def _silu(x):
    return x * _sigmoid(x)


def _gelu_tanh(x):
    c = np.sqrt(2.0 / np.pi).astype(np.float32)
    return 0.5 * x * (1.0 + jnp.tanh(c * (x + 0.044715 * (x * x * x))))


def _rms(x, w):
    return x * lax.rsqrt(jnp.mean(x * x, axis=-1, keepdims=True) + EPS) * w


def _lower_bound(lbl, d):
    l0 = lbl[2 * d:2 * d + 1, :]
    l1 = lbl[2 * d + 1:2 * d + 2, :]
    m = jnp.maximum(l0, l1)
    e0 = jnp.exp(l0 - m)
    e1 = jnp.exp(l1 - m)
    return e0 / (e0 + e1)


def _forget(z, lb):
    f = lb + (1.0 - lb) * _sigmoid(z)
    return jnp.log(f), 1.0 - f


def _cumsum_dot(m_bf16, x):
    hi, mid, lo = _split3(x)
    return _dot(m_bf16, hi) + _dot(m_bf16, mid) + _dot(m_bf16, lo)


def _same_head(shape):
    r = lax.broadcasted_iota(I32, shape, 0) < HD
    c = lax.broadcasted_iota(I32, shape, 1) < HD
    return r == c


def _block_diag(x):
    first = lax.broadcasted_iota(I32, x.shape, 1) < HD
    zero = jnp.zeros_like(x)
    return jnp.concatenate([jnp.where(first, x, zero), jnp.where(first, zero, x)], axis=0)


def _pair_outer(v_pair, w_pair):
    full = _dot_tn(v_pair, w_pair)
    return jnp.where(_same_head(full.shape), full, 0.0)


def _decay_matrices(reverse):
    c = CHUNK
    i = np.arange(c)[:, None]
    m = np.arange(c)[None, :]
    mats = []
    mats.append(m >= i if reverse else m <= i)
    for half in LEVELS[:-1]:
        a = (i // (2 * half)) * (2 * half)
        mid = a + half
        if not reverse:
            qside = i >= mid
            mat = np.where(qside, (m >= mid) & (m <= i), (m > i) & (m < mid))
        else:
            qside = i < mid
            mat = np.where(qside, (m >= i) & (m < mid), (m >= mid) & (m < i))
        mats.append(mat)
    return np.concatenate(mats, axis=0).astype(np.float32)


def _block_decay_matrix(n, reverse):
    i = np.arange(n)[:, None]
    m = np.arange(n)[None, :]
    return ((m < i) if reverse else (m > i)).astype(np.float32)


def _adaln_kernel(cond_ref, w_ref, b_ref, out_ref):
    s = _silu(cond_ref[...])
    s_hi, s_lo = _split2(s)
    w_hi, w_lo = _split2(w_ref[...])
    out_ref[...] = _dot(s_hi, w_hi) + _dot(s_hi, w_lo) + _dot(s_lo, w_hi) + b_ref[...]


def _adaln(cond, ada_w, ada_b):
    rows, d = cond.shape
    n = ada_w.shape[1]
    tn = 1024
    return pl.pallas_call(
        _adaln_kernel,
        out_shape=jax.ShapeDtypeStruct((rows, n), F32),
        grid=(n // tn,),
        in_specs=[pl.BlockSpec((rows, d), lambda j: (0, 0)),
                  pl.BlockSpec((d, tn), lambda j: (0, j)),
                  pl.BlockSpec((1, tn), lambda j: (0, j))],
        out_specs=pl.BlockSpec((rows, tn), lambda j: (0, j)),
        compiler_params=pltpu.CompilerParams(dimension_semantics=("arbitrary",),
                                             vmem_limit_bytes=VMEM_LIMIT),
        name="adaln",
    )(cond, ada_w, ada_b.reshape(1, n))


def _mod_rows(mod_ref, row, d):
    return [mod_ref[pl.ds(row, 1), j * d:(j + 1) * d] for j in range(N_MOD)]


def _ctx_kernel(ctx_row, x_ref, mod_ref, nw_ref, w_ref, lbl_ref, mf_ref, mb_ref, sf_ref, sb_ref):
    d = x_ref.shape[-1]
    x = x_ref[0]
    sh, sc = mod_ref[pl.ds(ctx_row, 1), 0:d], mod_ref[pl.ds(ctx_row, 1), d:2 * d]
    h = _rms(x, nw_ref[...]) * (1.0 + sc) + sh
    p = _dot(h.astype(BF16), w_ref[...])
    v = p[:, 2 * WIDTH:3 * WIDTH].astype(BF16)
    lbl = lbl_ref[...]
    for dirn, (m_ref, out_ref) in enumerate(((mf_ref, sf_ref), (mb_ref, sb_ref))):
        logf, k = _forget(p[:, dirn * WIDTH:(dirn + 1) * WIDTH], _lower_bound(lbl, dirn))
        w = (k * jnp.exp(_cumsum_dot(m_ref[...], logf))).astype(BF16)
        for pr in range(N_PAIRS):
            sl = slice(pr * PAIR, (pr + 1) * PAIR)
            out_ref[0, pr] = _pair_outer(v[:, sl], w[:, sl])


def _ctx_states(ctx, mod, norm_w, w_c, lbl, ctx_row):
    b, l, d = ctx.shape
    mf = jnp.asarray(_block_decay_matrix(l, False), BF16)
    mb = jnp.asarray(_block_decay_matrix(l, True), BF16)
    full = lambda a: pl.BlockSpec(a.shape, lambda i: (0,) * a.ndim)
    st = jax.ShapeDtypeStruct((b, N_PAIRS, PAIR, PAIR), F32)
    st_spec = pl.BlockSpec((1, N_PAIRS, PAIR, PAIR), lambda i: (i, 0, 0, 0))
    return pl.pallas_call(
        functools.partial(_ctx_kernel, ctx_row),
        out_shape=(st, st),
        grid=(b,),
        in_specs=[pl.BlockSpec((1, l, d), lambda i: (i, 0, 0)), full(mod), full(norm_w),
                  full(w_c), full(lbl), full(mf), full(mb)],
        out_specs=(st_spec, st_spec),
        compiler_params=pltpu.CompilerParams(dimension_semantics=("arbitrary",),
                                             vmem_limit_bytes=VMEM_LIMIT),
        name="ctx_state",
    )(ctx, mod, norm_w, w_c, lbl, mf, mb)


A_Q, A_LF, A_LB, A_V, A_G, A_U = (j * WIDTH for j in range(6))
A_COLS = 6 * WIDTH
G_SV, G_GATES = 0, WIDTH
PROJ_PIECE = 256


def _in_proj_kernel(sub, x_ref, mod_ref, nw_ref, w_ref, lbl_ref, snw_ref, m_ref, s0_ref,
                    act_ref, gat_ref, bst_ref, st_ref, kb_ref, h_ref):
    b = pl.program_id(0)
    d = x_ref.shape[-1]

    @pl.when(pl.program_id(1) == 0)
    def _():
        st_ref[...] = s0_ref[0]

    sh, sc = mod_ref[pl.ds(b, 1), 0:d], mod_ref[pl.ds(b, 1), d:2 * d]
    h_ref[...] = (_rms(x_ref[0], nw_ref[...]) * (1.0 + sc) + sh).astype(BF16)
    lbl = lbl_ref[...]

    def pieces(j):
        for c in range(0, WIDTH, PROJ_PIECE):
            yield c, _dot(h_ref[...], w_ref[:, j * WIDTH + c:j * WIDTH + c + PROJ_PIECE])

    lb_f, lb_b = _lower_bound(lbl, 0), _lower_bound(lbl, 1)
    for c, p in pieces(0):
        act_ref[0, :, A_Q + c:A_Q + c + PROJ_PIECE] = _silu(p)
    for c, p in pieces(1):
        act_ref[0, :, A_LF + c:A_LF + c + PROJ_PIECE] = _forget(p, lb_f[:, c:c + PROJ_PIECE])[0]
    for c, p in pieces(2):
        logf_b, k_b = _forget(p, lb_b[:, c:c + PROJ_PIECE])
        act_ref[0, :, A_LB + c:A_LB + c + PROJ_PIECE] = logf_b
        kb_ref[:, c:c + PROJ_PIECE] = k_b
    for c, p in pieces(3):
        act_ref[0, :, A_V + c:A_V + c + PROJ_PIECE] = p
    for c, p in pieces(4):
        act_ref[0, :, A_G + c:A_G + c + PROJ_PIECE] = _silu(p)
    for c, p in pieces(5):
        act_ref[0, :, A_U + c:A_U + c + PROJ_PIECE] = _gelu_tanh(p)
    for c, p in pieces(6):
        gat_ref[0, :, G_SV + c:G_SV + c + PROJ_PIECE] = (
            _group_rms(_gelu_tanh(p)) * snw_ref[:, c:c + PROJ_PIECE]).astype(BF16)
    for j in range(2 * d // WIDTH):
        for c, p in pieces(7 + j):
            gat_ref[0, :, G_GATES + j * WIDTH + c:G_GATES + j * WIDTH + c + PROJ_PIECE] = _sigmoid(p).astype(BF16)

    for piece in range(x_ref.shape[1] // sub - 1, -1, -1):
        rows = slice(piece * sub, (piece + 1) * sub)
        bst_ref[0, piece] = st_ref[...]
        logf = act_ref[0, rows, A_LB:A_LB + WIDTH]
        w = (kb_ref[rows, :] * jnp.exp(_cumsum_dot(m_ref[...], logf))).astype(BF16)
        v = act_ref[0, rows, A_V:A_V + WIDTH].astype(BF16)
        tot = jnp.exp(jnp.sum(logf, axis=0, keepdims=True))
        for pr in range(N_PAIRS):
            sl = slice(pr * PAIR, (pr + 1) * PAIR)
            st_ref[pr] = st_ref[pr] * tot[:, sl] + _pair_outer(v[:, sl], w[:, sl])


def _in_proj(x, mod, norm_w, w_in, lbl, snw, s_b0, tm, sub):
    b, t, d = x.shape
    nb = t // tm
    per = tm // sub
    m = jnp.asarray(_block_decay_matrix(sub, True), BF16)
    full = lambda a: pl.BlockSpec(a.shape, lambda bi, i: (0,) * a.ndim, pipeline_mode=pl.Buffered(1))
    n_gat = G_GATES + 2 * d
    return pl.pallas_call(
        functools.partial(_in_proj_kernel, sub),
        out_shape=(jax.ShapeDtypeStruct((b, t, A_COLS), F32),
                   jax.ShapeDtypeStruct((b, t, n_gat), BF16),
                   jax.ShapeDtypeStruct((b, t // sub, N_PAIRS, PAIR, PAIR), F32)),
        grid=(b, nb),
        in_specs=[pl.BlockSpec((1, tm, d), lambda bi, i: (bi, nb - 1 - i, 0)), full(mod), full(norm_w),
                  full(w_in), full(lbl), full(snw), full(m),
                  pl.BlockSpec((1, N_PAIRS, PAIR, PAIR), lambda bi, i: (bi, 0, 0, 0))],
        out_specs=(pl.BlockSpec((1, tm, A_COLS), lambda bi, i: (bi, nb - 1 - i, 0)),
                   pl.BlockSpec((1, tm, n_gat), lambda bi, i: (bi, nb - 1 - i, 0)),
                   pl.BlockSpec((1, per, N_PAIRS, PAIR, PAIR), lambda bi, i: (bi, nb - 1 - i, 0, 0, 0))),
        scratch_shapes=[pltpu.VMEM((N_PAIRS, PAIR, PAIR), F32), pltpu.VMEM((tm, WIDTH), F32),
                        pltpu.VMEM((tm, d), BF16)],
        compiler_params=pltpu.CompilerParams(dimension_semantics=("arbitrary", "arbitrary"),
                                             vmem_limit_bytes=VMEM_LIMIT),
        name="in_proj",
    )(x, mod, norm_w, w_in, lbl, snw, m, s_b0)


def _level_masks(reverse):
    row = lax.broadcasted_iota(I32, (CHUNK, PAIR), 0)
    col = lax.broadcasted_iota(I32, (CHUNK, PAIR), 1) & (CHUNK - 1)
    x = row ^ col
    out = []
    for half in LEVELS:
        in_pair = jnp.where(x >= half, jnp.where(x < 2 * half, 1.0, 0.0), 0.0)
        bit = (col if reverse else row) & half
        out.append((jnp.where(bit != 0, in_pair, 0.0), (row & half) != 0))
    return out


def _hgrn_scores(q, logf, dm_ref, masks, reverse, pm_ref, dio_ref, k_ref):
    lf2 = logf * LOG2E
    f = jnp.exp2(lf2)
    k = 1.0 - f
    k_ref[...] = k
    hi, lo = _split2(lf2)
    dall = _dot(dm_ref[...], jnp.concatenate([hi, lo], axis=0))
    dio_ref[...] = dall[0:CHUNK]
    n_groups = CHUNK // ROW_GROUP
    for pr in range(N_PAIRS):
        sl = slice(pr * PAIR, (pr + 1) * PAIR)
        qp, kp = q[:, sl], k[:, sl]
        pm = [jnp.zeros((ROW_GROUP, PAIR), F32) for _ in range(n_groups)]
        for li, half in enumerate(LEVELS):
            mask, row_bit = masks[li]
            qside = jnp.logical_not(row_bit) if reverse else row_bit
            if half > 1:
                e = jnp.exp2(dall[(1 + li) * CHUNK:(2 + li) * CHUNK, sl])
                xm = (jnp.where(qside, qp, kp) * e).astype(BF16)
            else:
                xm = jnp.where(qside, qp * f[:, sl], kp).astype(BF16)
            groups = [g for g in range(n_groups)
                      if half < ROW_GROUP or (((g * ROW_GROUP) & half) != 0) != reverse]
            lhs = xm if len(groups) == n_groups else jnp.concatenate(
                [xm[g * ROW_GROUP:(g + 1) * ROW_GROUP] for g in groups], axis=0)
            gm = _dot_nt(lhs, _block_diag(xm))
            for n, g in enumerate(groups):
                pm[g] = pm[g] + (gm[n * ROW_GROUP:(n + 1) * ROW_GROUP]
                                 * mask[g * ROW_GROUP:(g + 1) * ROW_GROUP])
        pm_ref[pr] = jnp.concatenate(pm, axis=0).astype(BF16)


def _hgrn_apply(q, v, reverse, pm_ref, dio_ref, k_ref, state_ref, o_ref, r0):
    k = k_ref[...]
    d_in = dio_ref[...]
    tot_row = 0 if reverse else CHUNK - 1
    d_tot = d_in[tot_row:tot_row + 1]
    e_in = jnp.exp2(d_in)
    e_st = jnp.exp2(d_tot - d_in)
    e_tot = e_in[tot_row:tot_row + 1]
    first = lax.broadcasted_iota(I32, (CHUNK, PAIR), 1) < HD
    for pr in range(N_PAIRS):
        sl = slice(pr * PAIR, (pr + 1) * PAIR)
        qp, kp, vp = q[:, sl], k[:, sl], v[:, sl]
        qk = qp * kp
        diag = jnp.where(first, jnp.sum(qk[:, 0:HD], axis=-1, keepdims=True),
                         jnp.sum(qk[:, HD:PAIR], axis=-1, keepdims=True))
        st = state_ref[pr]
        o = (_dot(pm_ref[pr], _block_diag(vp.astype(BF16))) + diag * vp
             + _dot_nt((qp * e_in[:, sl]).astype(BF16), st.astype(BF16)))
        o_ref[pl.ds(r0, CHUNK), sl] += o
        state_ref[pr] = st * e_tot[:, sl] + _pair_outer(vp.astype(BF16), (kp * e_st[:, sl]).astype(BF16))


def _group_rms(x):
    return jnp.concatenate(
        [x[:, g * HD:(g + 1) * HD]
         * lax.rsqrt(jnp.mean(x[:, g * HD:(g + 1) * HD] ** 2, axis=-1, keepdims=True) + EPS)
         for g in range(x.shape[1] // HD)], axis=-1)


def _mixer_kernel(act_ref, gat_ref, x_ref, mod_ref, nffn_ref, hnw_ref,
                  sw_ref, sbias_ref, wa_ref, wb_ref, wo_ref, rw_ref, dmf_ref, dmb_ref,
                  sf0_ref, bst_ref,
                  x1_ref, h2_ref, aff_ref,
                  o_ref, sg_ref, stf_ref, stb_ref, pm_ref, dio_ref, k_ref):
    b = pl.program_id(0)
    i = pl.program_id(1)
    tb, d = x_ref.shape[1], x_ref.shape[2]
    nch = tb // CHUNK
    _, _, g_m, sh_f, sc_f, _ = _mod_rows(mod_ref, b, d)
    act = lambda rows, c0: act_ref[0, rows, c0:c0 + WIDTH]
    every = slice(None)

    @pl.when(i == 0)
    def _():
        stf_ref[...] = sf0_ref[0]

    stb_ref[...] = bst_ref[0, 0]

    o_ref[...] = jnp.zeros_like(o_ref)
    work = []
    for reverse in (False, True):
        work += [(reverse, ci) for ci in (range(nch - 1, -1, -1) if reverse else range(nch))]
    masks = {reverse: _level_masks(reverse) for reverse in (False, True)}
    for n, (reverse, ci) in enumerate(work):
        rows = pl.ds(ci * CHUNK, CHUNK)
        a_l, dm_ref = (A_LB, dmb_ref) if reverse else (A_LF, dmf_ref)
        _hgrn_scores(act(rows, A_Q), act(rows, a_l), dm_ref, masks[reverse], reverse,
                     pm_ref.at[n], dio_ref.at[n], k_ref.at[n])
    for n, (reverse, ci) in enumerate(work):
        rows = pl.ds(ci * CHUNK, CHUNK)
        st_ref = stb_ref if reverse else stf_ref
        _hgrn_apply(act(rows, A_Q), act(rows, A_V), reverse, pm_ref.at[n], dio_ref.at[n], k_ref.at[n],
                    st_ref, o_ref, ci * CHUNK)

    a_in = _group_rms(o_ref[...]) * hnw_ref[...] * act(every, A_G)
    y_a = _dot(a_in.astype(BF16), wa_ref[...])

    for ci in range(nch):
        rows = slice(ci * CHUNK, (ci + 1) * CHUNK)
        for g in range(HEADS):
            sl = slice(g * HD, (g + 1) * HD)
            sg_ref[rows, sl] = (_dot(sw_ref[g], gat_ref[0, rows, G_SV + g * HD:G_SV + (g + 1) * HD])
                                + sbias_ref[:, sl])
    y_b = _dot((act(every, A_U) * sg_ref[...]).astype(BF16), wb_ref[...])

    merged = (gat_ref[0, :, G_GATES:G_GATES + d].astype(F32) * y_a
              + gat_ref[0, :, G_GATES + d:G_GATES + 2 * d].astype(F32) * y_b)
    y = _dot(merged.astype(BF16), wo_ref[...])
    x1 = x_ref[0] + g_m * y
    x1_ref[0] = x1

    h2 = _rms(x1, nffn_ref[...]) * (1.0 + sc_f) + sh_f
    h2_hi, h2_lo = _split2(h2)
    h2_ref[0] = h2_hi
    rw_hi, rw_lo = _split2(rw_ref[...])
    logits = _dot_nt(rw_hi, h2_hi) + _dot_nt(rw_hi, h2_lo) + _dot_nt(rw_lo, h2_hi)
    mx = jnp.max(logits, axis=0, keepdims=True)
    ex = jnp.exp(logits - mx)
    aff_ref[0] = ex / jnp.sum(ex, axis=0, keepdims=True)


def _mixer(act, gat, x, mod, nffn, hnw, sgu_w, sgu_bias, w_a, w_b, w_o, rw_t, s_f0, bstates, tb):
    b, t, d = x.shape
    nb = t // tb
    per_state = bstates.shape[1] // nb
    ne = rw_t.shape[0]
    dmf = jnp.asarray(np.tile(_decay_matrices(False), (1, 2)), BF16)
    dmb = jnp.asarray(np.tile(_decay_matrices(True), (1, 2)), BF16)
    const = lambda a: pl.BlockSpec(a.shape, lambda bi, i: (0,) * a.ndim, pipeline_mode=pl.Buffered(1))
    in_specs = [pl.BlockSpec((1, tb, act.shape[2]), lambda bi, i: (bi, i, 0)),
                pl.BlockSpec((1, tb, gat.shape[2]), lambda bi, i: (bi, i, 0)),
                pl.BlockSpec((1, tb, d), lambda bi, i: (bi, i, 0)),
                const(mod), const(nffn), const(hnw),
                const(sgu_w), const(sgu_bias), const(w_a), const(w_b), const(w_o), const(rw_t),
                const(dmf), const(dmb),
                pl.BlockSpec((1, N_PAIRS, PAIR, PAIR), lambda bi, i: (bi, 0, 0, 0)),
                pl.BlockSpec((1, 1, N_PAIRS, PAIR, PAIR), lambda bi, i: (bi, (i + 1) * per_state - 1, 0, 0, 0))]
    out_shape = (jax.ShapeDtypeStruct((b, t, d), F32),
                 jax.ShapeDtypeStruct((b, t, d), BF16),
                 jax.ShapeDtypeStruct((b, ne, t), F32))
    out_specs = (pl.BlockSpec((1, tb, d), lambda bi, i: (bi, i, 0)),
                 pl.BlockSpec((1, tb, d), lambda bi, i: (bi, i, 0)),
                 pl.BlockSpec((1, ne, tb), lambda bi, i: (bi, 0, i)))
    return pl.pallas_call(
        _mixer_kernel,
        out_shape=out_shape,
        grid=(b, nb),
        in_specs=in_specs,
        out_specs=out_specs,
        scratch_shapes=[pltpu.VMEM((tb, WIDTH), F32),
                        pltpu.VMEM((tb, WIDTH), F32),
                        pltpu.VMEM((N_PAIRS, PAIR, PAIR), F32),
                        pltpu.VMEM((N_PAIRS, PAIR, PAIR), F32),
                        pltpu.VMEM((2 * (tb // CHUNK), N_PAIRS, CHUNK, PAIR), BF16),
                        pltpu.VMEM((2 * (tb // CHUNK), CHUNK, WIDTH), F32),
                        pltpu.VMEM((2 * (tb // CHUNK), CHUNK, WIDTH), F32)],
        compiler_params=pltpu.CompilerParams(dimension_semantics=("arbitrary", "arbitrary"),
                                             vmem_limit_bytes=VMEM_LIMIT),
        name="mixer",
    )(act, gat, x, mod, nffn, hnw, sgu_w, sgu_bias, w_a, w_b, w_o, rw_t, dmf, dmb, s_f0, bstates)


def _front(x, c, ctx, c_ctx, ada_w, ada_b, norm_mix_w, norm_ffn_w, w_in, hgrn_lb_logits,
           hgrn_norm_w, sgu_norm_w, sgu_w, sgu_b, w_branch_a, w_branch_b, w_out, router_w, tb):
    b, t, d = x.shape
    layer = 0
    cond = jnp.zeros((8, d), F32).at[0:b].set(c).at[b].set(c_ctx)
    mod = _adaln(cond, ada_w[layer], ada_b[layer])
    lbl = hgrn_lb_logits[:, layer:layer + 2, :].reshape(4, WIDTH)
    nmix = norm_mix_w[layer].reshape(1, d)
    nffn = norm_ffn_w[layer].reshape(1, d)
    w_in_b = w_in[layer].astype(BF16)
    s_f0, s_b0 = _ctx_states(ctx, mod, nmix, w_in_b[:, WIDTH:4 * WIDTH], lbl, b)
    act, gat, bstates = _in_proj(x, mod, nmix, w_in_b, lbl, sgu_norm_w[layer].reshape(1, WIDTH), s_b0,
                                 min(t, 512), 256)
    sgu_bias = jnp.repeat(sgu_b[layer].T, HD, axis=1)
    return _mixer(act, gat, x, mod, nffn, hgrn_norm_w[layer].reshape(1, WIDTH),
                  sgu_w[layer].astype(BF16), sgu_bias,
                  w_branch_a[layer].astype(BF16), w_branch_b[layer].astype(BF16),
                  w_out[layer].astype(BF16), router_w[layer].T, s_f0, bstates, tb), mod


def _route_kernel(cap, aff_ref, pos_ref, gate_ref, starts_ref):
    a = aff_ref[0]
    ne, t = a.shape
    nblk = t // LANES
    bits = pltpu.bitcast(a, I32)

    def search(it, lo):
        cand = lo | (jnp.int32(1) << (30 - it))
        cnt = jnp.sum(jnp.where(bits >= cand, 1.0, 0.0), axis=-1, keepdims=True)
        return jnp.where(cnt >= cap, cand, lo)

    thr = lax.fori_loop(0, 31, search, jnp.zeros((ne, 1), I32))
    gt = bits > thr
    eq = bits == thr
    n_ties_wanted = cap - jnp.sum(jnp.where(gt, 1.0, 0.0), axis=-1, keepdims=True)

    row = lax.broadcasted_iota(I32, (LANES, LANES), 0)
    col = lax.broadcasted_iota(I32, (LANES, LANES), 1)
    upper = jnp.where(row <= col, 1.0, 0.0).astype(BF16)
    lane = lax.broadcasted_iota(I32, (ne, LANES), 1)

    off = jnp.zeros((ne, 1), F32)
    sel_blocks = []
    for j in range(nblk):
        sl = slice(j * LANES, (j + 1) * LANES)
        eqf = jnp.where(eq[:, sl], 1.0, 0.0)
        incl = _dot(eqf.astype(BF16), upper) + off
        keep_tie = jnp.where(incl - eqf < n_ties_wanted, eqf, 0.0)
        sel_blocks.append(jnp.where(gt[:, sl], 1.0, keep_tie))
        off = incl[:, LANES - 1:LANES]

    off = jnp.zeros((ne, 1), F32)
    starts = jnp.zeros((ne, LANES), F32)
    for j in range(nblk):
        sl = slice(j * LANES, (j + 1) * LANES)
        self = sel_blocks[j]
        starts = jnp.where(lane == j, off, starts)
        incl = _dot(self.astype(BF16), upper) + off
        pos_ref[0, :, sl] = jnp.where(self > 0.0, incl - 1.0, -1.0).astype(I32)
        for k, piece in enumerate(_split3(jnp.where(self > 0.0, a[:, sl], 0.0))):
            gate_ref[0, k * ne:(k + 1) * ne, sl] = piece.astype(F32)
        off = incl[:, LANES - 1:LANES]
    starts = jnp.where(lane >= nblk, off, starts)
    starts_ref[0] = starts.astype(I32)


def _route(aff_t, cap):
    b, ne, t = aff_t.shape
    spec = pl.BlockSpec((1, ne, t), lambda i: (i, 0, 0))
    return pl.pallas_call(
        functools.partial(_route_kernel, cap),
        out_shape=(jax.ShapeDtypeStruct((b, ne, t), I32),
                   jax.ShapeDtypeStruct((b, 3 * ne, t), F32),
                   jax.ShapeDtypeStruct((b, ne, LANES), I32)),
        grid=(b,),
        in_specs=[spec],
        out_specs=(spec, pl.BlockSpec((1, 3 * ne, t), lambda i: (i, 0, 0)),
                   pl.BlockSpec((1, ne, LANES), lambda i: (i, 0, 0))),
        compiler_params=pltpu.CompilerParams(dimension_semantics=("arbitrary",),
                                             vmem_limit_bytes=VMEM_LIMIT),
        name="route",
    )(aff_t)


TOK_BLOCK = 2 * LANES
GATHER_STEP_BLOCKS = 8
COMBINE_STEP_BLOCKS = 2


def _step_tokens(t, blocks):
    return blocks * TOK_BLOCK if t % (blocks * TOK_BLOCK) == 0 else TOK_BLOCK


SLOT_ALIGN = 16
AUX = LANES


def _slot_windows(starts_ref, b, e0, n_exp, ne, j, win):
    per = TOK_BLOCK // LANES
    s_lo, s_hi, a0 = [], [], []
    n_pass = jnp.int32(0)
    for le in range(n_exp):
        base = (b * ne + e0 + le) * LANES
        lo = starts_ref[base + per * j]
        hi = starts_ref[base + per * (j + 1)]
        a = (lo // SLOT_ALIGN) * SLOT_ALIGN
        n_pass = jnp.maximum(n_pass, jnp.where(hi > lo, (hi - a + win - 1) // win, 0))
        s_lo.append(lo)
        s_hi.append(hi)
        a0.append(a)
    return s_lo, s_hi, a0, n_pass


def _window(s_lo, s_hi, a0, p, win, cap):
    a = a0 + p * win
    a_c = pl.multiple_of(jnp.minimum(a, cap - win), SLOT_ALIGN)
    return a_c, jnp.maximum(s_lo, a), jnp.minimum(s_hi, a + win)


def _gather_kernel(win, starts_ref, pos_ref, gate_ref, h2_ref, xs_ref):
    b = pl.program_id(0)
    g = pl.program_id(1)
    eg, cap = xs_ref.shape[1], xs_ref.shape[2]
    ne = pl.num_programs(1) * eg
    d = h2_ref.shape[2]

    @pl.when(pl.program_id(2) == 0)
    def _():
        xs_ref[...] = jnp.zeros_like(xs_ref)

    row = lax.broadcasted_iota(I32, (win, TOK_BLOCK), 0)
    row_d = lax.broadcasted_iota(I32, (win, d), 0)
    row_aux = lax.broadcasted_iota(I32, (win, AUX), 0)

    for blk in range(h2_ref.shape[1] // TOK_BLOCK):
        j = pl.program_id(2) * (h2_ref.shape[1] // TOK_BLOCK) + blk
        toks = slice(blk * TOK_BLOCK, (blk + 1) * TOK_BLOCK)
        s_lo, s_hi, a0, n_pass = _slot_windows(starts_ref, b, g * eg, eg, ne, j, win)

        def one_pass(p, carry, j=j, toks=toks, s_lo=s_lo, s_hi=s_hi, a0=a0):
            wins = [_window(s_lo[le], s_hi[le], a0[le], p, win, cap) for le in range(eg)]
            onehots = []
            for le, (a_c, lo_i, hi_i) in enumerate(wins):
                prow = pos_ref[0, le, pl.ds(j, 1), :]
                owned = jnp.where(prow >= lo_i, jnp.where(prow < hi_i, prow, -1), -1)
                onehots.append(jnp.where(owned == row + a_c, 1.0, 0.0).astype(BF16))
            lhs = jnp.concatenate(onehots, axis=0)
            res = _dot(lhs, h2_ref[0, toks, :])
            res_aux = _dot(lhs, gate_ref[0, toks, :])
            for le, (a_c, lo_i, hi_i) in enumerate(wins):
                rows = pl.ds(a_c, win)
                mine = slice(le * win, (le + 1) * win)
                xs_ref[0, le, rows, 0:d] = jnp.where(row_d + a_c < lo_i, xs_ref[0, le, rows, 0:d],
                                                     res[mine].astype(BF16))
                xs_ref[0, le, rows, d:d + AUX] = jnp.where(row_aux + a_c < lo_i,
                                                           xs_ref[0, le, rows, d:d + AUX],
                                                           res_aux[mine].astype(BF16))
            return carry

        lax.fori_loop(0, n_pass, one_pass, 0)


def _gather(starts, pos, gate3, h2, cap, win, eg):
    b, t, d = h2.shape
    ne = pos.shape[1]
    ntb = t // TOK_BLOCK
    pos4 = pos.reshape(b, ne, ntb, TOK_BLOCK)
    gate_t = jnp.pad(jnp.swapaxes(gate3, 1, 2).astype(BF16), ((0, 0), (0, 0), (0, AUX - gate3.shape[1])))
    step = _step_tokens(t, GATHER_STEP_BLOCKS)
    grid_spec = pltpu.PrefetchScalarGridSpec(
        num_scalar_prefetch=1,
        grid=(b, ne // eg, t // step),
        in_specs=[pl.BlockSpec((1, eg, ntb, TOK_BLOCK), lambda bi, g, j, s: (bi, g, 0, 0)),
                  pl.BlockSpec((1, step, AUX), lambda bi, g, j, s: (bi, j, 0)),
                  pl.BlockSpec((1, step, d), lambda bi, g, j, s: (bi, j, 0))],
        out_specs=pl.BlockSpec((1, eg, cap, d + AUX), lambda bi, g, j, s: (bi, g, 0, 0)))
    return pl.pallas_call(
        functools.partial(_gather_kernel, win),
        out_shape=jax.ShapeDtypeStruct((b, ne, cap, d + AUX), BF16),
        grid_spec=grid_spec,
        compiler_params=pltpu.CompilerParams(
            dimension_semantics=("arbitrary", "arbitrary", "arbitrary"), vmem_limit_bytes=VMEM_LIMIT),
        name="gather",
    )(starts.reshape(-1), pos4, gate_t, h2)


FF_SUB = 256


def _ffn_kernel(xs_ref, wg_ref, wu_ref, wd_ref, gf_ref, ye_ref, acc_ref):
    bg = pl.program_id(0)
    e = pl.program_id(1)
    f = pl.program_id(2)
    ne = pl.num_programs(1)
    ns, cap, d = ye_ref.shape[0], ye_ref.shape[2], ye_ref.shape[3]
    xs = xs_ref[:, 0, :, 0:d].reshape(ns * cap, d)
    tf = wg_ref.shape[2]

    def run(acc):
        for c0 in range(0, tf, FF_SUB):
            a = _dot(xs, wg_ref[0, :, c0:c0 + FF_SUB].astype(BF16))
            u = _dot(xs, wu_ref[0, :, c0:c0 + FF_SUB].astype(BF16))
            down = _dot((_silu(a) * u).astype(BF16), wd_ref[0, c0:c0 + FF_SUB, :].astype(BF16))
            acc = down if acc is None else acc + down
        acc_ref[...] = acc
        for s in range(ns):
            aux = xs_ref[s, 0, :, d:d + AUX].astype(F32)
            lane = lax.broadcasted_iota(I32, aux.shape, 1)
            mine = jnp.where(lane < 3 * ne, jnp.where(lane % ne == e, aux, 0.0), 0.0)
            gate = jnp.sum(mine, axis=-1, keepdims=True)
            ye_ref[s, 0] = (acc[s * cap:(s + 1) * cap] * gate
                            * gf_ref[pl.ds(bg * ns + s, 1), :]).astype(BF16)

    @pl.when(f == 0)
    def _():
        run(None)

    @pl.when(f > 0)
    def _():
        run(acc_ref[...])


def _ffn(xs, w_gate, w_up, w_down, mod, tf, ns):
    b, ne, cap, daux = xs.shape
    d = daux - AUX
    ff = w_gate.shape[2]
    return pl.pallas_call(
        _ffn_kernel,
        out_shape=jax.ShapeDtypeStruct((b, ne, cap, d), BF16),
        grid=(b // ns, ne, ff // tf),
        in_specs=[pl.BlockSpec((ns, 1, cap, daux), lambda bi, e, f: (bi, e, 0, 0)),
                  pl.BlockSpec((1, d, tf), lambda bi, e, f: (e, 0, f)),
                  pl.BlockSpec((1, d, tf), lambda bi, e, f: (e, 0, f)),
                  pl.BlockSpec((1, tf, d), lambda bi, e, f: (e, f, 0)),
                  pl.BlockSpec((mod.shape[0], d), lambda bi, e, f: (0, N_MOD - 1))],
        out_specs=pl.BlockSpec((ns, 1, cap, d), lambda bi, e, f: (bi, e, 0, 0)),
        scratch_shapes=[pltpu.VMEM((ns * cap, d), F32)],
        compiler_params=pltpu.CompilerParams(dimension_semantics=("arbitrary", "arbitrary", "arbitrary"),
                                             vmem_limit_bytes=VMEM_LIMIT),
        name="ffn",
    )(xs, w_gate, w_up, w_down, mod)


def _combine_kernel(win, starts_ref, post_ref, ye_ref, x1_ref, fw_ref, out_ref, acc_ref, stack_ref):
    b = pl.program_id(0)
    ne, cap = ye_ref.shape[1], ye_ref.shape[2]
    row = lax.broadcasted_iota(I32, (win, TOK_BLOCK), 0)
    n_blk = x1_ref.shape[1] // TOK_BLOCK

    for blk in range(n_blk):
        j = pl.program_id(1) * n_blk + blk
        toks = slice(blk * TOK_BLOCK, (blk + 1) * TOK_BLOCK)
        acc_ref[...] = x1_ref[0, toks, :]
        s_lo, s_hi, a0, n_pass = _slot_windows(starts_ref, b, 0, ne, ne, j, win)

        def one_pass(p, carry, j=j, s_lo=s_lo, s_hi=s_hi, a0=a0):
            onehots = []
            for e in range(ne):
                a_c, lo_i, hi_i = _window(s_lo[e], s_hi[e], a0[e], p, win, cap)
                prow = post_ref[0, e, pl.ds(j, 1), :]
                owned = jnp.where(prow >= lo_i, jnp.where(prow < hi_i, prow, -1), -1)
                onehots.append(jnp.where(owned == row + a_c, 1.0, 0.0).astype(BF16))
                stack_ref[e * win:(e + 1) * win, :] = ye_ref[0, e, pl.ds(a_c, win), :]
            acc_ref[...] += _dot_tn(jnp.concatenate(onehots, axis=0), stack_ref[...])
            return carry

        lax.fori_loop(0, n_pass, one_pass, 0)
        out_ref[0, toks, :] = _rms(acc_ref[...], fw_ref[...])


def _combine(starts, pos, ye, x1, final_w, win):
    b, t, d = x1.shape
    ne, cap = ye.shape[1], ye.shape[2]
    ntb = t // TOK_BLOCK
    pos4 = pos.reshape(b, ne, ntb, TOK_BLOCK)
    step = _step_tokens(t, COMBINE_STEP_BLOCKS)
    grid_spec = pltpu.PrefetchScalarGridSpec(
        num_scalar_prefetch=1,
        grid=(b, t // step),
        in_specs=[pl.BlockSpec((1, ne, ntb, TOK_BLOCK), lambda bi, j, s: (bi, 0, 0, 0)),
                  pl.BlockSpec((1, ne, cap, d), lambda bi, j, s: (bi, 0, 0, 0),
                               pipeline_mode=pl.Buffered(1)),
                  pl.BlockSpec((1, step, d), lambda bi, j, s: (bi, j, 0)),
                  pl.BlockSpec((1, d), lambda bi, j, s: (0, 0))],
        out_specs=pl.BlockSpec((1, step, d), lambda bi, j, s: (bi, j, 0)),
        scratch_shapes=[pltpu.VMEM((TOK_BLOCK, d), F32), pltpu.VMEM((ne * win, d), BF16)])
    return pl.pallas_call(
        functools.partial(_combine_kernel, win),
        out_shape=jax.ShapeDtypeStruct((b, t, d), F32),
        grid_spec=grid_spec,
        compiler_params=pltpu.CompilerParams(
            dimension_semantics=("arbitrary", "arbitrary"), vmem_limit_bytes=VMEM_LIMIT),
        name="combine",
    )(starts.reshape(-1), pos4, ye, x1, final_w.reshape(1, d))


def kernel(x, c, ctx, c_ctx, ada_w, ada_b, norm_mix_w, norm_ffn_w, w_in, hgrn_lb_logits, hgrn_norm_w,
           sgu_norm_w, sgu_w, sgu_b, w_branch_a, w_branch_b, w_out, router_w, expert_w_gate,
           expert_w_up, expert_w_down, final_norm_w):
    b, t, d = x.shape
    assert b + 1 <= 8 and t % TOK_BLOCK == 0 and d % LANES == 0
    ne = router_w.shape[-1]
    cap = CAPACITY_FACTOR * t // ne
    win = min(cap, 64)
    assert cap % SLOT_ALIGN == 0 and t // LANES < LANES and 3 * ne <= AUX
    (x1, h2, aff_t), mod = _front(x, c, ctx, c_ctx, ada_w, ada_b, norm_mix_w, norm_ffn_w, w_in,
                                  hgrn_lb_logits, hgrn_norm_w, sgu_norm_w, sgu_w, sgu_b,
                                  w_branch_a, w_branch_b, w_out, router_w, 512)
    pos, gate, starts = _route(aff_t, cap)
    xs = _gather(starts, pos, gate, h2, cap, win, 8)
    ye = _ffn(xs, expert_w_gate[0], expert_w_up[0], expert_w_down[0], mod, 1024, 1)
    return _combine(starts, pos, ye, x1, final_norm_w, win)
```

```python
import functools

import numpy as np
import jax
import jax.numpy as jnp
from jax import lax
from jax.experimental import pallas as pl
from jax.experimental.pallas import tpu as pltpu

F32 = jnp.float32
BF16 = jnp.bfloat16
I32 = jnp.int32

LANES = 128
EPS = 1e-6
N_MOD = 6
HEADS = 4
HD = 128
WIDTH = HEADS * HD
CHUNK = 128
N_EXPERTS = 16
CAPACITY_FACTOR = 2
LEVELS = tuple(CHUNK >> (i + 1) for i in range(7))
N_DMAT = 1 + len(LEVELS) - 1
PAIR = 2 * HD
N_PAIRS = HEADS // 2
ROW_GROUP = 16
LOG2E = 1.4426950408889634
VMEM_LIMIT = 52 * 1024 * 1024


def _dot(a, b):
    return jnp.dot(a, b, preferred_element_type=F32)


def _dot_nt(a, b):
    return lax.dot_general(a, b, (((1,), (1,)), ((), ())), preferred_element_type=F32)


def _dot_tn(a, b):
    return lax.dot_general(a, b, (((0,), (0,)), ((), ())), preferred_element_type=F32)


def _split2(x):
    hi = x.astype(BF16)
    lo = (x - hi.astype(F32)).astype(BF16)
    return hi, lo


def _split3(x):
    hi = x.astype(BF16)
    r = x - hi.astype(F32)
    mid = r.astype(BF16)
    lo = (r - mid.astype(F32)).astype(BF16)
    return hi, mid, lo


def _sigmoid(x):
    return 1.0 / (1.0 + jnp.exp(-x))


def _silu(x):
    return x * _sigmoid(x)


def _gelu_tanh(x):
    c = np.sqrt(2.0 / np.pi).astype(np.float32)
    return 0.5 * x * (1.0 + jnp.tanh(c * (x + 0.044715 * (x * x * x))))


def _rms(x, w):
    return x * lax.rsqrt(jnp.mean(x * x, axis=-1, keepdims=True) + EPS) * w


def _lower_bound(lbl, d):
    l0 = lbl[2 * d:2 * d + 1, :]
    l1 = lbl[2 * d + 1:2 * d + 2, :]
    m = jnp.maximum(l0, l1)
    e0 = jnp.exp(l0 - m)
    e1 = jnp.exp(l1 - m)
    return e0 / (e0 + e1)


def _forget(z, lb):
    f = lb + (1.0 - lb) * _sigmoid(z)
    return jnp.log(f), 1.0 - f


def _cumsum_dot(m_bf16, x):
    hi, mid, lo = _split3(x)
    return _dot(m_bf16, hi) + _dot(m_bf16, mid) + _dot(m_bf16, lo)


def _same_head(shape):
    r = lax.broadcasted_iota(I32, shape, 0) < HD
    c = lax.broadcasted_iota(I32, shape, 1) < HD
    return r == c


def _block_diag(x):
    first = lax.broadcasted_iota(I32, x.shape, 1) < HD
    zero = jnp.zeros_like(x)
    return jnp.concatenate([jnp.where(first, x, zero), jnp.where(first, zero, x)], axis=0)


def _pair_outer(v_pair, w_pair):
    full = _dot_tn(v_pair, w_pair)
    return jnp.where(_same_head(full.shape), full, 0.0)


def _decay_matrices(reverse):
    c = CHUNK
    i = np.arange(c)[:, None]
    m = np.arange(c)[None, :]
    mats = []
    mats.append(m >= i if reverse else m <= i)
    for half in LEVELS[:-1]:
        a = (i // (2 * half)) * (2 * half)
        mid = a + half
        if not reverse:
            qside = i >= mid
            mat = np.where(qside, (m >= mid) & (m <= i), (m > i) & (m < mid))
        else:
            qside = i < mid
            mat = np.where(qside, (m >= i) & (m < mid), (m >= mid) & (m < i))
        mats.append(mat)
    return np.concatenate(mats, axis=0).astype(np.float32)


def _block_decay_matrix(n, reverse):
    i = np.arange(n)[:, None]
    m = np.arange(n)[None, :]
    return ((m < i) if reverse else (m > i)).astype(np.float32)


def _adaln_kernel(cond_ref, w_ref, b_ref, out_ref):
    s = _silu(cond_ref[...])
    s_hi, s_lo = _split2(s)
    w_hi, w_lo = _split2(w_ref[...])
    out_ref[...] = _dot(s_hi, w_hi) + _dot(s_hi, w_lo) + _dot(s_lo, w_hi) + b_ref[...]


def _adaln(cond, ada_w, ada_b):
    rows, d = cond.shape
    n = ada_w.shape[1]
    tn = 1024
    return pl.pallas_call(
        _adaln_kernel,
        out_shape=jax.ShapeDtypeStruct((rows, n), F32),
        grid=(n // tn,),
        in_specs=[pl.BlockSpec((rows, d), lambda j: (0, 0)),
                  pl.BlockSpec((d, tn), lambda j: (0, j)),
                  pl.BlockSpec((1, tn), lambda j: (0, j))],
        out_specs=pl.BlockSpec((rows, tn), lambda j: (0, j)),
        compiler_params=pltpu.CompilerParams(dimension_semantics=("arbitrary",),
                                             vmem_limit_bytes=VMEM_LIMIT),
        name="adaln",
    )(cond, ada_w, ada_b.reshape(1, n))


def _mod_rows(mod_ref, row, d):
    return [mod_ref[pl.ds(row, 1), j * d:(j + 1) * d] for j in range(N_MOD)]


def _ctx_kernel(ctx_row, x_ref, mod_ref, nw_ref, w_ref, lbl_ref, mf_ref, mb_ref, sf_ref, sb_ref):
    d = x_ref.shape[-1]
    x = x_ref[0]
    sh, sc = mod_ref[pl.ds(ctx_row, 1), 0:d], mod_ref[pl.ds(ctx_row, 1), d:2 * d]
    h = _rms(x, nw_ref[...]) * (1.0 + sc) + sh
    p = _dot(h.astype(BF16), w_ref[...])
    v = p[:, 2 * WIDTH:3 * WIDTH].astype(BF16)
    lbl = lbl_ref[...]
    for dirn, (m_ref, out_ref) in enumerate(((mf_ref, sf_ref), (mb_ref, sb_ref))):
        logf, k = _forget(p[:, dirn * WIDTH:(dirn + 1) * WIDTH], _lower_bound(lbl, dirn))
        w = (k * jnp.exp(_cumsum_dot(m_ref[...], logf))).astype(BF16)
        for pr in range(N_PAIRS):
            sl = slice(pr * PAIR, (pr + 1) * PAIR)
            out_ref[0, pr] = _pair_outer(v[:, sl], w[:, sl])


def _ctx_states(ctx, mod, norm_w, w_c, lbl, ctx_row):
    b, l, d = ctx.shape
    mf = jnp.asarray(_block_decay_matrix(l, False), BF16)
    mb = jnp.asarray(_block_decay_matrix(l, True), BF16)
    full = lambda a: pl.BlockSpec(a.shape, lambda i: (0,) * a.ndim)
    st = jax.ShapeDtypeStruct((b, N_PAIRS, PAIR, PAIR), F32)
    st_spec = pl.BlockSpec((1, N_PAIRS, PAIR, PAIR), lambda i: (i, 0, 0, 0))
    return pl.pallas_call(
        functools.partial(_ctx_kernel, ctx_row),
        out_shape=(st, st),
        grid=(b,),
        in_specs=[pl.BlockSpec((1, l, d), lambda i: (i, 0, 0)), full(mod), full(norm_w),
                  full(w_c), full(lbl), full(mf), full(mb)],
        out_specs=(st_spec, st_spec),
        compiler_params=pltpu.CompilerParams(dimension_semantics=("arbitrary",),
                                             vmem_limit_bytes=VMEM_LIMIT),
        name="ctx_state",
    )(ctx, mod, norm_w, w_c, lbl, mf, mb)


A_Q, A_LF, A_LB, A_V, A_G, A_U = (j * WIDTH for j in range(6))
A_COLS = 6 * WIDTH
G_SV, G_GATES = 0, WIDTH
PROJ_PIECE = 256


def _in_proj_kernel(sub, x_ref, mod_ref, nw_ref, w_ref, lbl_ref, snw_ref, m_ref, s0_ref,
                    act_ref, gat_ref, bst_ref, st_ref, kb_ref, h_ref):
    b = pl.program_id(0)
    d = x_ref.shape[-1]

    @pl.when(pl.program_id(1) == 0)
    def _():
        st_ref[...] = s0_ref[0]

    sh, sc = mod_ref[pl.ds(b, 1), 0:d], mod_ref[pl.ds(b, 1), d:2 * d]
    h_ref[...] = (_rms(x_ref[0], nw_ref[...]) * (1.0 + sc) + sh).astype(BF16)
    lbl = lbl_ref[...]

    def pieces(j):
        for c in range(0, WIDTH, PROJ_PIECE):
            yield c, _dot(h_ref[...], w_ref[:, j * WIDTH + c:j * WIDTH + c + PROJ_PIECE])

    lb_f, lb_b = _lower_bound(lbl, 0), _lower_bound(lbl, 1)
    for c, p in pieces(0):
        act_ref[0, :, A_Q + c:A_Q + c + PROJ_PIECE] = _silu(p)
    for c, p in pieces(1):
        act_ref[0, :, A_LF + c:A_LF + c + PROJ_PIECE] = _forget(p, lb_f[:, c:c + PROJ_PIECE])[0]
    for c, p in pieces(2):
        logf_b, k_b = _forget(p, lb_b[:, c:c + PROJ_PIECE])
        act_ref[0, :, A_LB + c:A_LB + c + PROJ_PIECE] = logf_b
        kb_ref[:, c:c + PROJ_PIECE] = k_b
    for c, p in pieces(3):
        act_ref[0, :, A_V + c:A_V + c + PROJ_PIECE] = p
    for c, p in pieces(4):
        act_ref[0, :, A_G + c:A_G + c + PROJ_PIECE] = _silu(p)
    for c, p in pieces(5):
        act_ref[0, :, A_U + c:A_U + c + PROJ_PIECE] = _gelu_tanh(p)
    for c, p in pieces(6):
        gat_ref[0, :, G_SV + c:G_SV + c + PROJ_PIECE] = (
            _group_rms(_gelu_tanh(p)) * snw_ref[:, c:c + PROJ_PIECE]).astype(BF16)
    for j in range(2 * d // WIDTH):
        for c, p in pieces(7 + j):
            gat_ref[0, :, G_GATES + j * WIDTH + c:G_GATES + j * WIDTH + c + PROJ_PIECE] = _sigmoid(p).astype(BF16)

    for piece in range(x_ref.shape[1] // sub - 1, -1, -1):
        rows = slice(piece * sub, (piece + 1) * sub)
        bst_ref[0, piece] = st_ref[...]
        logf = act_ref[0, rows, A_LB:A_LB + WIDTH]
        w = (kb_ref[rows, :] * jnp.exp(_cumsum_dot(m_ref[...], logf))).astype(BF16)
        v = act_ref[0, rows, A_V:A_V + WIDTH].astype(BF16)
        tot = jnp.exp(jnp.sum(logf, axis=0, keepdims=True))
        for pr in range(N_PAIRS):
            sl = slice(pr * PAIR, (pr + 1) * PAIR)
            st_ref[pr] = st_ref[pr] * tot[:, sl] + _pair_outer(v[:, sl], w[:, sl])


def _in_proj(x, mod, norm_w, w_in, lbl, snw, s_b0, tm, sub):
    b, t, d = x.shape
    nb = t // tm
    per = tm // sub
    m = jnp.asarray(_block_decay_matrix(sub, True), BF16)
    full = lambda a: pl.BlockSpec(a.shape, lambda bi, i: (0,) * a.ndim, pipeline_mode=pl.Buffered(1))
    n_gat = G_GATES + 2 * d
    return pl.pallas_call(
        functools.partial(_in_proj_kernel, sub),
        out_shape=(jax.ShapeDtypeStruct((b, t, A_COLS), F32),
                   jax.ShapeDtypeStruct((b, t, n_gat), BF16),
                   jax.ShapeDtypeStruct((b, t // sub, N_PAIRS, PAIR, PAIR), F32)),
        grid=(b, nb),
        in_specs=[pl.BlockSpec((1, tm, d), lambda bi, i: (bi, nb - 1 - i, 0)), full(mod), full(norm_w),
                  full(w_in), full(lbl), full(snw), full(m),
                  pl.BlockSpec((1, N_PAIRS, PAIR, PAIR), lambda bi, i: (bi, 0, 0, 0))],
        out_specs=(pl.BlockSpec((1, tm, A_COLS), lambda bi, i: (bi, nb - 1 - i, 0)),
                   pl.BlockSpec((1, tm, n_gat), lambda bi, i: (bi, nb - 1 - i, 0)),
                   pl.BlockSpec((1, per, N_PAIRS, PAIR, PAIR), lambda bi, i: (bi, nb - 1 - i, 0, 0, 0))),
        scratch_shapes=[pltpu.VMEM((N_PAIRS, PAIR, PAIR), F32), pltpu.VMEM((tm, WIDTH), F32),
                        pltpu.VMEM((tm, d), BF16)],
        compiler_params=pltpu.CompilerParams(dimension_semantics=("arbitrary", "arbitrary"),
                                             vmem_limit_bytes=VMEM_LIMIT),
        name="in_proj",
    )(x, mod, norm_w, w_in, lbl, snw, m, s_b0)


def _level_masks(reverse):
    row = lax.broadcasted_iota(I32, (CHUNK, PAIR), 0)
    col = lax.broadcasted_iota(I32, (CHUNK, PAIR), 1) & (CHUNK - 1)
    x = row ^ col
    out = []
    for half in LEVELS:
        in_pair = jnp.where(x >= half, jnp.where(x < 2 * half, 1.0, 0.0), 0.0)
        bit = (col if reverse else row) & half
        out.append((jnp.where(bit != 0, in_pair, 0.0), (row & half) != 0))
    return out


def _hgrn_scores(q, logf, dm_ref, masks, reverse, pm_ref, dio_ref, k_ref):
    lf2 = logf * LOG2E
    f = jnp.exp2(lf2)
    k = 1.0 - f
    k_ref[...] = k
    hi, lo = _split2(lf2)
    dall = _dot(dm_ref[...], jnp.concatenate([hi, lo], axis=0))
    dio_ref[...] = dall[0:CHUNK]
    n_groups = CHUNK // ROW_GROUP
    for pr in range(N_PAIRS):
        sl = slice(pr * PAIR, (pr + 1) * PAIR)
        qp, kp = q[:, sl], k[:, sl]
        pm = [jnp.zeros((ROW_GROUP, PAIR), F32) for _ in range(n_groups)]
        for li, half in enumerate(LEVELS):
            mask, row_bit = masks[li]
            qside = jnp.logical_not(row_bit) if reverse else row_bit
            if half > 1:
                e = jnp.exp2(dall[(1 + li) * CHUNK:(2 + li) * CHUNK, sl])
                xm = (jnp.where(qside, qp, kp) * e).astype(BF16)
            else:
                xm = jnp.where(qside, qp * f[:, sl], kp).astype(BF16)
            groups = [g for g in range(n_groups)
                      if half < ROW_GROUP or (((g * ROW_GROUP) & half) != 0) != reverse]
            lhs = xm if len(groups) == n_groups else jnp.concatenate(
                [xm[g * ROW_GROUP:(g + 1) * ROW_GROUP] for g in groups], axis=0)
            gm = _dot_nt(lhs, _block_diag(xm))
            for n, g in enumerate(groups):
                pm[g] = pm[g] + (gm[n * ROW_GROUP:(n + 1) * ROW_GROUP]
                                 * mask[g * ROW_GROUP:(g + 1) * ROW_GROUP])
        pm_ref[pr] = jnp.concatenate(pm, axis=0).astype(BF16)


def _hgrn_apply(q, v, reverse, pm_ref, dio_ref, k_ref, state_ref, o_ref, r0):
    k = k_ref[...]
    d_in = dio_ref[...]
    tot_row = 0 if reverse else CHUNK - 1
    d_tot = d_in[tot_row:tot_row + 1]
    e_in = jnp.exp2(d_in)
    e_st = jnp.exp2(d_tot - d_in)
    e_tot = e_in[tot_row:tot_row + 1]
    first = lax.broadcasted_iota(I32, (CHUNK, PAIR), 1) < HD
    for pr in range(N_PAIRS):
        sl = slice(pr * PAIR, (pr + 1) * PAIR)
        qp, kp, vp = q[:, sl], k[:, sl], v[:, sl]
        qk = qp * kp
        diag = jnp.where(first, jnp.sum(qk[:, 0:HD], axis=-1, keepdims=True),
                         jnp.sum(qk[:, HD:PAIR], axis=-1, keepdims=True))
        st = state_ref[pr]
        o = (_dot(pm_ref[pr], _block_diag(vp.astype(BF16))) + diag * vp
             + _dot_nt((qp * e_in[:, sl]).astype(BF16), st.astype(BF16)))
        o_ref[pl.ds(r0, CHUNK), sl] += o
        state_ref[pr] = st * e_tot[:, sl] + _pair_outer(vp.astype(BF16), (kp * e_st[:, sl]).astype(BF16))


def _group_rms(x):
    return jnp.concatenate(
        [x[:, g * HD:(g + 1) * HD]
         * lax.rsqrt(jnp.mean(x[:, g * HD:(g + 1) * HD] ** 2, axis=-1, keepdims=True) + EPS)
         for g in range(x.shape[1] // HD)], axis=-1)


def _mixer_kernel(act_ref, gat_ref, x_ref, mod_ref, nffn_ref, hnw_ref,
                  sw_ref, sbias_ref, wa_ref, wb_ref, wo_ref, rw_ref, dmf_ref, dmb_ref,
                  sf0_ref, bst_ref,
                  x1_ref, h2_ref, aff_ref,
                  o_ref, sg_ref, stf_ref, stb_ref, pm_ref, dio_ref, k_ref):
    b = pl.program_id(0)
    i = pl.program_id(1)
    tb, d = x_ref.shape[1], x_ref.shape[2]
    nch = tb // CHUNK
    _, _, g_m, sh_f, sc_f, _ = _mod_rows(mod_ref, b, d)
    act = lambda rows, c0: act_ref[0, rows, c0:c0 + WIDTH]
    every = slice(None)

    @pl.when(i == 0)
    def _():
        stf_ref[...] = sf0_ref[0]

    stb_ref[...] = bst_ref[0, 0]

    o_ref[...] = jnp.zeros_like(o_ref)
    work = []
    for reverse in (False, True):
        work += [(reverse, ci) for ci in (range(nch - 1, -1, -1) if reverse else range(nch))]
    masks = {reverse: _level_masks(reverse) for reverse in (False, True)}
    for n, (reverse, ci) in enumerate(work):
        rows = pl.ds(ci * CHUNK, CHUNK)
        a_l, dm_ref = (A_LB, dmb_ref) if reverse else (A_LF, dmf_ref)
        _hgrn_scores(act(rows, A_Q), act(rows, a_l), dm_ref, masks[reverse], reverse,
                     pm_ref.at[n], dio_ref.at[n], k_ref.at[n])
    for n, (reverse, ci) in enumerate(work):
        rows = pl.ds(ci * CHUNK, CHUNK)
        st_ref = stb_ref if reverse else stf_ref
        _hgrn_apply(act(rows, A_Q), act(rows, A_V), reverse, pm_ref.at[n], dio_ref.at[n], k_ref.at[n],
                    st_ref, o_ref, ci * CHUNK)

    a_in = _group_rms(o_ref[...]) * hnw_ref[...] * act(every, A_G)
    y_a = _dot(a_in.astype(BF16), wa_ref[...])

    for ci in range(nch):
        rows = slice(ci * CHUNK, (ci + 1) * CHUNK)
        for g in range(HEADS):
            sl = slice(g * HD, (g + 1) * HD)
            sg_ref[rows, sl] = (_dot(sw_ref[g], gat_ref[0, rows, G_SV + g * HD:G_SV + (g + 1) * HD])
                                + sbias_ref[:, sl])
    y_b = _dot((act(every, A_U) * sg_ref[...]).astype(BF16), wb_ref[...])

    merged = (gat_ref[0, :, G_GATES:G_GATES + d].astype(F32) * y_a
              + gat_ref[0, :, G_GATES + d:G_GATES + 2 * d].astype(F32) * y_b)
    y = _dot(merged.astype(BF16), wo_ref[...])
    x1 = x_ref[0] + g_m * y
    x1_ref[0] = x1

    h2 = _rms(x1, nffn_ref[...]) * (1.0 + sc_f) + sh_f
    h2_hi, h2_lo = _split2(h2)
    h2_ref[0] = h2_hi
    rw_hi, rw_lo = _split2(rw_ref[...])
    logits = _dot_nt(rw_hi, h2_hi) + _dot_nt(rw_hi, h2_lo) + _dot_nt(rw_lo, h2_hi)
    mx = jnp.max(logits, axis=0, keepdims=True)
    ex = jnp.exp(logits - mx)
    aff_ref[0] = ex / jnp.sum(ex, axis=0, keepdims=True)


def _mixer(act, gat, x, mod, nffn, hnw, sgu_w, sgu_bias, w_a, w_b, w_o, rw_t, s_f0, bstates, tb):
    b, t, d = x.shape
    nb = t // tb
    per_state = bstates.shape[1] // nb
    ne = rw_t.shape[0]
    dmf = jnp.asarray(np.tile(_decay_matrices(False), (1, 2)), BF16)
    dmb = jnp.asarray(np.tile(_decay_matrices(True), (1, 2)), BF16)
    const = lambda a: pl.BlockSpec(a.shape, lambda bi, i: (0,) * a.ndim, pipeline_mode=pl.Buffered(1))
    in_specs = [pl.BlockSpec((1, tb, act.shape[2]), lambda bi, i: (bi, i, 0)),
                pl.BlockSpec((1, tb, gat.shape[2]), lambda bi, i: (bi, i, 0)),
                pl.BlockSpec((1, tb, d), lambda bi, i: (bi, i, 0)),
                const(mod), const(nffn), const(hnw),
                const(sgu_w), const(sgu_bias), const(w_a), const(w_b), const(w_o), const(rw_t),
                const(dmf), const(dmb),
                pl.BlockSpec((1, N_PAIRS, PAIR, PAIR), lambda bi, i: (bi, 0, 0, 0)),
                pl.BlockSpec((1, 1, N_PAIRS, PAIR, PAIR), lambda bi, i: (bi, (i + 1) * per_state - 1, 0, 0, 0))]
    out_shape = (jax.ShapeDtypeStruct((b, t, d), F32),
                 jax.ShapeDtypeStruct((b, t, d), BF16),
                 jax.ShapeDtypeStruct((b, ne, t), F32))
    out_specs = (pl.BlockSpec((1, tb, d), lambda bi, i: (bi, i, 0)),
                 pl.BlockSpec((1, tb, d), lambda bi, i: (bi, i, 0)),
                 pl.BlockSpec((1, ne, tb), lambda bi, i: (bi, 0, i)))
    return pl.pallas_call(
        _mixer_kernel,
        out_shape=out_shape,
        grid=(b, nb),
        in_specs=in_specs,
        out_specs=out_specs,
        scratch_shapes=[pltpu.VMEM((tb, WIDTH), F32),
                        pltpu.VMEM((tb, WIDTH), F32),
                        pltpu.VMEM((N_PAIRS, PAIR, PAIR), F32),
                        pltpu.VMEM((N_PAIRS, PAIR, PAIR), F32),
                        pltpu.VMEM((2 * (tb // CHUNK), N_PAIRS, CHUNK, PAIR), BF16),
                        pltpu.VMEM((2 * (tb // CHUNK), CHUNK, WIDTH), F32),
                        pltpu.VMEM((2 * (tb // CHUNK), CHUNK, WIDTH), F32)],
        compiler_params=pltpu.CompilerParams(dimension_semantics=("arbitrary", "arbitrary"),
                                             vmem_limit_bytes=VMEM_LIMIT),
        name="mixer",
    )(act, gat, x, mod, nffn, hnw, sgu_w, sgu_bias, w_a, w_b, w_o, rw_t, dmf, dmb, s_f0, bstates)


def _front(x, c, ctx, c_ctx, ada_w, ada_b, norm_mix_w, norm_ffn_w, w_in, hgrn_lb_logits,
           hgrn_norm_w, sgu_norm_w, sgu_w, sgu_b, w_branch_a, w_branch_b, w_out, router_w, tb):
    b, t, d = x.shape
    layer = 0
    cond = jnp.zeros((8, d), F32).at[0:b].set(c).at[b].set(c_ctx)
    mod = _adaln(cond, ada_w[layer], ada_b[layer])
    lbl = hgrn_lb_logits[:, layer:layer + 2, :].reshape(4, WIDTH)
    nmix = norm_mix_w[layer].reshape(1, d)
    nffn = norm_ffn_w[layer].reshape(1, d)
    w_in_b = w_in[layer].astype(BF16)
    s_f0, s_b0 = _ctx_states(ctx, mod, nmix, w_in_b[:, WIDTH:4 * WIDTH], lbl, b)
    act, gat, bstates = _in_proj(x, mod, nmix, w_in_b, lbl, sgu_norm_w[layer].reshape(1, WIDTH), s_b0,
                                 min(t, 512), 256)
    sgu_bias = jnp.repeat(sgu_b[layer].T, HD, axis=1)
    return _mixer(act, gat, x, mod, nffn, hgrn_norm_w[layer].reshape(1, WIDTH),
                  sgu_w[layer].astype(BF16), sgu_bias,
                  w_branch_a[layer].astype(BF16), w_branch_b[layer].astype(BF16),
                  w_out[layer].astype(BF16), router_w[layer].T, s_f0, bstates, tb), mod


def _route_kernel(cap, aff_ref, pos_ref, gate_ref, starts_ref):
    a = aff_ref[0]
    ne, t = a.shape
    nblk = t // LANES
    bits = pltpu.bitcast(a, I32)

    def search(it, lo):
        cand = lo | (jnp.int32(1) << (30 - it))
        cnt = jnp.sum(jnp.where(bits >= cand, 1.0, 0.0), axis=-1, keepdims=True)
        return jnp.where(cnt >= cap, cand, lo)

    thr = lax.fori_loop(0, 31, search, jnp.zeros((ne, 1), I32))
    gt = bits > thr
    eq = bits == thr
    n_ties_wanted = cap - jnp.sum(jnp.where(gt, 1.0, 0.0), axis=-1, keepdims=True)

    row = lax.broadcasted_iota(I32, (LANES, LANES), 0)
    col = lax.broadcasted_iota(I32, (LANES, LANES), 1)
    upper = jnp.where(row <= col, 1.0, 0.0).astype(BF16)
    lane = lax.broadcasted_iota(I32, (ne, LANES), 1)

    off = jnp.zeros((ne, 1), F32)
    sel_blocks = []
    for j in range(nblk):
        sl = slice(j * LANES, (j + 1) * LANES)
        eqf = jnp.where(eq[:, sl], 1.0, 0.0)
        incl = _dot(eqf.astype(BF16), upper) + off
        keep_tie = jnp.where(incl - eqf < n_ties_wanted, eqf, 0.0)
        sel_blocks.append(jnp.where(gt[:, sl], 1.0, keep_tie))
        off = off + jnp.sum(eqf, axis=-1, keepdims=True)

    off = jnp.zeros((ne, 1), F32)
    starts = jnp.zeros((ne, LANES), F32)
    for j in range(nblk):
        sl = slice(j * LANES, (j + 1) * LANES)
        self = sel_blocks[j]
        starts = jnp.where(lane == j, off, starts)
        incl = _dot(self.astype(BF16), upper) + off
        pos_ref[0, :, sl] = jnp.where(self > 0.0, incl - 1.0, -1.0).astype(I32)
        for k, piece in enumerate(_split3(jnp.where(self > 0.0, a[:, sl], 0.0))):
            gate_ref[0, k * ne:(k + 1) * ne, sl] = piece.astype(F32)
        off = off + jnp.sum(self, axis=-1, keepdims=True)
    starts = jnp.where(lane >= nblk, off, starts)
    starts_ref[0] = starts.astype(I32)


def _route(aff_t, cap):
    b, ne, t = aff_t.shape
    spec = pl.BlockSpec((1, ne, t), lambda i: (i, 0, 0))
    return pl.pallas_call(
        functools.partial(_route_kernel, cap),
        out_shape=(jax.ShapeDtypeStruct((b, ne, t), I32),
                   jax.ShapeDtypeStruct((b, 3 * ne, t), F32),
                   jax.ShapeDtypeStruct((b, ne, LANES), I32)),
        grid=(b,),
        in_specs=[spec],
        out_specs=(spec, pl.BlockSpec((1, 3 * ne, t), lambda i: (i, 0, 0)),
                   pl.BlockSpec((1, ne, LANES), lambda i: (i, 0, 0))),
        compiler_params=pltpu.CompilerParams(dimension_semantics=("arbitrary",),
                                             vmem_limit_bytes=VMEM_LIMIT),
        name="route",
    )(aff_t)


TOK_BLOCK = 2 * LANES
GATHER_STEP_BLOCKS = 8
COMBINE_STEP_BLOCKS = 2


def _step_tokens(t, blocks):
    return blocks * TOK_BLOCK if t % (blocks * TOK_BLOCK) == 0 else TOK_BLOCK


SLOT_ALIGN = 16
AUX = LANES


def _slot_windows(starts_ref, b, e0, n_exp, ne, j, win):
    per = TOK_BLOCK // LANES
    s_lo, s_hi, a0 = [], [], []
    n_pass = jnp.int32(0)
    for le in range(n_exp):
        base = (b * ne + e0 + le) * LANES
        lo = starts_ref[base + per * j]
        hi = starts_ref[base + per * (j + 1)]
        a = (lo // SLOT_ALIGN) * SLOT_ALIGN
        n_pass = jnp.maximum(n_pass, jnp.where(hi > lo, (hi - a + win - 1) // win, 0))
        s_lo.append(lo)
        s_hi.append(hi)
        a0.append(a)
    return s_lo, s_hi, a0, n_pass


def _window(s_lo, s_hi, a0, p, win, cap):
    a = a0 + p * win
    a_c = pl.multiple_of(jnp.minimum(a, cap - win), SLOT_ALIGN)
    return a_c, jnp.maximum(s_lo, a), jnp.minimum(s_hi, a + win)


def _gather_kernel(win, starts_ref, pos_ref, gate_ref, h2_ref, xs_ref):
    b = pl.program_id(0)
    g = pl.program_id(1)
    eg, cap = xs_ref.shape[1], xs_ref.shape[2]
    ne = pl.num_programs(1) * eg
    d = h2_ref.shape[2]

    @pl.when(pl.program_id(2) == 0)
    def _():
        xs_ref[...] = jnp.zeros_like(xs_ref)

    row = lax.broadcasted_iota(I32, (win, TOK_BLOCK), 0)
    row_d = lax.broadcasted_iota(I32, (win, d), 0)
    row_aux = lax.broadcasted_iota(I32, (win, AUX), 0)

    for blk in range(h2_ref.shape[1] // TOK_BLOCK):
        j = pl.program_id(2) * (h2_ref.shape[1] // TOK_BLOCK) + blk
        toks = slice(blk * TOK_BLOCK, (blk + 1) * TOK_BLOCK)
        s_lo, s_hi, a0, n_pass = _slot_windows(starts_ref, b, g * eg, eg, ne, j, win)

        def one_pass(p, carry, j=j, toks=toks, s_lo=s_lo, s_hi=s_hi, a0=a0):
            wins = [_window(s_lo[le], s_hi[le], a0[le], p, win, cap) for le in range(eg)]
            onehots = []
            for le, (a_c, lo_i, hi_i) in enumerate(wins):
                prow = pos_ref[0, le, pl.ds(j, 1), :]
                owned = jnp.where(prow >= lo_i, jnp.where(prow < hi_i, prow, -1), -1)
                onehots.append(jnp.where(owned == row + a_c, 1.0, 0.0).astype(BF16))
            lhs = jnp.concatenate(onehots, axis=0)
            res = _dot(lhs, h2_ref[0, toks, :])
            res_aux = _dot(lhs, gate_ref[0, toks, :])
            for le, (a_c, lo_i, hi_i) in enumerate(wins):
                rows = pl.ds(a_c, win)
                mine = slice(le * win, (le + 1) * win)
                xs_ref[0, le, rows, 0:d] = jnp.where(row_d + a_c < lo_i, xs_ref[0, le, rows, 0:d],
                                                     res[mine].astype(BF16))
                xs_ref[0, le, rows, d:d + AUX] = jnp.where(row_aux + a_c < lo_i,
                                                           xs_ref[0, le, rows, d:d + AUX],
                                                           res_aux[mine].astype(BF16))
            return carry

        lax.fori_loop(0, n_pass, one_pass, 0)


def _gather(starts, pos, gate3, h2, cap, win, eg):
    b, t, d = h2.shape
    ne = pos.shape[1]
    ntb = t // TOK_BLOCK
    pos4 = pos.reshape(b, ne, ntb, TOK_BLOCK)
    gate_t = jnp.pad(jnp.swapaxes(gate3, 1, 2).astype(BF16), ((0, 0), (0, 0), (0, AUX - gate3.shape[1])))
    step = _step_tokens(t, GATHER_STEP_BLOCKS)
    grid_spec = pltpu.PrefetchScalarGridSpec(
        num_scalar_prefetch=1,
        grid=(b, ne // eg, t // step),
        in_specs=[pl.BlockSpec((1, eg, ntb, TOK_BLOCK), lambda bi, g, j, s: (bi, g, 0, 0)),
                  pl.BlockSpec((1, step, AUX), lambda bi, g, j, s: (bi, j, 0)),
                  pl.BlockSpec((1, step, d), lambda bi, g, j, s: (bi, j, 0))],
        out_specs=pl.BlockSpec((1, eg, cap, d + AUX), lambda bi, g, j, s: (bi, g, 0, 0)))
    return pl.pallas_call(
        functools.partial(_gather_kernel, win),
        out_shape=jax.ShapeDtypeStruct((b, ne, cap, d + AUX), BF16),
        grid_spec=grid_spec,
        compiler_params=pltpu.CompilerParams(
            dimension_semantics=("arbitrary", "arbitrary", "arbitrary"), vmem_limit_bytes=VMEM_LIMIT),
        name="gather",
    )(starts.reshape(-1), pos4, gate_t, h2)


FF_SUB = 256


def _ffn_kernel(n_tiles, xs_ref, wg_ref, wu_ref, wd_ref, gf_ref, ye_ref, acc_ref):
    bg = pl.program_id(0)
    e = pl.program_id(1)
    f = pl.program_id(2)
    ne = pl.num_programs(1)
    ns, cap, d = ye_ref.shape[0], ye_ref.shape[2], ye_ref.shape[3]
    xs = xs_ref[:, 0, :, 0:d].reshape(ns * cap, d)
    tf = wg_ref.shape[2]

    def run(first, last):
        acc = None if first else acc_ref[...]
        for c0 in range(0, tf, FF_SUB):
            a = _dot(xs, wg_ref[0, :, c0:c0 + FF_SUB].astype(BF16))
            u = _dot(xs, wu_ref[0, :, c0:c0 + FF_SUB].astype(BF16))
            down = _dot((_silu(a) * u).astype(BF16), wd_ref[0, c0:c0 + FF_SUB, :].astype(BF16))
            acc = down if acc is None else acc + down
        if not last:
            acc_ref[...] = acc
            return
        for s in range(ns):
            aux = xs_ref[s, 0, :, d:d + AUX].astype(F32)
            lane = lax.broadcasted_iota(I32, aux.shape, 1)
            mine = jnp.where(lane < 3 * ne, jnp.where(lane % ne == e, aux, 0.0), 0.0)
            gate = jnp.sum(mine, axis=-1, keepdims=True)
            ye_ref[s, 0] = (acc[s * cap:(s + 1) * cap] * gate
                            * gf_ref[pl.ds(bg * ns + s, 1), :]).astype(BF16)

    if n_tiles == 1:
        run(True, True)
    else:
        pl.when(f == 0)(functools.partial(run, True, False))
        if n_tiles > 2:
            pl.when(jnp.logical_and(f > 0, f < n_tiles - 1))(functools.partial(run, False, False))
        pl.when(f == n_tiles - 1)(functools.partial(run, False, True))


def _ffn(xs, w_gate, w_up, w_down, mod, tf, ns):
    b, ne, cap, daux = xs.shape
    d = daux - AUX
    ff = w_gate.shape[2]
    return pl.pallas_call(
        functools.partial(_ffn_kernel, ff // tf),
        out_shape=jax.ShapeDtypeStruct((b, ne, cap, d), BF16),
        grid=(b // ns, ne, ff // tf),
        in_specs=[pl.BlockSpec((ns, 1, cap, daux), lambda bi, e, f: (bi, e, 0, 0)),
                  pl.BlockSpec((1, d, tf), lambda bi, e, f: (e, 0, f)),
                  pl.BlockSpec((1, d, tf), lambda bi, e, f: (e, 0, f)),
                  pl.BlockSpec((1, tf, d), lambda bi, e, f: (e, f, 0)),
                  pl.BlockSpec((mod.shape[0], d), lambda bi, e, f: (0, N_MOD - 1))],
        out_specs=pl.BlockSpec((ns, 1, cap, d), lambda bi, e, f: (bi, e, 0, 0)),
        scratch_shapes=[pltpu.VMEM((ns * cap, d), F32)],
        compiler_params=pltpu.CompilerParams(dimension_semantics=("arbitrary", "arbitrary", "arbitrary"),
                                             vmem_limit_bytes=VMEM_LIMIT),
        name="ffn",
    )(xs, w_gate, w_up, w_down, mod)


def _combine_kernel(win, starts_ref, post_ref, ye_ref, x1_ref, fw_ref, out_ref, acc_ref, stack_ref):
    b = pl.program_id(0)
    ne, cap = ye_ref.shape[1], ye_ref.shape[2]
    row = lax.broadcasted_iota(I32, (win, TOK_BLOCK), 0)
    n_blk = x1_ref.shape[1] // TOK_BLOCK

    for blk in range(n_blk):
        j = pl.program_id(1) * n_blk + blk
        toks = slice(blk * TOK_BLOCK, (blk + 1) * TOK_BLOCK)
        acc_ref[...] = x1_ref[0, toks, :]
        s_lo, s_hi, a0, n_pass = _slot_windows(starts_ref, b, 0, ne, ne, j, win)

        def one_pass(p, carry, j=j, s_lo=s_lo, s_hi=s_hi, a0=a0):
            onehots = []
            for e in range(ne):
                a_c, lo_i, hi_i = _window(s_lo[e], s_hi[e], a0[e], p, win, cap)
                prow = post_ref[0, e, pl.ds(j, 1), :]
                owned = jnp.where(prow >= lo_i, jnp.where(prow < hi_i, prow, -1), -1)
                onehots.append(jnp.where(owned == row + a_c, 1.0, 0.0).astype(BF16))
                stack_ref[e * win:(e + 1) * win, :] = ye_ref[0, e, pl.ds(a_c, win), :]
            acc_ref[...] += _dot_tn(jnp.concatenate(onehots, axis=0), stack_ref[...])
            return carry

        lax.fori_loop(0, n_pass, one_pass, 0)
        out_ref[0, toks, :] = _rms(acc_ref[...], fw_ref[...])


def _combine(starts, pos, ye, x1, final_w, win):
    b, t, d = x1.shape
    ne, cap = ye.shape[1], ye.shape[2]
    ntb = t // TOK_BLOCK
    pos4 = pos.reshape(b, ne, ntb, TOK_BLOCK)
    step = _step_tokens(t, COMBINE_STEP_BLOCKS)
    grid_spec = pltpu.PrefetchScalarGridSpec(
        num_scalar_prefetch=1,
        grid=(b, t // step),
        in_specs=[pl.BlockSpec((1, ne, ntb, TOK_BLOCK), lambda bi, j, s: (bi, 0, 0, 0)),
                  pl.BlockSpec((1, ne, cap, d), lambda bi, j, s: (bi, 0, 0, 0),
                               pipeline_mode=pl.Buffered(1)),
                  pl.BlockSpec((1, step, d), lambda bi, j, s: (bi, j, 0)),
                  pl.BlockSpec((1, d), lambda bi, j, s: (0, 0))],
        out_specs=pl.BlockSpec((1, step, d), lambda bi, j, s: (bi, j, 0)),
        scratch_shapes=[pltpu.VMEM((TOK_BLOCK, d), F32), pltpu.VMEM((ne * win, d), BF16)])
    return pl.pallas_call(
        functools.partial(_combine_kernel, win),
        out_shape=jax.ShapeDtypeStruct((b, t, d), F32),
        grid_spec=grid_spec,
        compiler_params=pltpu.CompilerParams(
            dimension_semantics=("arbitrary", "arbitrary"), vmem_limit_bytes=VMEM_LIMIT),
        name="combine",
    )(starts.reshape(-1), pos4, ye, x1, final_w.reshape(1, d))


def kernel(x, c, ctx, c_ctx, ada_w, ada_b, norm_mix_w, norm_ffn_w, w_in, hgrn_lb_logits, hgrn_norm_w,
           sgu_norm_w, sgu_w, sgu_b, w_branch_a, w_branch_b, w_out, router_w, expert_w_gate,
           expert_w_up, expert_w_down, final_norm_w):
    b, t, d = x.shape
    assert b + 1 <= 8 and t % TOK_BLOCK == 0 and d % LANES == 0
    ne = router_w.shape[-1]
    cap = CAPACITY_FACTOR * t // ne
    win = min(cap, 64)
    assert cap % SLOT_ALIGN == 0 and t // LANES < LANES and 3 * ne <= AUX
    (x1, h2, aff_t), mod = _front(x, c, ctx, c_ctx, ada_w, ada_b, norm_mix_w, norm_ffn_w, w_in,
                                  hgrn_lb_logits, hgrn_norm_w, sgu_norm_w, sgu_w, sgu_b,
                                  w_branch_a, w_branch_b, w_out, router_w, 512)
    pos, gate, starts = _route(aff_t, cap)
    xs = _gather(starts, pos, gate, h2, cap, win, 8)
    ye = _ffn(xs, expert_w_gate[0], expert_w_up[0], expert_w_down[0], mod, 1024, 1)
    return _combine(starts, pos, ye, x1, final_norm_w, win)
```

```python
import functools

import numpy as np
import jax
import jax.numpy as jnp
from jax import lax
from jax.experimental import pallas as pl
from jax.experimental.pallas import tpu as pltpu

F32 = jnp.float32
BF16 = jnp.bfloat16
I32 = jnp.int32

LANES = 128
EPS = 1e-6
N_MOD = 6
HEADS = 4
HD = 128
WIDTH = HEADS * HD
CHUNK = 128
N_EXPERTS = 16
CAPACITY_FACTOR = 2
LEVELS = tuple(CHUNK >> (i + 1) for i in range(7))
N_DMAT = 1 + len(LEVELS) - 1
PAIR = 2 * HD
N_PAIRS = HEADS // 2
ROW_GROUP = 16
LOG2E = 1.4426950408889634
VMEM_LIMIT = 52 * 1024 * 1024


def _dot(a, b):
    return jnp.dot(a, b, preferred_element_type=F32)


def _dot_nt(a, b):
    return lax.dot_general(a, b, (((1,), (1,)), ((), ())), preferred_element_type=F32)


def _dot_tn(a, b):
    return lax.dot_general(a, b, (((0,), (0,)), ((), ())), preferred_element_type=F32)


def _split2(x):
    hi = x.astype(BF16)
    lo = (x - hi.astype(F32)).astype(BF16)
    return hi, lo


def _split3(x):
    hi = x.astype(BF16)
    r = x - hi.astype(F32)
    mid = r.astype(BF16)
    lo = (r - mid.astype(F32)).astype(BF16)
    return hi, mid, lo


def _sigmoid(x):
    return 1.0 / (1.0 + jnp.exp(-x))


def _silu(x):
    return x * _sigmoid(x)


def _gelu_tanh(x):
    c = np.sqrt(2.0 / np.pi).astype(np.float32)
    return 0.5 * x * (1.0 + jnp.tanh(c * (x + 0.044715 * (x * x * x))))


def _rms(x, w):
    return x * lax.rsqrt(jnp.mean(x * x, axis=-1, keepdims=True) + EPS) * w


def _lower_bound(lbl, d):
    l0 = lbl[2 * d:2 * d + 1, :]
    l1 = lbl[2 * d + 1:2 * d + 2, :]
    m = jnp.maximum(l0, l1)
    e0 = jnp.exp(l0 - m)
    e1 = jnp.exp(l1 - m)
    return e0 / (e0 + e1)


def _forget(z, lb):
    f = lb + (1.0 - lb) * _sigmoid(z)
    return jnp.log(f), 1.0 - f


def _cumsum_dot(m_bf16, x):
    hi, mid, lo = _split3(x)
    return _dot(m_bf16, hi) + _dot(m_bf16, mid) + _dot(m_bf16, lo)


def _same_head(shape):
    r = lax.broadcasted_iota(I32, shape, 0) < HD
    c = lax.broadcasted_iota(I32, shape, 1) < HD
    return r == c


def _block_diag(x):
    first = lax.broadcasted_iota(I32, x.shape, 1) < HD
    zero = jnp.zeros_like(x)
    return jnp.concatenate([jnp.where(first, x, zero), jnp.where(first, zero, x)], axis=0)


def _pair_outer(v_pair, w_pair):
    full = _dot_tn(v_pair, w_pair)
    return jnp.where(_same_head(full.shape), full, 0.0)


def _decay_matrices(reverse):
    c = CHUNK
    i = np.arange(c)[:, None]
    m = np.arange(c)[None, :]
    mats = []
    mats.append(m >= i if reverse else m <= i)
    for half in LEVELS[:-1]:
        a = (i // (2 * half)) * (2 * half)
        mid = a + half
        if not reverse:
            qside = i >= mid
            mat = np.where(qside, (m >= mid) & (m <= i), (m > i) & (m < mid))
        else:
            qside = i < mid
            mat = np.where(qside, (m >= i) & (m < mid), (m >= mid) & (m < i))
        mats.append(mat)
    return np.concatenate(mats, axis=0).astype(np.float32)


def _block_decay_matrix(n, reverse):
    i = np.arange(n)[:, None]
    m = np.arange(n)[None, :]
    return ((m < i) if reverse else (m > i)).astype(np.float32)


def _adaln_kernel(cond_ref, w_ref, b_ref, out_ref):
    s = _silu(cond_ref[...])
    s_hi, s_lo = _split2(s)
    w_hi, w_lo = _split2(w_ref[...])
    out_ref[...] = _dot(s_hi, w_hi) + _dot(s_hi, w_lo) + _dot(s_lo, w_hi) + b_ref[...]


def _adaln(cond, ada_w, ada_b):
    rows, d = cond.shape
    n = ada_w.shape[1]
    tn = 1024
    return pl.pallas_call(
        _adaln_kernel,
        out_shape=jax.ShapeDtypeStruct((rows, n), F32),
        grid=(n // tn,),
        in_specs=[pl.BlockSpec((rows, d), lambda j: (0, 0)),
                  pl.BlockSpec((d, tn), lambda j: (0, j)),
                  pl.BlockSpec((1, tn), lambda j: (0, j))],
        out_specs=pl.BlockSpec((rows, tn), lambda j: (0, j)),
        compiler_params=pltpu.CompilerParams(dimension_semantics=("arbitrary",),
                                             vmem_limit_bytes=VMEM_LIMIT),
        name="adaln",
    )(cond, ada_w, ada_b.reshape(1, n))


def _mod_rows(mod_ref, row, d):
    return [mod_ref[pl.ds(row, 1), j * d:(j + 1) * d] for j in range(N_MOD)]


def _ctx_kernel(ctx_row, x_ref, mod_ref, nw_ref, w_ref, lbl_ref, mf_ref, mb_ref, sf_ref, sb_ref):
    d = x_ref.shape[-1]
    x = x_ref[0]
    sh, sc = mod_ref[pl.ds(ctx_row, 1), 0:d], mod_ref[pl.ds(ctx_row, 1), d:2 * d]
    h = _rms(x, nw_ref[...]) * (1.0 + sc) + sh
    p = _dot(h.astype(BF16), w_ref[...])
    v = p[:, 2 * WIDTH:3 * WIDTH].astype(BF16)
    lbl = lbl_ref[...]
    for dirn, (m_ref, out_ref) in enumerate(((mf_ref, sf_ref), (mb_ref, sb_ref))):
        logf, k = _forget(p[:, dirn * WIDTH:(dirn + 1) * WIDTH], _lower_bound(lbl, dirn))
        w = (k * jnp.exp(_cumsum_dot(m_ref[...], logf))).astype(BF16)
        for pr in range(N_PAIRS):
            sl = slice(pr * PAIR, (pr + 1) * PAIR)
            out_ref[0, pr] = _pair_outer(v[:, sl], w[:, sl])


def _ctx_states(ctx, mod, norm_w, w_c, lbl, ctx_row):
    b, l, d = ctx.shape
    mf = jnp.asarray(_block_decay_matrix(l, False), BF16)
    mb = jnp.asarray(_block_decay_matrix(l, True), BF16)
    full = lambda a: pl.BlockSpec(a.shape, lambda i: (0,) * a.ndim)
    st = jax.ShapeDtypeStruct((b, N_PAIRS, PAIR, PAIR), F32)
    st_spec = pl.BlockSpec((1, N_PAIRS, PAIR, PAIR), lambda i: (i, 0, 0, 0))
    return pl.pallas_call(
        functools.partial(_ctx_kernel, ctx_row),
        out_shape=(st, st),
        grid=(b,),
        in_specs=[pl.BlockSpec((1, l, d), lambda i: (i, 0, 0)), full(mod), full(norm_w),
                  full(w_c), full(lbl), full(mf), full(mb)],
        out_specs=(st_spec, st_spec),
        compiler_params=pltpu.CompilerParams(dimension_semantics=("arbitrary",),
                                             vmem_limit_bytes=VMEM_LIMIT),
        name="ctx_state",
    )(ctx, mod, norm_w, w_c, lbl, mf, mb)


A_Q, A_LF, A_LB, A_V, A_G, A_U = (j * WIDTH for j in range(6))
A_COLS = 6 * WIDTH
G_SV, G_GATES = 0, WIDTH
PROJ_PIECE = 256


def _in_proj_kernel(sub, x_ref, mod_ref, nw_ref, w_ref, lbl_ref, snw_ref, m_ref, s0_ref,
                    act_ref, gat_ref, bst_ref, st_ref, kb_ref, h_ref):
    b = pl.program_id(0)
    d = x_ref.shape[-1]

    @pl.when(pl.program_id(1) == 0)
    def _():
        st_ref[...] = s0_ref[0]

    sh, sc = mod_ref[pl.ds(b, 1), 0:d], mod_ref[pl.ds(b, 1), d:2 * d]
    h_ref[...] = (_rms(x_ref[0], nw_ref[...]) * (1.0 + sc) + sh).astype(BF16)
    lbl = lbl_ref[...]

    def pieces(j):
        for c in range(0, WIDTH, PROJ_PIECE):
            yield c, _dot(h_ref[...], w_ref[:, j * WIDTH + c:j * WIDTH + c + PROJ_PIECE])

    lb_f, lb_b = _lower_bound(lbl, 0), _lower_bound(lbl, 1)
    for c, p in pieces(0):
        act_ref[0, :, A_Q + c:A_Q + c + PROJ_PIECE] = _silu(p)
    for c, p in pieces(1):
        act_ref[0, :, A_LF + c:A_LF + c + PROJ_PIECE] = _forget(p, lb_f[:, c:c + PROJ_PIECE])[0]
    for c, p in pieces(2):
        logf_b, k_b = _forget(p, lb_b[:, c:c + PROJ_PIECE])
        act_ref[0, :, A_LB + c:A_LB + c + PROJ_PIECE] = logf_b
        kb_ref[:, c:c + PROJ_PIECE] = k_b
    for c, p in pieces(3):
        act_ref[0, :, A_V + c:A_V + c + PROJ_PIECE] = p
    for c, p in pieces(4):
        act_ref[0, :, A_G + c:A_G + c + PROJ_PIECE] = _silu(p)
    for c, p in pieces(5):
        act_ref[0, :, A_U + c:A_U + c + PROJ_PIECE] = _gelu_tanh(p)
    for c, p in pieces(6):
        gat_ref[0, :, G_SV + c:G_SV + c + PROJ_PIECE] = (
            _group_rms(_gelu_tanh(p)) * snw_ref[:, c:c + PROJ_PIECE]).astype(BF16)
    for j in range(2 * d // WIDTH):
        for c, p in pieces(7 + j):
            gat_ref[0, :, G_GATES + j * WIDTH + c:G_GATES + j * WIDTH + c + PROJ_PIECE] = _sigmoid(p).astype(BF16)

    for piece in range(x_ref.shape[1] // sub - 1, -1, -1):
        rows = slice(piece * sub, (piece + 1) * sub)
        bst_ref[0, piece] = st_ref[...]
        logf = act_ref[0, rows, A_LB:A_LB + WIDTH]
        w = (kb_ref[rows, :] * jnp.exp(_cumsum_dot(m_ref[...], logf))).astype(BF16)
        v = act_ref[0, rows, A_V:A_V + WIDTH].astype(BF16)
        tot = jnp.exp(jnp.sum(logf, axis=0, keepdims=True))
        for pr in range(N_PAIRS):
            sl = slice(pr * PAIR, (pr + 1) * PAIR)
            st_ref[pr] = st_ref[pr] * tot[:, sl] + _pair_outer(v[:, sl], w[:, sl])


def _in_proj(x, mod, norm_w, w_in, lbl, snw, s_b0, tm, sub):
    b, t, d = x.shape
    nb = t // tm
    per = tm // sub
    m = jnp.asarray(_block_decay_matrix(sub, True), BF16)
    full = lambda a: pl.BlockSpec(a.shape, lambda bi, i: (0,) * a.ndim, pipeline_mode=pl.Buffered(1))
    n_gat = G_GATES + 2 * d
    return pl.pallas_call(
        functools.partial(_in_proj_kernel, sub),
        out_shape=(jax.ShapeDtypeStruct((b, t, A_COLS), F32),
                   jax.ShapeDtypeStruct((b, t, n_gat), BF16),
                   jax.ShapeDtypeStruct((b, t // sub, N_PAIRS, PAIR, PAIR), F32)),
        grid=(b, nb),
        in_specs=[pl.BlockSpec((1, tm, d), lambda bi, i: (bi, nb - 1 - i, 0)), full(mod), full(norm_w),
                  full(w_in), full(lbl), full(snw), full(m),
                  pl.BlockSpec((1, N_PAIRS, PAIR, PAIR), lambda bi, i: (bi, 0, 0, 0))],
        out_specs=(pl.BlockSpec((1, tm, A_COLS), lambda bi, i: (bi, nb - 1 - i, 0)),
                   pl.BlockSpec((1, tm, n_gat), lambda bi, i: (bi, nb - 1 - i, 0)),
                   pl.BlockSpec((1, per, N_PAIRS, PAIR, PAIR), lambda bi, i: (bi, nb - 1 - i, 0, 0, 0))),
        scratch_shapes=[pltpu.VMEM((N_PAIRS, PAIR, PAIR), F32), pltpu.VMEM((tm, WIDTH), F32),
                        pltpu.VMEM((tm, d), BF16)],
        compiler_params=pltpu.CompilerParams(dimension_semantics=("arbitrary", "arbitrary"),
                                             vmem_limit_bytes=VMEM_LIMIT),
        name="in_proj",
    )(x, mod, norm_w, w_in, lbl, snw, m, s_b0)


def _level_masks(reverse):
    row = lax.broadcasted_iota(I32, (CHUNK, PAIR), 0)
    col = lax.broadcasted_iota(I32, (CHUNK, PAIR), 1) & (CHUNK - 1)
    x = row ^ col
    out = []
    for half in LEVELS:
        in_pair = jnp.where(x >= half, jnp.where(x < 2 * half, 1.0, 0.0), 0.0)
        bit = (col if reverse else row) & half
        out.append((jnp.where(bit != 0, in_pair, 0.0), (row & half) != 0))
    return out


def _hgrn_scores(q, logf, dm_ref, masks, reverse, pm_ref, dio_ref, k_ref):
    lf2 = logf * LOG2E
    f = jnp.exp2(lf2)
    k = 1.0 - f
    k_ref[...] = k
    hi, lo = _split2(lf2)
    dall = _dot(dm_ref[...], jnp.concatenate([hi, lo], axis=0))
    dio_ref[...] = dall[0:CHUNK]
    n_groups = CHUNK // ROW_GROUP
    for pr in range(N_PAIRS):
        sl = slice(pr * PAIR, (pr + 1) * PAIR)
        qp, kp = q[:, sl], k[:, sl]
        pm = [jnp.zeros((ROW_GROUP, PAIR), F32) for _ in range(n_groups)]
        for li, half in enumerate(LEVELS):
            mask, row_bit = masks[li]
            qside = jnp.logical_not(row_bit) if reverse else row_bit
            if half > 1:
                e = jnp.exp2(dall[(1 + li) * CHUNK:(2 + li) * CHUNK, sl])
                xm = (jnp.where(qside, qp, kp) * e).astype(BF16)
            else:
                xm = jnp.where(qside, qp * f[:, sl], kp).astype(BF16)
            groups = [g for g in range(n_groups)
                      if half < ROW_GROUP or (((g * ROW_GROUP) & half) != 0) != reverse]
            lhs = xm if len(groups) == n_groups else jnp.concatenate(
                [xm[g * ROW_GROUP:(g + 1) * ROW_GROUP] for g in groups], axis=0)
            gm = _dot_nt(lhs, _block_diag(xm))
            for n, g in enumerate(groups):
                pm[g] = pm[g] + (gm[n * ROW_GROUP:(n + 1) * ROW_GROUP]
                                 * mask[g * ROW_GROUP:(g + 1) * ROW_GROUP])
        pm_ref[pr] = jnp.concatenate(pm, axis=0).astype(BF16)


def _hgrn_apply(q, v, reverse, pm_ref, dio_ref, k_ref, state_ref, o_ref, r0):
    k = k_ref[...]
    d_in = dio_ref[...]
    tot_row = 0 if reverse else CHUNK - 1
    d_tot = d_in[tot_row:tot_row + 1]
    e_in = jnp.exp2(d_in)
    e_st = jnp.exp2(d_tot - d_in)
    e_tot = e_in[tot_row:tot_row + 1]
    first = lax.broadcasted_iota(I32, (CHUNK, PAIR), 1) < HD
    for pr in range(N_PAIRS):
        sl = slice(pr * PAIR, (pr + 1) * PAIR)
        qp, kp, vp = q[:, sl], k[:, sl], v[:, sl]
        qk = qp * kp
        diag = jnp.where(first, jnp.sum(qk[:, 0:HD], axis=-1, keepdims=True),
                         jnp.sum(qk[:, HD:PAIR], axis=-1, keepdims=True))
        st = state_ref[pr]
        o = (_dot(pm_ref[pr], _block_diag(vp.astype(BF16))) + diag * vp
             + _dot_nt((qp * e_in[:, sl]).astype(BF16), st.astype(BF16)))
        o_ref[pl.ds(r0, CHUNK), sl] += o
        state_ref[pr] = st * e_tot[:, sl] + _pair_outer(vp.astype(BF16), (kp * e_st[:, sl]).astype(BF16))


def _group_rms(x):
    return jnp.concatenate(
        [x[:, g * HD:(g + 1) * HD]
         * lax.rsqrt(jnp.mean(x[:, g * HD:(g + 1) * HD] ** 2, axis=-1, keepdims=True) + EPS)
         for g in range(x.shape[1] // HD)], axis=-1)


def _mixer_kernel(act_ref, gat_ref, x_ref, mod_ref, nffn_ref, hnw_ref,
                  sw_ref, sbias_ref, wa_ref, wb_ref, wo_ref, rw_ref, dmf_ref, dmb_ref,
                  sf0_ref, bst_ref,
                  x1_ref, h2_ref, aff_ref,
                  o_ref, sg_ref, stf_ref, stb_ref, pm_ref, dio_ref, k_ref):
    b = pl.program_id(0)
    i = pl.program_id(1)
    tb, d = x_ref.shape[1], x_ref.shape[2]
    nch = tb // CHUNK
    _, _, g_m, sh_f, sc_f, _ = _mod_rows(mod_ref, b, d)
    act = lambda rows, c0: act_ref[0, rows, c0:c0 + WIDTH]
    every = slice(None)

    @pl.when(i == 0)
    def _():
        stf_ref[...] = sf0_ref[0]

    stb_ref[...] = bst_ref[0, 0]

    o_ref[...] = jnp.zeros_like(o_ref)
    work = []
    for reverse in (False, True):
        work += [(reverse, ci) for ci in (range(nch - 1, -1, -1) if reverse else range(nch))]
    masks = {reverse: _level_masks(reverse) for reverse in (False, True)}
    for n, (reverse, ci) in enumerate(work):
        rows = pl.ds(ci * CHUNK, CHUNK)
        a_l, dm_ref = (A_LB, dmb_ref) if reverse else (A_LF, dmf_ref)
        _hgrn_scores(act(rows, A_Q), act(rows, a_l), dm_ref, masks[reverse], reverse,
                     pm_ref.at[n], dio_ref.at[n], k_ref.at[n])
    for n, (reverse, ci) in enumerate(work):
        rows = pl.ds(ci * CHUNK, CHUNK)
        st_ref = stb_ref if reverse else stf_ref
        _hgrn_apply(act(rows, A_Q), act(rows, A_V), reverse, pm_ref.at[n], dio_ref.at[n], k_ref.at[n],
                    st_ref, o_ref, ci * CHUNK)

    a_in = _group_rms(o_ref[...]) * hnw_ref[...] * act(every, A_G)
    y_a = _dot(a_in.astype(BF16), wa_ref[...])

    for ci in range(nch):
        rows = slice(ci * CHUNK, (ci + 1) * CHUNK)
        for g in range(HEADS):
            sl = slice(g * HD, (g + 1) * HD)
            sg_ref[rows, sl] = (_dot(sw_ref[g], gat_ref[0, rows, G_SV + g * HD:G_SV + (g + 1) * HD])
                                + sbias_ref[:, sl])
    y_b = _dot((act(every, A_U) * sg_ref[...]).astype(BF16), wb_ref[...])

    merged = (gat_ref[0, :, G_GATES:G_GATES + d].astype(F32) * y_a
              + gat_ref[0, :, G_GATES + d:G_GATES + 2 * d].astype(F32) * y_b)
    y = _dot(merged.astype(BF16), wo_ref[...])
    x1 = x_ref[0] + g_m * y
    x1_ref[0] = x1

    h2 = _rms(x1, nffn_ref[...]) * (1.0 + sc_f) + sh_f
    h2_hi, h2_lo = _split2(h2)
    h2_ref[0] = h2_hi
    rw_hi, rw_lo = _split2(rw_ref[...])
    logits = _dot_nt(rw_hi, h2_hi) + _dot_nt(rw_hi, h2_lo) + _dot_nt(rw_lo, h2_hi)
    mx = jnp.max(logits, axis=0, keepdims=True)
    ex = jnp.exp(logits - mx)
    aff_ref[0] = ex / jnp.sum(ex, axis=0, keepdims=True)


def _mixer(act, gat, x, mod, nffn, hnw, sgu_w, sgu_bias, w_a, w_b, w_o, rw_t, s_f0, bstates, tb):
    b, t, d = x.shape
    nb = t // tb
    per_state = bstates.shape[1] // nb
    ne = rw_t.shape[0]
    dmf = jnp.asarray(np.tile(_decay_matrices(False), (1, 2)), BF16)
    dmb = jnp.asarray(np.tile(_decay_matrices(True), (1, 2)), BF16)
    const = lambda a: pl.BlockSpec(a.shape, lambda bi, i: (0,) * a.ndim, pipeline_mode=pl.Buffered(1))
    in_specs = [pl.BlockSpec((1, tb, act.shape[2]), lambda bi, i: (bi, i, 0)),
                pl.BlockSpec((1, tb, gat.shape[2]), lambda bi, i: (bi, i, 0)),
                pl.BlockSpec((1, tb, d), lambda bi, i: (bi, i, 0)),
                const(mod), const(nffn), const(hnw),
                const(sgu_w), const(sgu_bias), const(w_a), const(w_b), const(w_o), const(rw_t),
                const(dmf), const(dmb),
                pl.BlockSpec((1, N_PAIRS, PAIR, PAIR), lambda bi, i: (bi, 0, 0, 0)),
                pl.BlockSpec((1, 1, N_PAIRS, PAIR, PAIR), lambda bi, i: (bi, (i + 1) * per_state - 1, 0, 0, 0))]
    out_shape = (jax.ShapeDtypeStruct((b, t, d), F32),
                 jax.ShapeDtypeStruct((b, t, d), BF16),
                 jax.ShapeDtypeStruct((b, ne, t), F32))
    out_specs = (pl.BlockSpec((1, tb, d), lambda bi, i: (bi, i, 0)),
                 pl.BlockSpec((1, tb, d), lambda bi, i: (bi, i, 0)),
                 pl.BlockSpec((1, ne, tb), lambda bi, i: (bi, 0, i)))
    return pl.pallas_call(
        _mixer_kernel,
        out_shape=out_shape,
        grid=(b, nb),
        in_specs=in_specs,
        out_specs=out_specs,
        scratch_shapes=[pltpu.VMEM((tb, WIDTH), F32),
                        pltpu.VMEM((tb, WIDTH), F32),
                        pltpu.VMEM((N_PAIRS, PAIR, PAIR), F32),
                        pltpu.VMEM((N_PAIRS, PAIR, PAIR), F32),
                        pltpu.VMEM((2 * (tb // CHUNK), N_PAIRS, CHUNK, PAIR), BF16),
                        pltpu.VMEM((2 * (tb // CHUNK), CHUNK, WIDTH), F32),
                        pltpu.VMEM((2 * (tb // CHUNK), CHUNK, WIDTH), F32)],
        compiler_params=pltpu.CompilerParams(dimension_semantics=("arbitrary", "arbitrary"),
                                             vmem_limit_bytes=VMEM_LIMIT),
        name="mixer",
    )(act, gat, x, mod, nffn, hnw, sgu_w, sgu_bias, w_a, w_b, w_o, rw_t, dmf, dmb, s_f0, bstates)


def _front(x, c, ctx, c_ctx, ada_w, ada_b, norm_mix_w, norm_ffn_w, w_in, hgrn_lb_logits,
           hgrn_norm_w, sgu_norm_w, sgu_w, sgu_b, w_branch_a, w_branch_b, w_out, router_w, tb):
    b, t, d = x.shape
    layer = 0
    cond = jnp.zeros((8, d), F32).at[0:b].set(c).at[b].set(c_ctx)
    mod = _adaln(cond, ada_w[layer], ada_b[layer])
    lbl = hgrn_lb_logits[:, layer:layer + 2, :].reshape(4, WIDTH)
    nmix = norm_mix_w[layer].reshape(1, d)
    nffn = norm_ffn_w[layer].reshape(1, d)
    w_in_b = w_in[layer].astype(BF16)
    s_f0, s_b0 = _ctx_states(ctx, mod, nmix, w_in_b[:, WIDTH:4 * WIDTH], lbl, b)
    act, gat, bstates = _in_proj(x, mod, nmix, w_in_b, lbl, sgu_norm_w[layer].reshape(1, WIDTH), s_b0,
                                 min(t, 512), 256)
    sgu_bias = jnp.repeat(sgu_b[layer].T, HD, axis=1)
    return _mixer(act, gat, x, mod, nffn, hgrn_norm_w[layer].reshape(1, WIDTH),
                  sgu_w[layer].astype(BF16), sgu_bias,
                  w_branch_a[layer].astype(BF16), w_branch_b[layer].astype(BF16),
                  w_out[layer].astype(BF16), router_w[layer].T, s_f0, bstates, tb), mod


def _route_kernel(cap, aff_ref, pos_ref, gate_ref, starts_ref):
    a = aff_ref[0]
    ne, t = a.shape
    nblk = t // LANES
    bits = pltpu.bitcast(a, I32)

    def search(it, lo):
        cand = lo | (jnp.int32(1) << (30 - it))
        cnt = jnp.sum(jnp.where(bits >= cand, 1.0, 0.0), axis=-1, keepdims=True)
        return jnp.where(cnt >= cap, cand, lo)

    thr = lax.fori_loop(0, 31, search, jnp.zeros((ne, 1), I32))
    gt = bits > thr
    eq = bits == thr
    n_ties_wanted = cap - jnp.sum(jnp.where(gt, 1.0, 0.0), axis=-1, keepdims=True)

    row = lax.broadcasted_iota(I32, (LANES, LANES), 0)
    col = lax.broadcasted_iota(I32, (LANES, LANES), 1)
    upper = jnp.where(row <= col, 1.0, 0.0).astype(BF16)
    lane = lax.broadcasted_iota(I32, (ne, LANES), 1)

    off = jnp.zeros((ne, 1), F32)
    sel_blocks = []
    for j in range(nblk):
        sl = slice(j * LANES, (j + 1) * LANES)
        eqf = jnp.where(eq[:, sl], 1.0, 0.0)
        incl = _dot(eqf.astype(BF16), upper) + off
        keep_tie = jnp.where(incl - eqf < n_ties_wanted, eqf, 0.0)
        sel_blocks.append(jnp.where(gt[:, sl], 1.0, keep_tie))
        off = off + jnp.sum(eqf, axis=-1, keepdims=True)

    off = jnp.zeros((ne, 1), F32)
    starts = jnp.zeros((ne, LANES), F32)
    for j in range(nblk):
        sl = slice(j * LANES, (j + 1) * LANES)
        self = sel_blocks[j]
        starts = jnp.where(lane == j, off, starts)
        incl = _dot(self.astype(BF16), upper) + off
        pos_ref[0, :, sl] = jnp.where(self > 0.0, incl - 1.0, -1.0).astype(I32)
        for k, piece in enumerate(_split3(jnp.where(self > 0.0, a[:, sl], 0.0))):
            gate_ref[0, k * ne:(k + 1) * ne, sl] = piece.astype(F32)
        off = off + jnp.sum(self, axis=-1, keepdims=True)
    starts = jnp.where(lane >= nblk, off, starts)
    starts_ref[0] = starts.astype(I32)


def _route(aff_t, cap):
    b, ne, t = aff_t.shape
    spec = pl.BlockSpec((1, ne, t), lambda i: (i, 0, 0))
    return pl.pallas_call(
        functools.partial(_route_kernel, cap),
        out_shape=(jax.ShapeDtypeStruct((b, ne, t), I32),
                   jax.ShapeDtypeStruct((b, 3 * ne, t), F32),
                   jax.ShapeDtypeStruct((b, ne, LANES), I32)),
        grid=(b,),
        in_specs=[spec],
        out_specs=(spec, pl.BlockSpec((1, 3 * ne, t), lambda i: (i, 0, 0)),
                   pl.BlockSpec((1, ne, LANES), lambda i: (i, 0, 0))),
        compiler_params=pltpu.CompilerParams(dimension_semantics=("arbitrary",),
                                             vmem_limit_bytes=VMEM_LIMIT),
        name="route",
    )(aff_t)


TOK_BLOCK = 2 * LANES
GATHER_STEP_BLOCKS = 8
COMBINE_STEP_BLOCKS = 4
COMBINE_VMEM_LIMIT = 56 * 1024 * 1024


def _step_tokens(t, blocks):
    return blocks * TOK_BLOCK if t % (blocks * TOK_BLOCK) == 0 else TOK_BLOCK


SLOT_ALIGN = 16
AUX = LANES


def _slot_windows(starts_ref, b, e0, n_exp, ne, j, win):
    per = TOK_BLOCK // LANES
    s_lo, s_hi, a0 = [], [], []
    n_pass = jnp.int32(0)
    for le in range(n_exp):
        base = (b * ne + e0 + le) * LANES
        lo = starts_ref[base + per * j]
        hi = starts_ref[base + per * (j + 1)]
        a = (lo // SLOT_ALIGN) * SLOT_ALIGN
        n_pass = jnp.maximum(n_pass, jnp.where(hi > lo, (hi - a + win - 1) // win, 0))
        s_lo.append(lo)
        s_hi.append(hi)
        a0.append(a)
    return s_lo, s_hi, a0, n_pass


def _window(s_lo, s_hi, a0, p, win, cap):
    a = a0 + p * win
    a_c = pl.multiple_of(jnp.minimum(a, cap - win), SLOT_ALIGN)
    return a_c, jnp.maximum(s_lo, a), jnp.minimum(s_hi, a + win)


def _gather_kernel(win, starts_ref, pos_ref, gate_ref, h2_ref, xs_ref):
    b = pl.program_id(0)
    g = pl.program_id(1)
    eg, cap = xs_ref.shape[1], xs_ref.shape[2]
    ne = pl.num_programs(1) * eg
    d = h2_ref.shape[2]

    @pl.when(pl.program_id(2) == 0)
    def _():
        xs_ref[...] = jnp.zeros_like(xs_ref)

    row = lax.broadcasted_iota(I32, (win, TOK_BLOCK), 0)
    row_d = lax.broadcasted_iota(I32, (win, d), 0)
    row_aux = lax.broadcasted_iota(I32, (win, AUX), 0)

    for blk in range(h2_ref.shape[1] // TOK_BLOCK):
        j = pl.program_id(2) * (h2_ref.shape[1] // TOK_BLOCK) + blk
        toks = slice(blk * TOK_BLOCK, (blk + 1) * TOK_BLOCK)
        s_lo, s_hi, a0, n_pass = _slot_windows(starts_ref, b, g * eg, eg, ne, j, win)

        def one_pass(p, carry, j=j, toks=toks, s_lo=s_lo, s_hi=s_hi, a0=a0):
            wins = [_window(s_lo[le], s_hi[le], a0[le], p, win, cap) for le in range(eg)]
            onehots = []
            for le, (a_c, lo_i, hi_i) in enumerate(wins):
                prow = pos_ref[0, le, pl.ds(j, 1), :]
                owned = jnp.where(prow >= lo_i, jnp.where(prow < hi_i, prow, -1), -1)
                onehots.append(jnp.where(owned == row + a_c, 1.0, 0.0).astype(BF16))
            lhs = jnp.concatenate(onehots, axis=0)
            res = _dot(lhs, h2_ref[0, toks, :])
            res_aux = _dot(lhs, gate_ref[0, toks, :])
            for le, (a_c, lo_i, hi_i) in enumerate(wins):
                rows = pl.ds(a_c, win)
                mine = slice(le * win, (le + 1) * win)
                xs_ref[0, le, rows, 0:d] = jnp.where(row_d + a_c < lo_i, xs_ref[0, le, rows, 0:d],
                                                     res[mine].astype(BF16))
                xs_ref[0, le, rows, d:d + AUX] = jnp.where(row_aux + a_c < lo_i,
                                                           xs_ref[0, le, rows, d:d + AUX],
                                                           res_aux[mine].astype(BF16))
            return carry

        lax.fori_loop(0, n_pass, one_pass, 0)


def _gather(starts, pos, gate3, h2, cap, win, eg):
    b, t, d = h2.shape
    ne = pos.shape[1]
    ntb = t // TOK_BLOCK
    pos4 = pos.reshape(b, ne, ntb, TOK_BLOCK)
    gate_t = jnp.pad(jnp.swapaxes(gate3, 1, 2).astype(BF16), ((0, 0), (0, 0), (0, AUX - gate3.shape[1])))
    step = _step_tokens(t, GATHER_STEP_BLOCKS)
    grid_spec = pltpu.PrefetchScalarGridSpec(
        num_scalar_prefetch=1,
        grid=(b, ne // eg, t // step),
        in_specs=[pl.BlockSpec((1, eg, ntb, TOK_BLOCK), lambda bi, g, j, s: (bi, g, 0, 0)),
                  pl.BlockSpec((1, step, AUX), lambda bi, g, j, s: (bi, j, 0)),
                  pl.BlockSpec((1, step, d), lambda bi, g, j, s: (bi, j, 0))],
        out_specs=pl.BlockSpec((1, eg, cap, d + AUX), lambda bi, g, j, s: (bi, g, 0, 0)))
    return pl.pallas_call(
        functools.partial(_gather_kernel, win),
        out_shape=jax.ShapeDtypeStruct((b, ne, cap, d + AUX), BF16),
        grid_spec=grid_spec,
        compiler_params=pltpu.CompilerParams(
            dimension_semantics=("arbitrary", "arbitrary", "arbitrary"), vmem_limit_bytes=VMEM_LIMIT),
        name="gather",
    )(starts.reshape(-1), pos4, gate_t, h2)


FF_SUB = 256


def _ffn_kernel(n_tiles, xs_ref, wg_ref, wu_ref, wd_ref, gf_ref, ye_ref, acc_ref):
    bg = pl.program_id(0)
    e = pl.program_id(1)
    f = pl.program_id(2)
    ne = pl.num_programs(1)
    ns, cap, d = ye_ref.shape[0], ye_ref.shape[2], ye_ref.shape[3]
    xs = xs_ref[:, 0, :, 0:d].reshape(ns * cap, d)
    tf = wg_ref.shape[2]

    def run(first, last):
        acc = None if first else acc_ref[...]
        for c0 in range(0, tf, FF_SUB):
            a = _dot(xs, wg_ref[0, :, c0:c0 + FF_SUB].astype(BF16))
            u = _dot(xs, wu_ref[0, :, c0:c0 + FF_SUB].astype(BF16))
            down = _dot((_silu(a) * u).astype(BF16), wd_ref[0, c0:c0 + FF_SUB, :].astype(BF16))
            acc = down if acc is None else acc + down
        if not last:
            acc_ref[...] = acc
            return
        for s in range(ns):
            aux = xs_ref[s, 0, :, d:d + AUX].astype(F32)
            lane = lax.broadcasted_iota(I32, aux.shape, 1)
            mine = jnp.where(lane < 3 * ne, jnp.where(lane % ne == e, aux, 0.0), 0.0)
            gate = jnp.sum(mine, axis=-1, keepdims=True)
            ye_ref[s, 0] = (acc[s * cap:(s + 1) * cap] * gate
                            * gf_ref[pl.ds(bg * ns + s, 1), :]).astype(BF16)

    if n_tiles == 1:
        run(True, True)
    else:
        pl.when(f == 0)(functools.partial(run, True, False))
        if n_tiles > 2:
            pl.when(jnp.logical_and(f > 0, f < n_tiles - 1))(functools.partial(run, False, False))
        pl.when(f == n_tiles - 1)(functools.partial(run, False, True))


def _ffn(xs, w_gate, w_up, w_down, mod, tf, ns):
    b, ne, cap, daux = xs.shape
    d = daux - AUX
    ff = w_gate.shape[2]
    return pl.pallas_call(
        functools.partial(_ffn_kernel, ff // tf),
        out_shape=jax.ShapeDtypeStruct((b, ne, cap, d), BF16),
        grid=(b // ns, ne, ff // tf),
        in_specs=[pl.BlockSpec((ns, 1, cap, daux), lambda bi, e, f: (bi, e, 0, 0)),
                  pl.BlockSpec((1, d, tf), lambda bi, e, f: (e, 0, f)),
                  pl.BlockSpec((1, d, tf), lambda bi, e, f: (e, 0, f)),
                  pl.BlockSpec((1, tf, d), lambda bi, e, f: (e, f, 0)),
                  pl.BlockSpec((mod.shape[0], d), lambda bi, e, f: (0, N_MOD - 1))],
        out_specs=pl.BlockSpec((ns, 1, cap, d), lambda bi, e, f: (bi, e, 0, 0)),
        scratch_shapes=[pltpu.VMEM((ns * cap, d), F32)],
        compiler_params=pltpu.CompilerParams(dimension_semantics=("arbitrary", "arbitrary", "arbitrary"),
                                             vmem_limit_bytes=VMEM_LIMIT),
        name="ffn",
    )(xs, w_gate, w_up, w_down, mod)


def _combine_kernel(win, starts_ref, post_ref, ye_ref, x1_ref, fw_ref, out_ref, acc_ref, stack_ref):
    b = pl.program_id(0)
    ne, cap = ye_ref.shape[1], ye_ref.shape[2]
    row = lax.broadcasted_iota(I32, (win, TOK_BLOCK), 0)
    n_blk = x1_ref.shape[1] // TOK_BLOCK

    for blk in range(n_blk):
        j = pl.program_id(1) * n_blk + blk
        toks = slice(blk * TOK_BLOCK, (blk + 1) * TOK_BLOCK)
        acc_ref[...] = x1_ref[0, toks, :]
        s_lo, s_hi, a0, n_pass = _slot_windows(starts_ref, b, 0, ne, ne, j, win)

        def one_pass(p, carry, j=j, s_lo=s_lo, s_hi=s_hi, a0=a0):
            onehots = []
            for e in range(ne):
                a_c, lo_i, hi_i = _window(s_lo[e], s_hi[e], a0[e], p, win, cap)
                prow = post_ref[0, e, pl.ds(j, 1), :]
                owned = jnp.where(prow >= lo_i, jnp.where(prow < hi_i, prow, -1), -1)
                onehots.append(jnp.where(owned == row + a_c, 1.0, 0.0).astype(BF16))
                stack_ref[e * win:(e + 1) * win, :] = ye_ref[0, e, pl.ds(a_c, win), :]
            acc_ref[...] += _dot_tn(jnp.concatenate(onehots, axis=0), stack_ref[...])
            return carry

        lax.fori_loop(0, n_pass, one_pass, 0)
        out_ref[0, toks, :] = _rms(acc_ref[...], fw_ref[...])


def _combine(starts, pos, ye, x1, final_w, win):
    b, t, d = x1.shape
    ne, cap = ye.shape[1], ye.shape[2]
    ntb = t // TOK_BLOCK
    pos4 = pos.reshape(b, ne, ntb, TOK_BLOCK)
    step = _step_tokens(t, COMBINE_STEP_BLOCKS)
    grid_spec = pltpu.PrefetchScalarGridSpec(
        num_scalar_prefetch=1,
        grid=(b, t // step),
        in_specs=[pl.BlockSpec((1, ne, ntb, TOK_BLOCK), lambda bi, j, s: (bi, 0, 0, 0)),
                  pl.BlockSpec((1, ne, cap, d), lambda bi, j, s: (bi, 0, 0, 0),
                               pipeline_mode=pl.Buffered(1)),
                  pl.BlockSpec((1, step, d), lambda bi, j, s: (bi, j, 0)),
                  pl.BlockSpec((1, d), lambda bi, j, s: (0, 0))],
        out_specs=pl.BlockSpec((1, step, d), lambda bi, j, s: (bi, j, 0)),
        scratch_shapes=[pltpu.VMEM((TOK_BLOCK, d), F32), pltpu.VMEM((ne * win, d), BF16)])
    return pl.pallas_call(
        functools.partial(_combine_kernel, win),
        out_shape=jax.ShapeDtypeStruct((b, t, d), F32),
        grid_spec=grid_spec,
        compiler_params=pltpu.CompilerParams(
            dimension_semantics=("arbitrary", "arbitrary"), vmem_limit_bytes=COMBINE_VMEM_LIMIT),
        name="combine",
    )(starts.reshape(-1), pos4, ye, x1, final_w.reshape(1, d))


def kernel(x, c, ctx, c_ctx, ada_w, ada_b, norm_mix_w, norm_ffn_w, w_in, hgrn_lb_logits, hgrn_norm_w,
           sgu_norm_w, sgu_w, sgu_b, w_branch_a, w_branch_b, w_out, router_w, expert_w_gate,
           expert_w_up, expert_w_down, final_norm_w):
    b, t, d = x.shape
    assert b + 1 <= 8 and t % TOK_BLOCK == 0 and d % LANES == 0
    ne = router_w.shape[-1]
    cap = CAPACITY_FACTOR * t // ne
    win = min(cap, 64)
    assert cap % SLOT_ALIGN == 0 and t // LANES < LANES and 3 * ne <= AUX
    (x1, h2, aff_t), mod = _front(x, c, ctx, c_ctx, ada_w, ada_b, norm_mix_w, norm_ffn_w, w_in,
                                  hgrn_lb_logits, hgrn_norm_w, sgu_norm_w, sgu_w, sgu_b,
                                  w_branch_a, w_branch_b, w_out, router_w, 512)
    pos, gate, starts = _route(aff_t, cap)
    xs = _gather(starts, pos, gate, h2, cap, win, 8)
    ye = _ffn(xs, expert_w_gate[0], expert_w_up[0], expert_w_down[0], mod, 1024, 1)
    return _combine(starts, pos, ye, x1, final_norm_w, win)
```
